```python
import numpy as np
import jax, jax.numpy as jnp
from jax import lax

D_MODEL = 1024
BATCH = 8
SEQ = 2048
DEPTH = 4

D_MIX = D_MODEL
D_FF = 2816
NSA_HEADS = 8
NSA_KV_GROUPS = 2
NSA_HEAD_DIM = 64
NSA_GROUP_SIZE = NSA_HEADS // NSA_KV_GROUPS
NSA_OUT = NSA_HEADS * NSA_HEAD_DIM
NSA_KV_DIM = NSA_KV_GROUPS * NSA_HEAD_DIM
CMP_BLOCK = 32
CMP_STRIDE = 16
SLC_BLOCK = 64
SLC_TOPK = 8
WINDOW = 512
Q_BLOCK = 128
POOL_WINDOWS = (2, 4, 8, 16)
POOL_GROUPS = 4
POOL_GROUP_DIM = 64
POOL_DIM = POOL_GROUPS * POOL_GROUP_DIM
GLA_HEADS = 4
GLA_KEY_DIM = 32
GLA_VAL_DIM = 64
GLA_GATE_RANK = 16
GLA_TAU = 16.0
GLA_CHUNK = 64
GLA_QK = GLA_HEADS * GLA_KEY_DIM
GLA_OUT = GLA_HEADS * GLA_VAL_DIM
IN_SPLITS = (NSA_OUT, NSA_KV_DIM, NSA_KV_DIM, NSA_KV_DIM, NSA_KV_DIM, NSA_KV_DIM, NSA_KV_DIM,
             3 * NSA_HEADS, POOL_DIM, GLA_QK, GLA_QK, GLA_OUT, GLA_GATE_RANK, GLA_OUT)
D_IN = NSA_OUT + 6 * NSA_KV_DIM + 3 * NSA_HEADS + POOL_DIM + 2 * GLA_QK + 2 * GLA_OUT + GLA_GATE_RANK
ALPHA = (2.0 * DEPTH) ** 0.25
BETA = (8.0 * DEPTH) ** -0.25
LN_EPS = 1e-5
RMS_EPS = 1e-6

kernel_name = "hymba_nsa_pool_gla_macaron_deepnorm"


def layer_norm(x, g, b):
    xf = x.astype(jnp.float32)
    mu = jnp.mean(xf, axis=-1, keepdims=True)
    var = jnp.mean(jnp.square(xf - mu), axis=-1, keepdims=True)
    y = (xf - mu) * lax.rsqrt(var + LN_EPS)
    return (y * g + b).astype(x.dtype)


def swiglu(x, wg, wu, wd):
    return (jax.nn.silu(x @ wg) * (x @ wu)) @ wd


def masked_softmax(s, mask):
    s = jnp.where(mask, s.astype(jnp.float32), -jnp.inf)
    m = jnp.max(s, axis=-1, keepdims=True)
    m = jnp.where(jnp.isfinite(m), m, 0.0)
    e = jnp.exp(s - m)
    d = jnp.sum(e, axis=-1, keepdims=True)
    return e / jnp.where(d > 0, d, 1.0)


def compress_tokens(z, pe, w1, w2):
    B, S, G, D = z.shape
    chunks = z.reshape(B, S // CMP_STRIDE, CMP_STRIDE, G, D)
    blocks = jnp.concatenate([chunks[:, :-1], chunks[:, 1:]], axis=2)
    blocks = blocks + pe[None, None, :, None, :]
    flat = jnp.transpose(blocks, (0, 1, 3, 2, 4)).reshape(B, -1, G, CMP_BLOCK * D)
    return jax.nn.gelu(flat @ w1) @ w2


def nsa_mixer(q, kc_raw, vc_raw, ks, vs, kw, vw, gates, pe_k, w1_k, w2_k, pe_v, w1_v, w2_v):
    B, S = q.shape[:2]
    G, R, HD = NSA_KV_GROUPS, NSA_GROUP_SIZE, NSA_HEAD_DIM
    q = q.reshape(B, S, G, R, HD) * (HD ** -0.5)
    kc = compress_tokens(kc_raw, pe_k, w1_k, w2_k)
    vc = compress_tokens(vc_raw, pe_v, w1_v, w2_v)
    n_cmp = kc.shape[1]
    cmp_end = jnp.arange(n_cmp) * CMP_STRIDE + CMP_BLOCK - 1
    n_slc = S // SLC_BLOCK
    topk = min(SLC_TOPK, n_slc)
    c_start = jnp.arange(n_cmp)[:, None] * CMP_STRIDE
    s_start = jnp.arange(n_slc)[None, :] * SLC_BLOCK
    overlap = ((c_start <= s_start + SLC_BLOCK - 1) & (c_start + CMP_BLOCK - 1 >= s_start)).astype(jnp.float32)
    ks_blk = jnp.transpose(ks.reshape(B, n_slc, SLC_BLOCK, G, HD), (0, 3, 1, 2, 4))
    vs_blk = jnp.transpose(vs.reshape(B, n_slc, SLC_BLOCK, G, HD), (0, 3, 1, 2, 4))
    pad = ((0, 0), (WINDOW, 0), (0, 0), (0, 0))
    kw_pad = jnp.pad(kw, pad)
    vw_pad = jnp.pad(vw, pad)
    n_qb = S // Q_BLOCK
    q_blocks = jnp.swapaxes(q.reshape(B, n_qb, Q_BLOCK, G, R, HD), 0, 1)
    g_blocks = jnp.swapaxes(gates.reshape(B, n_qb, Q_BLOCK, G, R, 3), 0, 1)
    blk_ids = jnp.arange(n_slc)
    gather = jax.vmap(jax.vmap(lambda kb, ix: kb[ix]))

    def one_block(args):
        qi, qb, gb = args
        t = qi * Q_BLOCK + jnp.arange(Q_BLOCK)
        s_c = jnp.einsum('bqgrd,bngd->bgrqn', qb, kc)
        p_c = masked_softmax(s_c, cmp_end[None, :] <= t[:, None])
        o_c = jnp.einsum('bgrqn,bngd->bqgrd', p_c.astype(vc.dtype), vc)
        imp = jnp.einsum('bgrqn,nm->bgqm', p_c, overlap)
        cur = (t // SLC_BLOCK)[:, None]
        forced = (blk_ids == 0) | (blk_ids == cur) | (blk_ids == cur - 1)
        future = blk_ids[None, :] * SLC_BLOCK > t[:, None]
        imp = jnp.where(forced, jnp.inf, jnp.where(future, -jnp.inf, imp))
        _, sel = lax.top_k(imp, topk)
        k_sel = gather(ks_blk, sel).reshape(B, G, Q_BLOCK, topk * SLC_BLOCK, HD)
        v_sel = gather(vs_blk, sel).reshape(B, G, Q_BLOCK, topk * SLC_BLOCK, HD)
        kpos = (sel[..., None] * SLC_BLOCK + jnp.arange(SLC_BLOCK)).reshape(B, G, Q_BLOCK, topk * SLC_BLOCK)
        s_s = jnp.einsum('bqgrd,bgqkd->bgrqk', qb, k_sel)
        p_s = masked_softmax(s_s, (kpos <= t[:, None])[:, :, None])
        o_s = jnp.einsum('bgrqk,bgqkd->bqgrd', p_s.astype(v_sel.dtype), v_sel)
        start = qi * Q_BLOCK
        k_win = lax.dynamic_slice_in_dim(kw_pad, start, Q_BLOCK + WINDOW, axis=1)
        v_win = lax.dynamic_slice_in_dim(vw_pad, start, Q_BLOCK + WINDOW, axis=1)
        kp = start - WINDOW + jnp.arange(Q_BLOCK + WINDOW)
        mask_w = (kp[None, :] <= t[:, None]) & (kp[None, :] > t[:, None] - WINDOW) & (kp[None, :] >= 0)
        s_w = jnp.einsum('bqgrd,bkgd->bgrqk', qb, k_win)
        p_w = masked_softmax(s_w, mask_w)
        o_w = jnp.einsum('bgrqk,bkgd->bqgrd', p_w.astype(v_win.dtype), v_win)
        o = gb[..., 0:1] * o_c + gb[..., 1:2] * o_s + gb[..., 2:3] * o_w
        return o.reshape(B, Q_BLOCK, G * R * HD)

    out = lax.map(one_block, (jnp.arange(n_qb), q_blocks, g_blocks))
    return jnp.swapaxes(out, 0, 1).reshape(B, S, NSA_OUT)


def pool_mixer(u, pool_w, pool_scale):
    B, S, _ = u.shape
    uf = u.astype(jnp.float32).reshape(B, S, POOL_GROUPS, POOL_GROUP_DIM)
    c = jnp.pad(jnp.cumsum(uf, axis=1), ((0, 0), (1, 0), (0, 0), (0, 0)))
    t = jnp.arange(S)
    means = []
    for gi, w in enumerate(POOL_WINDOWS):
        cg = c[:, :, gi]
        lower = jnp.pad(cg, ((0, 0), (w, 0), (0, 0)))[:, 1:S + 1]
        cnt = jnp.minimum(w, t + 1).astype(jnp.float32)[None, :, None]
        means.append((cg[:, 1:] - lower) / cnt)
    pooled = (jnp.stack(means, axis=2) - uf).astype(u.dtype)
    y = jnp.einsum('bsgc,gcd->bsgd', pooled, pool_w).reshape(B, S, POOL_DIM)
    return y * pool_scale


def gla_mixer(q, k, v, a_lr, r, wa2, ba, norm_g):
    B, S = q.shape[:2]
    H, DK, DV, C = GLA_HEADS, GLA_KEY_DIM, GLA_VAL_DIM, GLA_CHUNK
    nc = S // C
    log_a = jax.nn.log_sigmoid((a_lr @ wa2 + ba).astype(jnp.float32)) / GLA_TAU
    qf = q.astype(jnp.float32).reshape(B, nc, C, H, DK) * (DK ** -0.5)
    kf = k.astype(jnp.float32).reshape(B, nc, C, H, DK)
    vf = v.astype(jnp.float32).reshape(B, nc, C, H, DV)
    b = jnp.cumsum(log_a.reshape(B, nc, C, H, DK), axis=2)
    b_last = b[:, :, -1:]
    q_t = qf * jnp.exp(b)
    k_t = kf * jnp.exp(-b)
    causal = jnp.tril(jnp.ones((C, C), dtype=bool))
    A = jnp.where(causal, jnp.einsum('bnihk,bnjhk->bnhij', q_t, k_t), 0.0)
    o_intra = jnp.einsum('bnhij,bnjhv->bnihv', A, vf)
    dS = jnp.einsum('bnjhk,bnjhv->nbhkv', kf * jnp.exp(b_last - b), vf)
    decay = jnp.swapaxes(jnp.exp(b_last[:, :, 0]), 0, 1)

    def step(state, inp):
        d, ds = inp
        return d[..., None] * state + ds, state

    _, s_in = lax.scan(step, jnp.zeros((B, H, DK, DV), jnp.float32), (decay, dS))
    o_inter = jnp.einsum('bnihk,nbhkv->bnihv', q_t, s_in)
    o = (o_intra + o_inter).reshape(B, S, H, DV)
    o = o * lax.rsqrt(jnp.mean(jnp.square(o), axis=-1, keepdims=True) + RMS_EPS)
    o = o.reshape(B, S, GLA_OUT) * norm_g
    return (o * jax.nn.silu(r.astype(jnp.float32))).astype(q.dtype)


def token_mix(x, w_in, w_out, cmp_pe, cmp_w1, cmp_w2, pool_w, pool_scale, gla_wa2, gla_ba, gla_norm_g):
    B, S, _ = x.shape
    h = x @ w_in
    split_pts = np.cumsum(IN_SPLITS)[:-1].tolist()
    (nq, kc, vc, ks, vs, kw, vw, gl, u, gq, gk, gv, ga, gr) = jnp.split(h, split_pts, axis=-1)
    kv = lambda z: z.reshape(B, S, NSA_KV_GROUPS, NSA_HEAD_DIM)
    gates = jax.nn.sigmoid(gl).reshape(B, S, NSA_HEADS, 3)
    o_a = nsa_mixer(nq, kv(kc), kv(vc), kv(ks), kv(vs), kv(kw), kv(vw), gates,
                    cmp_pe[0], cmp_w1[0], cmp_w2[0], cmp_pe[1], cmp_w1[1], cmp_w2[1])
    o_b = pool_mixer(u, pool_w, pool_scale)
    o_c = gla_mixer(gq, gk, gv, ga, gr, gla_wa2, gla_ba, gla_norm_g)
    return jnp.concatenate([o_a, o_b, o_c], axis=-1) @ w_out


def setup_inputs(seed: int = 0) -> dict:
    key = jax.random.key(seed)
    ks = jax.random.split(key, 16)
    nrm = lambda k, shape, s: jax.random.normal(k, shape, jnp.float32) * s
    L = DEPTH
    return {
        "x": nrm(ks[0], (BATCH, SEQ, D_MODEL), 1.0),
        "ln_g": 1.0 + nrm(ks[1], (L, 3, D_MODEL), 0.05),
        "ln_b": nrm(ks[2], (L, 3, D_MODEL), 0.01),
        "ffn_wg": nrm(ks[3], (L, 2, D_MODEL, D_FF), D_MODEL ** -0.5),
        "ffn_wu": nrm(ks[4], (L, 2, D_MODEL, D_FF), D_MODEL ** -0.5),
        "ffn_wd": nrm(ks[5], (L, 2, D_FF, D_MODEL), BETA * D_FF ** -0.5),
        "w_in": nrm(ks[6], (L, D_MODEL, D_IN), D_MODEL ** -0.5),
        "w_out": nrm(ks[7], (L, D_MIX, D_MODEL), BETA * D_MIX ** -0.5),
        "cmp_pe": nrm(ks[8], (L, 2, CMP_BLOCK, NSA_HEAD_DIM), 0.1),
        "cmp_w1": nrm(ks[9], (L, 2, CMP_BLOCK * NSA_HEAD_DIM, NSA_HEAD_DIM), (CMP_BLOCK * NSA_HEAD_DIM) ** -0.5),
        "cmp_w2": nrm(ks[10], (L, 2, NSA_HEAD_DIM, NSA_HEAD_DIM), NSA_HEAD_DIM ** -0.5),
        "pool_w": nrm(ks[11], (L, POOL_GROUPS, POOL_GROUP_DIM, POOL_GROUP_DIM), POOL_GROUP_DIM ** -0.5),
        "pool_scale": 1.0 + nrm(ks[12], (L, POOL_DIM), 0.1),
        "gla_wa2": nrm(ks[13], (L, GLA_GATE_RANK, GLA_QK), GLA_GATE_RANK ** -0.5),
        "gla_ba": nrm(ks[14], (L, GLA_QK), 0.1),
        "gla_norm_g": 1.0 + nrm(ks[15], (L, GLA_OUT), 0.05),
    }


def reference(x, ln_g, ln_b, ffn_wg, ffn_wu, ffn_wd, w_in, w_out, cmp_pe, cmp_w1, cmp_w2,
              pool_w, pool_scale, gla_wa2, gla_ba, gla_norm_g):
    for l in range(DEPTH):
        x = layer_norm(ALPHA * x + 0.5 * swiglu(x, ffn_wg[l, 0], ffn_wu[l, 0], ffn_wd[l, 0]), ln_g[l, 0], ln_b[l, 0])
        m = token_mix(x, w_in[l], w_out[l], cmp_pe[l], cmp_w1[l], cmp_w2[l], pool_w[l], pool_scale[l],
                      gla_wa2[l], gla_ba[l], gla_norm_g[l])
        x = layer_norm(ALPHA * x + m, ln_g[l, 1], ln_b[l, 1])
        x = layer_norm(ALPHA * x + 0.5 * swiglu(x, ffn_wg[l, 1], ffn_wu[l, 1], ffn_wd[l, 1]), ln_g[l, 2], ln_b[l, 2])
    return x
```

```python
import functools

import numpy as np
import jax
import jax.numpy as jnp
from jax import lax
from jax.experimental import pallas as pl
from jax.experimental.pallas import tpu as pltpu

F32 = jnp.float32
BF16 = jnp.bfloat16

D_MODEL = 1024
DEPTH = 4
D_FF = 2816
NSA_HEADS = 8
NSA_KV_GROUPS = 2
NSA_HEAD_DIM = 64
NSA_GROUP_SIZE = NSA_HEADS // NSA_KV_GROUPS
NSA_OUT = NSA_HEADS * NSA_HEAD_DIM
NSA_KV_DIM = NSA_KV_GROUPS * NSA_HEAD_DIM
CMP_BLOCK = 32
CMP_STRIDE = 16
SLC_BLOCK = 64
SLC_TOPK = 8
WINDOW = 512
Q_BLOCK = 128
POOL_WINDOWS = (2, 4, 8, 16)
POOL_GROUPS = 4
POOL_GROUP_DIM = 64
POOL_DIM = POOL_GROUPS * POOL_GROUP_DIM
GLA_HEADS = 4
GLA_KEY_DIM = 32
GLA_VAL_DIM = 64
GLA_GATE_RANK = 16
GLA_TAU = 16.0
GLA_CHUNK = 64
GLA_QK = GLA_HEADS * GLA_KEY_DIM
GLA_OUT = GLA_HEADS * GLA_VAL_DIM
ALPHA = (2.0 * DEPTH) ** 0.25
LN_EPS = 1e-5
RMS_EPS = 1e-6

LANES = 128
NEG = -1e30
VMEM_LIMIT = 56 * 1024 * 1024

_IN_LAYOUT = {
    "qkv": (0, 1280, 0),
    "u": (1304, 256, 1280),
    "gv": (1816, 256, 1536),
    "gr": (2088, 256, 1792),
    "gq": (1560, 128, 2048),
    "gk": (1688, 128, 2176),
    "gl": (1280, 24, 2304),
    "ga": (2072, 16, 2432),
}
D_IN_PAD = 2560


def _dot(a, b):
    return jnp.dot(a, b, preferred_element_type=F32)


def _dot_nt(a, b):
    return lax.dot_general(a, b, (((1,), (1,)), ((), ())), preferred_element_type=F32)


def _layer_norm(y, g, b):
    mu = jnp.mean(y, axis=-1, keepdims=True)
    yc = y - mu
    var = jnp.mean(yc * yc, axis=-1, keepdims=True)
    return yc * lax.rsqrt(var + LN_EPS) * g + b


def _split3(x):
    hi = x.astype(BF16)
    r1 = x - hi.astype(F32)
    mid = r1.astype(BF16)
    lo = (r1 - mid.astype(F32)).astype(BF16)
    return hi, mid, lo


def _ffn_kernel(x_ref, wg_ref, wu_ref, wd_ref, g_ref, b_ref, o_ref, xb_ref, acc_ref):
    j = pl.program_id(1)

    @pl.when(j == 0)
    def _():
        xb_ref[...] = x_ref[...].astype(BF16)
        acc_ref[...] = jnp.zeros_like(acc_ref)

    xb = xb_ref[...]
    gate = _dot(xb, wg_ref[...])
    up = _dot(xb, wu_ref[...])
    act = (gate * jax.nn.sigmoid(gate)) * up
    acc_ref[...] += _dot(act.astype(BF16), wd_ref[...])

    @pl.when(j == pl.num_programs(1) - 1)
    def _():
        y = ALPHA * x_ref[...] + 0.5 * acc_ref[...]
        o_ref[...] = _layer_norm(y, g_ref[...], b_ref[...])


def _ffn_ln(x, wg, wu, wd, g, b, *, tm=1024, tf=256):
    n = x.shape[0]
    grid = (n // tm, D_FF // tf)
    return pl.pallas_call(
        _ffn_kernel,
        grid=grid,
        in_specs=[
            pl.BlockSpec((tm, D_MODEL), lambda i, j: (i, 0)),
            pl.BlockSpec((D_MODEL, tf), lambda i, j: (0, j)),
            pl.BlockSpec((D_MODEL, tf), lambda i, j: (0, j)),
            pl.BlockSpec((tf, D_MODEL), lambda i, j: (j, 0)),
            pl.BlockSpec((1, D_MODEL), lambda i, j: (0, 0)),
            pl.BlockSpec((1, D_MODEL), lambda i, j: (0, 0)),
        ],
        out_specs=pl.BlockSpec((tm, D_MODEL), lambda i, j: (i, 0)),
        out_shape=jax.ShapeDtypeStruct((n, D_MODEL), F32),
        scratch_shapes=[pltpu.VMEM((tm, D_MODEL), BF16), pltpu.VMEM((tm, D_MODEL), F32)],
        compiler_params=pltpu.CompilerParams(
            dimension_semantics=("parallel", "arbitrary"), vmem_limit_bytes=VMEM_LIMIT),
        name="ffn_ln",
    )(x, wg, wu, wd, g, b)


def _inproj_kernel(x_ref, w_ref, o_ref):
    o_ref[...] = _dot(x_ref[...].astype(BF16), w_ref[...])


def _in_proj(x, w_pad, *, tm=512):
    n = x.shape[0]
    return pl.pallas_call(
        _inproj_kernel,
        grid=(n // tm,),
        in_specs=[
            pl.BlockSpec((tm, D_MODEL), lambda i: (i, 0)),
            pl.BlockSpec((D_MODEL, D_IN_PAD), lambda i: (0, 0)),
        ],
        out_specs=pl.BlockSpec((tm, D_IN_PAD), lambda i: (i, 0)),
        out_shape=jax.ShapeDtypeStruct((n, D_IN_PAD), F32),
        compiler_params=pltpu.CompilerParams(
            dimension_semantics=("parallel",), vmem_limit_bytes=VMEM_LIMIT),
        name="in_proj",
    )(x, w_pad)


N_CMP_PAD = 128


def _gelu_tanh(x):
    return 0.5 * x * (1.0 + jnp.tanh(np.sqrt(2.0 / np.pi) * (x + 0.044715 * (x * x * x))))


def _compress_kernel(zk_ref, zv_ref, pe_ref, w1_ref, w2_ref, ok_ref, ov_ref):
    def one(z_ref, which, o_ref):
        slabs = [z_ref[pl.ds(q, N_CMP_PAD, stride=CMP_STRIDE), :] for q in range(CMP_STRIDE)]
        cat = jnp.concatenate(slabs, axis=1)
        half = CMP_STRIDE * LANES
        top = _dot((cat + pe_ref[which, :, :half]).astype(BF16), w1_ref[which, :half, :])
        bot = _dot((cat + pe_ref[which, :, half:]).astype(BF16), w1_ref[which, half:, :])
        pre = top + pltpu.roll(bot, N_CMP_PAD - 1, 0)
        o_ref[...] = _dot(_gelu_tanh(pre).astype(BF16), w2_ref[which])

    one(zk_ref, 0, ok_ref)
    one(zv_ref, 1, ov_ref)


def _compress(h, pe_rows, w1_bd, w2_bd, batch, seq):
    kc_blk = 512 // LANES
    return pl.pallas_call(
        _compress_kernel,
        grid=(batch,),
        in_specs=[
            pl.BlockSpec((seq, LANES), lambda b: (b, kc_blk)),
            pl.BlockSpec((seq, LANES), lambda b: (b, kc_blk + 1)),
            pl.BlockSpec((2, 1, CMP_BLOCK * LANES), lambda b: (0, 0, 0)),
            pl.BlockSpec((2, CMP_BLOCK * LANES, LANES), lambda b: (0, 0, 0)),
            pl.BlockSpec((2, LANES, LANES), lambda b: (0, 0, 0)),
        ],
        out_specs=[
            pl.BlockSpec((N_CMP_PAD, LANES), lambda b: (b, 0)),
            pl.BlockSpec((N_CMP_PAD, LANES), lambda b: (b, 0)),
        ],
        out_shape=[jax.ShapeDtypeStruct((batch * N_CMP_PAD, LANES), F32)] * 2,
        compiler_params=pltpu.CompilerParams(
            dimension_semantics=("parallel",), vmem_limit_bytes=VMEM_LIMIT),
        name="nsa_compress",
    )(h, h, pe_rows, w1_bd, w2_bd)


N_SLC = 32
WIN_KEYS = WINDOW + Q_BLOCK


def _softmax_parts(s):
    m = jnp.max(s, axis=-1, keepdims=True)
    m = jnp.where(m > 0.5 * NEG, m, 0.0)
    e = jnp.exp(s - m)
    d = jnp.sum(e, axis=-1, keepdims=True)
    inv = 1.0 / jnp.where(d > 0.0, d, 1.0)
    return e, inv


def _nsa_kernel(q_ref, kc_ref, vc_ref, ksvs_ref, kwvw_ref, gl_ref, o_ref, *, seq):
    qi = pl.program_id(1)
    start = qi * Q_BLOCK
    hd = NSA_HEAD_DIM
    t_col = start + lax.broadcasted_iota(jnp.int32, (Q_BLOCK, 1), 0)
    t_row = start + lax.broadcasted_iota(jnp.int32, (1, Q_BLOCK), 1)

    n_lane = lax.broadcasted_iota(jnp.int32, (1, N_CMP_PAD), 1)
    bias_c = jnp.where(n_lane * CMP_STRIDE + (CMP_BLOCK - 1) <= t_col, 0.0, NEG)

    m_sub = lax.broadcasted_iota(jnp.int32, (N_SLC, N_CMP_PAD), 0)
    n_l2 = lax.broadcasted_iota(jnp.int32, (N_SLC, N_CMP_PAD), 1)
    c0 = n_l2 * CMP_STRIDE
    s0 = m_sub * SLC_BLOCK
    ov_t = ((c0 <= s0 + SLC_BLOCK - 1) & (c0 + CMP_BLOCK - 1 >= s0)
            & (n_l2 < seq // CMP_STRIDE - 1)).astype(BF16)

    kpos = lax.broadcasted_iota(jnp.int32, (1, seq), 1)
    causal_s = kpos <= t_col
    e_rows = lax.broadcasted_iota(jnp.int32, (LANES, seq), 0)
    e_cols = lax.broadcasted_iota(jnp.int32, (LANES, seq), 1)
    expand = (e_cols // SLC_BLOCK == e_rows).astype(BF16)
    m_idx = lax.broadcasted_iota(jnp.int32, (N_SLC, Q_BLOCK), 0)
    cur = t_row // SLC_BLOCK
    forced = (m_idx == 0) | (m_idx == cur) | (m_idx == cur - 1)
    future = m_idx * SLC_BLOCK > t_row

    w0 = pl.multiple_of(jnp.maximum(start - WINDOW, 0), Q_BLOCK)
    kp_w = w0 + lax.broadcasted_iota(jnp.int32, (1, WIN_KEYS), 1)
    bias_w = jnp.where((kp_w <= t_col) & (kp_w > t_col - WINDOW), 0.0, NEG)

    gates = jax.nn.sigmoid(gl_ref[...])
    q_all = q_ref[...] * (hd ** -0.5)
    kwvw = kwvw_ref[pl.ds(w0, WIN_KEYS), :]

    outs = []
    for g in range(NSA_KV_GROUPS):
        kc = kc_ref[:, g * hd:(g + 1) * hd].astype(BF16)
        vc = vc_ref[:, g * hd:(g + 1) * hd].astype(BF16)
        ks = ksvs_ref[:, g * hd:(g + 1) * hd].astype(BF16)
        vs = ksvs_ref[:, NSA_KV_DIM + g * hd:NSA_KV_DIM + (g + 1) * hd].astype(BF16)
        kw = kwvw[:, g * hd:(g + 1) * hd].astype(BF16)
        vw = kwvw[:, NSA_KV_DIM + g * hd:NSA_KV_DIM + (g + 1) * hd].astype(BF16)

        qs = []
        o_cs = []
        p_sum = jnp.zeros((Q_BLOCK, N_CMP_PAD), F32)
        for r in range(NSA_GROUP_SIZE):
            h = g * NSA_GROUP_SIZE + r
            qh = q_all[:, h * hd:(h + 1) * hd].astype(BF16)
            qs.append(qh)
            e, inv = _softmax_parts(_dot_nt(qh, kc) + bias_c)
            p_c = e * inv
            p_sum = p_sum + p_c
            o_cs.append(_dot(p_c.astype(BF16), vc))

        hi, mid, lo = _split3(p_sum)
        imp = _dot_nt(ov_t, hi) + _dot_nt(ov_t, mid) + _dot_nt(ov_t, lo)
        imp = jnp.where(forced, -NEG, jnp.where(future, NEG, imp))
        rank = jnp.zeros((N_SLC, Q_BLOCK), jnp.int32)
        for m2 in range(N_SLC):
            row = imp[m2:m2 + 1, :]
            ahead = (row > imp) | ((row == imp) & (m2 < m_idx))
            rank = rank + ahead.astype(jnp.int32)
        sel_t = (rank < SLC_TOPK).astype(F32)
        sel_t = jnp.concatenate([sel_t, jnp.zeros((LANES - N_SLC, Q_BLOCK), F32)], axis=0)
        sel = sel_t.T.astype(BF16)
        picked = _dot(sel, expand)
        bias_s = jnp.where(causal_s & (picked > 0.5), 0.0, NEG)

        for r in range(NSA_GROUP_SIZE):
            h = g * NSA_GROUP_SIZE + r
            qh = qs[r]
            e, inv = _softmax_parts(_dot_nt(qh, ks) + bias_s)
            o_s = _dot(e.astype(BF16), vs) * inv
            e, inv = _softmax_parts(_dot_nt(qh, kw) + bias_w)
            o_w = _dot(e.astype(BF16), vw) * inv
            o = (gates[:, 3 * h:3 * h + 1] * o_cs[r] + gates[:, 3 * h + 1:3 * h + 2] * o_s
                 + gates[:, 3 * h + 2:3 * h + 3] * o_w)
            outs.append(o)
    o_ref[...] = jnp.concatenate(outs, axis=1).astype(o_ref.dtype)


def _nsa(h, kc, vc, batch, seq):
    nqb = seq // Q_BLOCK
    return pl.pallas_call(
        functools.partial(_nsa_kernel, seq=seq),
        grid=(batch, nqb),
        in_specs=[
            pl.BlockSpec((Q_BLOCK, NSA_OUT), lambda b, i: (b * nqb + i, 0)),
            pl.BlockSpec((N_CMP_PAD, LANES), lambda b, i: (b, 0)),
            pl.BlockSpec((N_CMP_PAD, LANES), lambda b, i: (b, 0)),
            pl.BlockSpec((seq, 2 * NSA_KV_DIM), lambda b, i: (b, 768 // 256)),
            pl.BlockSpec((seq, 2 * NSA_KV_DIM), lambda b, i: (b, 1024 // 256)),
            pl.BlockSpec((Q_BLOCK, LANES), lambda b, i: (b * nqb + i, 2304 // LANES)),
        ],
        out_specs=pl.BlockSpec((Q_BLOCK, NSA_OUT), lambda b, i: (b * nqb + i, 0)),
        out_shape=jax.ShapeDtypeStruct((batch * seq, NSA_OUT), BF16),
        compiler_params=pltpu.CompilerParams(
            dimension_semantics=("parallel", "arbitrary"), vmem_limit_bytes=VMEM_LIMIT),
        name="nsa_attn",
    )(h, kc, vc, h, h, h)


def _shift_rows(x, k, row):
    return jnp.where(row >= k, pltpu.roll(x, k, 0), 0.0)


def _pool_kernel(u_ref, w_ref, sc_ref, o_ref, *, seq):
    u = u_ref[...]
    row = lax.broadcasted_iota(jnp.int32, (seq, 1), 0)
    lane = lax.broadcasted_iota(jnp.int32, (1, POOL_DIM), 1)
    tp1 = (row + 1).astype(F32)
    acc = u
    mean = jnp.zeros_like(u)
    span = 1
    for gi, w in enumerate(POOL_WINDOWS):
        while span < w:
            acc = acc + _shift_rows(acc, span, row)
            span *= 2
        cnt = jnp.minimum(float(w), tp1)
        in_group = (lane >= gi * POOL_GROUP_DIM) & (lane < (gi + 1) * POOL_GROUP_DIM)
        mean = jnp.where(in_group, acc / cnt, mean)
    pooled = mean - u
    o_ref[...] = (_dot(pooled.astype(BF16), w_ref[...]) * sc_ref[...]).astype(o_ref.dtype)


def _pool(h, w_bd, scale, batch, seq):
    return pl.pallas_call(
        functools.partial(_pool_kernel, seq=seq),
        grid=(batch,),
        in_specs=[
            pl.BlockSpec((seq, POOL_DIM), lambda b: (b, 1280 // 256)),
            pl.BlockSpec((POOL_DIM, POOL_DIM), lambda b: (0, 0)),
            pl.BlockSpec((1, POOL_DIM), lambda b: (0, 0)),
        ],
        out_specs=pl.BlockSpec((seq, POOL_DIM), lambda b: (b, 0)),
        out_shape=jax.ShapeDtypeStruct((batch * seq, POOL_DIM), BF16),
        compiler_params=pltpu.CompilerParams(
            dimension_semantics=("parallel",), vmem_limit_bytes=VMEM_LIMIT),
        name="pool_mix",
    )(h, w_bd, scale)


def _gla_kernel(q_ref, k_ref, v_ref, a_ref, r_ref, wa2_ref, ba_ref, ng_ref, o_ref,
                qt_ref, kt_ref, kd_ref, dec_ref, oacc_ref, *, seq):
    C, H, DK, DV = GLA_CHUNK, GLA_HEADS, GLA_KEY_DIM, GLA_VAL_DIM
    nc = seq // C
    z = _dot(a_ref[...].astype(BF16), wa2_ref[...]) + ba_ref[...]
    log_a = (jnp.minimum(z, 0.0) - jnp.log(1.0 + jnp.exp(-jnp.abs(z)))) / GLA_TAU
    pos = lax.broadcasted_iota(jnp.int32, (seq, 1), 0) % C
    b = log_a
    step = 1
    while step < C:
        b = b + jnp.where(pos >= step, pltpu.roll(b, step, 0), 0.0)
        step *= 2
    b3 = b.reshape(nc, C, GLA_QK)
    b_last = b3[:, C - 1:C, :]
    qt_ref[...] = q_ref[...] * (DK ** -0.5) * jnp.exp(b)
    kt_ref[...] = k_ref[...] * jnp.exp(-b)
    kd_ref[...] = (k_ref[...].reshape(nc, C, GLA_QK) * jnp.exp(b_last - b3)).reshape(seq, GLA_QK)
    dec_ref[...] = jnp.exp(b_last)

    r_k = lax.broadcasted_iota(jnp.int32, (H * C, GLA_QK), 0) // C
    c_k = lax.broadcasted_iota(jnp.int32, (H * C, GLA_QK), 1) // DK
    mask_k = r_k == c_k
    r_v = lax.broadcasted_iota(jnp.int32, (H * C, GLA_OUT), 0) // C
    c_v = lax.broadcasted_iota(jnp.int32, (H * C, GLA_OUT), 1) // DV
    mask_v = r_v == c_v
    r_s = lax.broadcasted_iota(jnp.int32, (GLA_OUT, GLA_QK), 0) // DV
    c_s = lax.broadcasted_iota(jnp.int32, (GLA_OUT, GLA_QK), 1) // DK
    mask_s = r_s == c_s
    i_a = lax.broadcasted_iota(jnp.int32, (C, H * C), 0)
    j_a = lax.broadcasted_iota(jnp.int32, (C, H * C), 1) % C
    tril = j_a <= i_a

    def chunk(n, state_t):
        rows = pl.ds(pl.multiple_of(n * C, C), C)
        q_t = qt_ref[rows, :].astype(BF16)
        k_t = kt_ref[rows, :]
        k_d = kd_ref[rows, :].astype(BF16)
        v_c = v_ref[rows, :]
        k_bd = jnp.where(mask_k, jnp.concatenate([k_t] * H, axis=0), 0.0).astype(BF16)
        a_cat = jnp.where(tril, _dot_nt(q_t, k_bd), 0.0)
        v_bd = jnp.where(mask_v, jnp.concatenate([v_c] * H, axis=0), 0.0).astype(BF16)
        o_intra = _dot(a_cat.astype(BF16), v_bd)
        o_inter = _dot_nt(q_t, state_t.astype(BF16))
        oacc_ref[rows, :] = o_intra + o_inter
        d_state = jnp.where(mask_s, _dot(v_c.T.astype(BF16), k_d), 0.0)
        return state_t * dec_ref[n] + d_state

    lax.fori_loop(0, nc, chunk, jnp.zeros((GLA_OUT, GLA_QK), F32))

    o = oacc_ref[...]
    gr = lax.broadcasted_iota(jnp.int32, (GLA_OUT, GLA_OUT), 0) // DV
    gc = lax.broadcasted_iota(jnp.int32, (GLA_OUT, GLA_OUT), 1) // DV
    group_mean = jnp.where(gr == gc, 1.0 / DV, 0.0).astype(BF16)
    hi, mid, lo = _split3(o * o)
    ms = _dot(hi, group_mean) + _dot(mid, group_mean) + _dot(lo, group_mean)
    o = o * lax.rsqrt(ms + RMS_EPS) * ng_ref[...]
    r = r_ref[...]
    o_ref[...] = (o * (r * jax.nn.sigmoid(r))).astype(o_ref.dtype)


def _gla(h, wa2_pad, ba, norm_g, batch, seq):
    nc = seq // GLA_CHUNK
    return pl.pallas_call(
        functools.partial(_gla_kernel, seq=seq),
        grid=(batch,),
        in_specs=[
            pl.BlockSpec((seq, GLA_QK), lambda b: (b, 2048 // LANES)),
            pl.BlockSpec((seq, GLA_QK), lambda b: (b, 2176 // LANES)),
            pl.BlockSpec((seq, GLA_OUT), lambda b: (b, 1536 // 256)),
            pl.BlockSpec((seq, LANES), lambda b: (b, 2432 // LANES)),
            pl.BlockSpec((seq, GLA_OUT), lambda b: (b, 1792 // 256)),
            pl.BlockSpec((LANES, GLA_QK), lambda b: (0, 0)),
            pl.BlockSpec((1, GLA_QK), lambda b: (0, 0)),
            pl.BlockSpec((1, GLA_OUT), lambda b: (0, 0)),
        ],
        out_specs=pl.BlockSpec((seq, GLA_OUT), lambda b: (b, 0)),
        out_shape=jax.ShapeDtypeStruct((batch * seq, GLA_OUT), BF16),
        scratch_shapes=[
            pltpu.VMEM((seq, GLA_QK), F32),
            pltpu.VMEM((seq, GLA_QK), F32),
            pltpu.VMEM((seq, GLA_QK), F32),
            pltpu.VMEM((nc, 1, GLA_QK), F32),
            pltpu.VMEM((seq, GLA_OUT), F32),
        ],
        compiler_params=pltpu.CompilerParams(
            dimension_semantics=("parallel",), vmem_limit_bytes=VMEM_LIMIT),
        name="gla_mix",
    )(h, h, h, h, h, wa2_pad, ba, norm_g)


def _outproj_kernel(x_ref, oa_ref, ob_ref, oc_ref, wa_ref, wb_ref, wc_ref, g_ref, b_ref, o_ref):
    m = _dot(oa_ref[...], wa_ref[...]) + _dot(ob_ref[...], wb_ref[...]) + _dot(oc_ref[...], wc_ref[...])
    o_ref[...] = _layer_norm(ALPHA * x_ref[...] + m, g_ref[...], b_ref[...])


def _out_proj_ln(x, o_a, o_b, o_c, w_a, w_b, w_c, g, b, *, tm=512):
    n = x.shape[0]
    row = lambda i: (i, 0)
    const = lambda i: (0, 0)
    return pl.pallas_call(
        _outproj_kernel,
        grid=(n // tm,),
        in_specs=[
            pl.BlockSpec((tm, D_MODEL), row),
            pl.BlockSpec((tm, NSA_OUT), row),
            pl.BlockSpec((tm, POOL_DIM), row),
            pl.BlockSpec((tm, GLA_OUT), row),
            pl.BlockSpec((NSA_OUT, D_MODEL), const),
            pl.BlockSpec((POOL_DIM, D_MODEL), const),
            pl.BlockSpec((GLA_OUT, D_MODEL), const),
            pl.BlockSpec((1, D_MODEL), const),
            pl.BlockSpec((1, D_MODEL), const),
        ],
        out_specs=pl.BlockSpec((tm, D_MODEL), row),
        out_shape=jax.ShapeDtypeStruct((n, D_MODEL), F32),
        compiler_params=pltpu.CompilerParams(
            dimension_semantics=("parallel",), vmem_limit_bytes=VMEM_LIMIT),
        name="out_proj_ln",
    )(x, o_a, o_b, o_c, w_a, w_b, w_c, g, b)


def _pad_cols(w, width):
    return jnp.pad(w, ((0, 0), (0, width - w.shape[1])))


def _prep_w_in(w_in):
    parts = [w_in[:, 0:1280]]
    for name in ("u", "gv", "gr", "gq", "gk"):
        s, wd, _ = _IN_LAYOUT[name]
        parts.append(w_in[:, s:s + wd])
    for name in ("gl", "ga"):
        s, wd, _ = _IN_LAYOUT[name]
        parts.append(_pad_cols(w_in[:, s:s + wd], LANES))
    return jnp.concatenate(parts, axis=1).astype(BF16)


def _block_diag(blocks):
    n, r, c = blocks.shape
    eye = jnp.eye(n, dtype=blocks.dtype)
    return jnp.einsum("grc,gh->grhc", blocks, eye).reshape(n * r, n * c)


def _prep_compress(cmp_pe, cmp_w1, cmp_w2):
    G, HD = NSA_KV_GROUPS, NSA_HEAD_DIM
    pe = jnp.tile(cmp_pe[:, :, None, :], (1, 1, G, 1)).reshape(2, 1, CMP_BLOCK * G * HD)
    w1 = cmp_w1.reshape(2, CMP_BLOCK, HD, HD)
    eye = jnp.eye(G, dtype=cmp_w1.dtype)
    w1_bd = jnp.einsum("kpde,gh->kpgdhe", w1, eye).reshape(2, CMP_BLOCK * G * HD, G * HD)
    w2_bd = jnp.einsum("kde,gh->kgdhe", cmp_w2, eye).reshape(2, G * HD, G * HD)
    return pe, w1_bd.astype(BF16), w2_bd.astype(BF16)


def kernel(x, ln_g, ln_b, ffn_wg, ffn_wu, ffn_wd, w_in, w_out, cmp_pe, cmp_w1, cmp_w2,
           pool_w, pool_scale, gla_wa2, gla_ba, gla_norm_g):
    batch, seq, _ = x.shape
    xf = x.reshape(batch * seq, D_MODEL)
    for l in range(DEPTH):
        lg = lambda i: ln_g[l, i].reshape(1, D_MODEL)
        lb = lambda i: ln_b[l, i].reshape(1, D_MODEL)
        xf = _ffn_ln(xf, ffn_wg[l, 0].astype(BF16), ffn_wu[l, 0].astype(BF16), ffn_wd[l, 0].astype(BF16),
                     lg(0), lb(0))
        h = _in_proj(xf, _prep_w_in(w_in[l]))
        pe_rows, w1_bd, w2_bd = _prep_compress(cmp_pe[l], cmp_w1[l], cmp_w2[l])
        kc, vc = _compress(h, pe_rows, w1_bd, w2_bd, batch, seq)
        o_a = _nsa(h, kc, vc, batch, seq)
        o_b = _pool(h, _block_diag(pool_w[l]).astype(BF16), pool_scale[l].reshape(1, POOL_DIM), batch, seq)
        wa2_pad = jnp.pad(gla_wa2[l], ((0, LANES - GLA_GATE_RANK), (0, 0))).astype(BF16)
        o_c = _gla(h, wa2_pad, gla_ba[l].reshape(1, GLA_QK), gla_norm_g[l].reshape(1, GLA_OUT), batch, seq)
        wo = w_out[l].astype(BF16)
        xf = _out_proj_ln(xf, o_a, o_b, o_c, wo[:NSA_OUT], wo[NSA_OUT:NSA_OUT + POOL_DIM],
                          wo[NSA_OUT + POOL_DIM:], lg(1), lb(1))
        xf = _ffn_ln(xf, ffn_wg[l, 1].astype(BF16), ffn_wu[l, 1].astype(BF16), ffn_wd[l, 1].astype(BF16),
                     lg(2), lb(2))
    return xf.reshape(batch, seq, D_MODEL)
```

```python
import functools

import numpy as np
import jax
import jax.numpy as jnp
from jax import lax
from jax.experimental import pallas as pl
from jax.experimental.pallas import tpu as pltpu

F32 = jnp.float32
BF16 = jnp.bfloat16

D_MODEL = 1024
DEPTH = 4
D_FF = 2816
NSA_HEADS = 8
NSA_KV_GROUPS = 2
NSA_HEAD_DIM = 64
NSA_GROUP_SIZE = NSA_HEADS // NSA_KV_GROUPS
NSA_OUT = NSA_HEADS * NSA_HEAD_DIM
NSA_KV_DIM = NSA_KV_GROUPS * NSA_HEAD_DIM
CMP_BLOCK = 32
CMP_STRIDE = 16
SLC_BLOCK = 64
SLC_TOPK = 8
WINDOW = 512
Q_BLOCK = 128
POOL_WINDOWS = (2, 4, 8, 16)
POOL_GROUPS = 4
POOL_GROUP_DIM = 64
POOL_DIM = POOL_GROUPS * POOL_GROUP_DIM
GLA_HEADS = 4
GLA_KEY_DIM = 32
GLA_VAL_DIM = 64
GLA_GATE_RANK = 16
GLA_TAU = 16.0
GLA_CHUNK = 64
GLA_QK = GLA_HEADS * GLA_KEY_DIM
GLA_OUT = GLA_HEADS * GLA_VAL_DIM
ALPHA = (2.0 * DEPTH) ** 0.25
LN_EPS = 1e-5
RMS_EPS = 1e-6

LANES = 128
NEG = -1e30
VMEM_LIMIT = 56 * 1024 * 1024

_IN_LAYOUT = {
    "qkv": (0, 1280, 0),
    "u": (1304, 256, 1280),
    "gv": (1816, 256, 1536),
    "gr": (2088, 256, 1792),
    "gq": (1560, 128, 2048),
    "gk": (1688, 128, 2176),
    "gl": (1280, 24, 2304),
    "ga": (2072, 16, 2432),
}
D_IN_PAD = 2560


def _dot(a, b):
    return jnp.dot(a, b, preferred_element_type=F32)


def _dot_nt(a, b):
    return lax.dot_general(a, b, (((1,), (1,)), ((), ())), preferred_element_type=F32)


def _layer_norm(y, g, b):
    mu = jnp.mean(y, axis=-1, keepdims=True)
    yc = y - mu
    var = jnp.mean(yc * yc, axis=-1, keepdims=True)
    return yc * lax.rsqrt(var + LN_EPS) * g + b


def _split3(x):
    hi = x.astype(BF16)
    r1 = x - hi.astype(F32)
    mid = r1.astype(BF16)
    lo = (r1 - mid.astype(F32)).astype(BF16)
    return hi, mid, lo


def _ffn_kernel(x_ref, wg_ref, wu_ref, wd_ref, g_ref, b_ref, o_ref, xb_ref, acc_ref):
    j = pl.program_id(1)

    @pl.when(j == 0)
    def _():
        xb_ref[...] = x_ref[...].astype(BF16)
        acc_ref[...] = jnp.zeros_like(acc_ref)

    xb = xb_ref[...]
    gate = _dot(xb, wg_ref[...])
    up = _dot(xb, wu_ref[...])
    act = (gate * jax.nn.sigmoid(gate)) * up
    acc_ref[...] += _dot(act.astype(BF16), wd_ref[...])

    @pl.when(j == pl.num_programs(1) - 1)
    def _():
        y = ALPHA * x_ref[...] + 0.5 * acc_ref[...]
        o_ref[...] = _layer_norm(y, g_ref[...], b_ref[...])


def _ffn_ln(x, wg, wu, wd, g, b, *, tm=1024, tf=256):
    n = x.shape[0]
    grid = (n // tm, D_FF // tf)
    return pl.pallas_call(
        _ffn_kernel,
        grid=grid,
        in_specs=[
            pl.BlockSpec((tm, D_MODEL), lambda i, j: (i, 0)),
            pl.BlockSpec((D_MODEL, tf), lambda i, j: (0, j)),
            pl.BlockSpec((D_MODEL, tf), lambda i, j: (0, j)),
            pl.BlockSpec((tf, D_MODEL), lambda i, j: (j, 0)),
            pl.BlockSpec((1, D_MODEL), lambda i, j: (0, 0)),
            pl.BlockSpec((1, D_MODEL), lambda i, j: (0, 0)),
        ],
        out_specs=pl.BlockSpec((tm, D_MODEL), lambda i, j: (i, 0)),
        out_shape=jax.ShapeDtypeStruct((n, D_MODEL), F32),
        scratch_shapes=[pltpu.VMEM((tm, D_MODEL), BF16), pltpu.VMEM((tm, D_MODEL), F32)],
        compiler_params=pltpu.CompilerParams(
            dimension_semantics=("parallel", "arbitrary"), vmem_limit_bytes=VMEM_LIMIT),
        name="ffn_ln",
    )(x, wg, wu, wd, g, b)


def _inproj_kernel(x_ref, w_ref, o_ref):
    o_ref[...] = _dot(x_ref[...].astype(BF16), w_ref[...])


def _in_proj(x, w_pad, *, tm=512):
    n = x.shape[0]
    return pl.pallas_call(
        _inproj_kernel,
        grid=(n // tm,),
        in_specs=[
            pl.BlockSpec((tm, D_MODEL), lambda i: (i, 0)),
            pl.BlockSpec((D_MODEL, D_IN_PAD), lambda i: (0, 0)),
        ],
        out_specs=pl.BlockSpec((tm, D_IN_PAD), lambda i: (i, 0)),
        out_shape=jax.ShapeDtypeStruct((n, D_IN_PAD), F32),
        compiler_params=pltpu.CompilerParams(
            dimension_semantics=("parallel",), vmem_limit_bytes=VMEM_LIMIT),
        name="in_proj",
    )(x, w_pad)


N_CMP_PAD = 128


def _gelu_tanh(x):
    return 0.5 * x * (1.0 + jnp.tanh(np.sqrt(2.0 / np.pi) * (x + 0.044715 * (x * x * x))))


def _compress_kernel(zk_ref, zv_ref, pe_ref, w1_ref, w2_ref, ok_ref, ov_ref):
    def one(z_ref, which, o_ref):
        slabs = [z_ref[pl.ds(q, N_CMP_PAD, stride=CMP_STRIDE), :] for q in range(CMP_STRIDE)]
        cat = jnp.concatenate(slabs, axis=1)
        half = CMP_STRIDE * LANES
        top = _dot((cat + pe_ref[which, :, :half]).astype(BF16), w1_ref[which, :half, :])
        bot = _dot((cat + pe_ref[which, :, half:]).astype(BF16), w1_ref[which, half:, :])
        pre = top + pltpu.roll(bot, N_CMP_PAD - 1, 0)
        o_ref[...] = _dot(_gelu_tanh(pre).astype(BF16), w2_ref[which])

    one(zk_ref, 0, ok_ref)
    one(zv_ref, 1, ov_ref)


def _compress(h, pe_rows, w1_bd, w2_bd, batch, seq):
    kc_blk = 512 // LANES
    return pl.pallas_call(
        _compress_kernel,
        grid=(batch,),
        in_specs=[
            pl.BlockSpec((seq, LANES), lambda b: (b, kc_blk)),
            pl.BlockSpec((seq, LANES), lambda b: (b, kc_blk + 1)),
            pl.BlockSpec((2, 1, CMP_BLOCK * LANES), lambda b: (0, 0, 0)),
            pl.BlockSpec((2, CMP_BLOCK * LANES, LANES), lambda b: (0, 0, 0)),
            pl.BlockSpec((2, LANES, LANES), lambda b: (0, 0, 0)),
        ],
        out_specs=[
            pl.BlockSpec((N_CMP_PAD, LANES), lambda b: (b, 0)),
            pl.BlockSpec((N_CMP_PAD, LANES), lambda b: (b, 0)),
        ],
        out_shape=[jax.ShapeDtypeStruct((batch * N_CMP_PAD, LANES), F32)] * 2,
        compiler_params=pltpu.CompilerParams(
            dimension_semantics=("parallel",), vmem_limit_bytes=VMEM_LIMIT),
        name="nsa_compress",
    )(h, h, pe_rows, w1_bd, w2_bd)


N_SLC = 32
WIN_KEYS = WINDOW + Q_BLOCK
KEY_CHUNK = 512
GQ = NSA_GROUP_SIZE * Q_BLOCK


def _softmax_cols(s):
    m = jnp.max(s, axis=0, keepdims=True)
    m = jnp.where(m > 0.5 * NEG, m, 0.0)
    e = jnp.exp(s - m)
    d = jnp.sum(e, axis=0, keepdims=True)
    inv = 1.0 / jnp.where(d > 0.0, d, 1.0)
    return e, inv


def _tile4(x):
    return jnp.concatenate([x] * NSA_GROUP_SIZE, axis=1)


def _nsa_kernel(q_ref, kc_ref, vc_ref, ksvs_ref, kwvw_ref, gl_ref, o_ref,
                ks_s, vst_s, kw_s, vwt_s, kc_s, vct_s, exp_s, *, seq):
    qi = pl.program_id(1)
    start = qi * Q_BLOCK
    hd = NSA_HEAD_DIM
    n_chunks = seq // KEY_CHUNK
    n_qb = seq // Q_BLOCK

    @pl.when(qi == 0)
    def _():
        ks_s[...] = ksvs_ref[:, :NSA_KV_DIM].astype(BF16)
        kw_s[...] = kwvw_ref[:, :NSA_KV_DIM].astype(BF16)
        for c in range(n_chunks):
            vst_s[c] = ksvs_ref[c * KEY_CHUNK:(c + 1) * KEY_CHUNK, NSA_KV_DIM:].T.astype(BF16)
        for j in range(n_qb):
            vwt_s[j] = kwvw_ref[j * Q_BLOCK:(j + 1) * Q_BLOCK, NSA_KV_DIM:].T.astype(BF16)
        kc_s[...] = kc_ref[...].astype(BF16)
        vct_s[...] = vc_ref[...].T.astype(BF16)
        key_blk = lax.broadcasted_iota(jnp.int32, (seq, LANES), 0) // SLC_BLOCK
        blk = lax.broadcasted_iota(jnp.int32, (seq, LANES), 1)
        exp_s[...] = (key_blk == blk).astype(BF16)

    t_row = start + lax.broadcasted_iota(jnp.int32, (1, Q_BLOCK), 1)

    q_t = (q_ref[...] * (hd ** -0.5)).T.astype(BF16)
    zeros_half = jnp.zeros((hd, GQ), BF16)
    q_ops = []
    for g in range(NSA_KV_GROUPS):
        top = jnp.concatenate([q_t[(g * NSA_GROUP_SIZE + r) * hd:(g * NSA_GROUP_SIZE + r + 1) * hd, :]
                               for r in range(NSA_GROUP_SIZE)], axis=1)
        q_ops.append(jnp.concatenate([top, zeros_half] if g == 0 else [zeros_half, top], axis=0))

    n_sub = lax.broadcasted_iota(jnp.int32, (N_CMP_PAD, 1), 0)
    bias_c = jnp.where(n_sub * CMP_STRIDE + (CMP_BLOCK - 1) <= t_row, 0.0, NEG)
    m_sub = lax.broadcasted_iota(jnp.int32, (N_SLC, N_CMP_PAD), 0)
    n_lane = lax.broadcasted_iota(jnp.int32, (N_SLC, N_CMP_PAD), 1)
    c0 = n_lane * CMP_STRIDE
    s0 = m_sub * SLC_BLOCK
    ov_t = ((c0 <= s0 + SLC_BLOCK - 1) & (c0 + CMP_BLOCK - 1 >= s0)
            & (n_lane < seq // CMP_STRIDE - 1)).astype(BF16)
    m_idx = lax.broadcasted_iota(jnp.int32, (N_SLC, Q_BLOCK), 0)
    cur = t_row // SLC_BLOCK
    forced = (m_idx == 0) | (m_idx == cur) | (m_idx == cur - 1)
    future = m_idx * SLC_BLOCK > t_row

    o_c = []
    sels = []
    for g in range(NSA_KV_GROUPS):
        e, inv = _softmax_cols(_dot(kc_s[...], q_ops[g]) + _tile4(bias_c))
        p_c = e * inv
        o_c.append(_dot(vct_s[g * hd:(g + 1) * hd, :], p_c.astype(BF16)))
        p_sum = (p_c[:, 0:Q_BLOCK] + p_c[:, Q_BLOCK:2 * Q_BLOCK]
                 + p_c[:, 2 * Q_BLOCK:3 * Q_BLOCK] + p_c[:, 3 * Q_BLOCK:])
        hi, mid, lo = _split3(p_sum)
        imp = _dot(ov_t, hi) + _dot(ov_t, mid) + _dot(ov_t, lo)
        imp = jnp.where(forced, -NEG, jnp.where(future, NEG, imp))
        rank = jnp.zeros((N_SLC, Q_BLOCK), jnp.int32)
        for m2 in range(N_SLC):
            row = imp[m2:m2 + 1, :]
            ahead = (row > imp) | ((row == imp) & (m2 < m_idx))
            rank = rank + ahead.astype(jnp.int32)
        sel = (rank < SLC_TOPK).astype(BF16)
        sels.append(jnp.concatenate([sel, jnp.zeros((LANES - N_SLC, Q_BLOCK), BF16)], axis=0))

    def chunk_step(c, carry):
        off = pl.multiple_of(c * KEY_CHUNK, KEY_CHUNK)
        k_c = ks_s[pl.ds(off, KEY_CHUNK), :]
        e_c = exp_s[pl.ds(off, KEY_CHUNK), :]
        kpos = off + lax.broadcasted_iota(jnp.int32, (KEY_CHUNK, 1), 0)
        causal = kpos <= t_row
        out = []
        for g in range(NSA_KV_GROUPS):
            m_p, l_p, acc = carry[3 * g:3 * g + 3]
            picked = _dot(e_c, sels[g])
            bias = jnp.where(causal & (picked > 0.5), 0.0, NEG)
            s = _dot(k_c, q_ops[g]) + _tile4(bias)
            m_n = jnp.maximum(m_p, jnp.max(s, axis=0, keepdims=True))
            alpha = jnp.exp(m_p - m_n)
            p = jnp.exp(s - m_n)
            l_n = alpha * l_p + jnp.sum(p, axis=0, keepdims=True)
            acc_n = alpha * acc + _dot(vst_s[c, g * hd:(g + 1) * hd, :], p.astype(BF16))
            out += [m_n, l_n, acc_n]
        return tuple(out)

    init = (jnp.full((1, GQ), NEG, F32), jnp.zeros((1, GQ), F32), jnp.zeros((hd, GQ), F32)) * NSA_KV_GROUPS
    n_live = qi // (KEY_CHUNK // Q_BLOCK) + 1
    fin = lax.fori_loop(0, n_live, chunk_step, init)
    o_s = [fin[3 * g + 2] * (1.0 / fin[3 * g + 1]) for g in range(NSA_KV_GROUPS)]

    j0 = jnp.maximum(qi - WINDOW // Q_BLOCK, 0)
    w0 = pl.multiple_of(j0 * Q_BLOCK, Q_BLOCK)
    kp_w = w0 + lax.broadcasted_iota(jnp.int32, (WIN_KEYS, 1), 0)
    bias_w = jnp.where((kp_w <= t_row) & (kp_w > t_row - WINDOW), 0.0, NEG)
    k_w = kw_s[pl.ds(w0, WIN_KEYS), :]
    o_w = []
    for g in range(NSA_KV_GROUPS):
        e, inv = _softmax_cols(_dot(k_w, q_ops[g]) + _tile4(bias_w))
        v_t = jnp.concatenate([vwt_s[j0 + j, g * hd:(g + 1) * hd, :] for j in range(WIN_KEYS // Q_BLOCK)],
                              axis=1)
        o_w.append(_dot(v_t, e.astype(BF16)) * inv)

    gates_t = jax.nn.sigmoid(gl_ref[...]).T
    outs = []
    for h in range(NSA_HEADS):
        g, r = divmod(h, NSA_GROUP_SIZE)
        lanes = slice(r * Q_BLOCK, (r + 1) * Q_BLOCK)
        outs.append(gates_t[3 * h:3 * h + 1, :] * o_c[g][:, lanes]
                    + gates_t[3 * h + 1:3 * h + 2, :] * o_s[g][:, lanes]
                    + gates_t[3 * h + 2:3 * h + 3, :] * o_w[g][:, lanes])
    o_ref[...] = jnp.concatenate(outs, axis=0).T.astype(o_ref.dtype)


def _nsa(h, kc, vc, batch, seq):
    nqb = seq // Q_BLOCK
    return pl.pallas_call(
        functools.partial(_nsa_kernel, seq=seq),
        grid=(batch, nqb),
        in_specs=[
            pl.BlockSpec((Q_BLOCK, NSA_OUT), lambda b, i: (b * nqb + i, 0)),
            pl.BlockSpec((N_CMP_PAD, LANES), lambda b, i: (b, 0)),
            pl.BlockSpec((N_CMP_PAD, LANES), lambda b, i: (b, 0)),
            pl.BlockSpec((seq, 2 * NSA_KV_DIM), lambda b, i: (b, 768 // 256)),
            pl.BlockSpec((seq, 2 * NSA_KV_DIM), lambda b, i: (b, 1024 // 256)),
            pl.BlockSpec((Q_BLOCK, LANES), lambda b, i: (b * nqb + i, 2304 // LANES)),
        ],
        out_specs=pl.BlockSpec((Q_BLOCK, NSA_OUT), lambda b, i: (b * nqb + i, 0)),
        out_shape=jax.ShapeDtypeStruct((batch * seq, NSA_OUT), BF16),
        scratch_shapes=[
            pltpu.VMEM((seq, NSA_KV_DIM), BF16),
            pltpu.VMEM((seq // KEY_CHUNK, NSA_KV_DIM, KEY_CHUNK), BF16),
            pltpu.VMEM((seq, NSA_KV_DIM), BF16),
            pltpu.VMEM((nqb, NSA_KV_DIM, Q_BLOCK), BF16),
            pltpu.VMEM((N_CMP_PAD, NSA_KV_DIM), BF16),
            pltpu.VMEM((NSA_KV_DIM, N_CMP_PAD), BF16),
            pltpu.VMEM((seq, LANES), BF16),
        ],
        compiler_params=pltpu.CompilerParams(
            dimension_semantics=("parallel", "arbitrary"), vmem_limit_bytes=VMEM_LIMIT),
        name="nsa_attn",
    )(h, kc, vc, h, h, h)


def _shift_rows(x, k, row):
    return jnp.where(row >= k, pltpu.roll(x, k, 0), 0.0)


def _pool_kernel(u_ref, w_ref, sc_ref, o_ref, *, seq):
    u = u_ref[...]
    row = lax.broadcasted_iota(jnp.int32, (seq, 1), 0)
    lane = lax.broadcasted_iota(jnp.int32, (1, POOL_DIM), 1)
    tp1 = (row + 1).astype(F32)
    acc = u
    mean = jnp.zeros_like(u)
    span = 1
    for gi, w in enumerate(POOL_WINDOWS):
        while span < w:
            acc = acc + _shift_rows(acc, span, row)
            span *= 2
        cnt = jnp.minimum(float(w), tp1)
        in_group = (lane >= gi * POOL_GROUP_DIM) & (lane < (gi + 1) * POOL_GROUP_DIM)
        mean = jnp.where(in_group, acc / cnt, mean)
    pooled = mean - u
    o_ref[...] = (_dot(pooled.astype(BF16), w_ref[...]) * sc_ref[...]).astype(o_ref.dtype)


def _pool(h, w_bd, scale, batch, seq):
    return pl.pallas_call(
        functools.partial(_pool_kernel, seq=seq),
        grid=(batch,),
        in_specs=[
            pl.BlockSpec((seq, POOL_DIM), lambda b: (b, 1280 // 256)),
            pl.BlockSpec((POOL_DIM, POOL_DIM), lambda b: (0, 0)),
            pl.BlockSpec((1, POOL_DIM), lambda b: (0, 0)),
        ],
        out_specs=pl.BlockSpec((seq, POOL_DIM), lambda b: (b, 0)),
        out_shape=jax.ShapeDtypeStruct((batch * seq, POOL_DIM), BF16),
        compiler_params=pltpu.CompilerParams(
            dimension_semantics=("parallel",), vmem_limit_bytes=VMEM_LIMIT),
        name="pool_mix",
    )(h, w_bd, scale)


def _gla_kernel(q_ref, k_ref, v_ref, a_ref, r_ref, wa2_ref, ba_ref, ng_ref, o_ref,
                qt_ref, kt_ref, kd_ref, dec_ref, oacc_ref, *, seq):
    C, H, DK, DV = GLA_CHUNK, GLA_HEADS, GLA_KEY_DIM, GLA_VAL_DIM
    nc = seq // C
    z = _dot(a_ref[...].astype(BF16), wa2_ref[...]) + ba_ref[...]
    log_a = (jnp.minimum(z, 0.0) - jnp.log(1.0 + jnp.exp(-jnp.abs(z)))) / GLA_TAU
    pos = lax.broadcasted_iota(jnp.int32, (seq, 1), 0) % C
    b = log_a
    step = 1
    while step < C:
        b = b + jnp.where(pos >= step, pltpu.roll(b, step, 0), 0.0)
        step *= 2
    b3 = b.reshape(nc, C, GLA_QK)
    b_last = b3[:, C - 1:C, :]
    qt_ref[...] = q_ref[...] * (DK ** -0.5) * jnp.exp(b)
    kt_ref[...] = k_ref[...] * jnp.exp(-b)
    kd_ref[...] = (k_ref[...].reshape(nc, C, GLA_QK) * jnp.exp(b_last - b3)).reshape(seq, GLA_QK)
    dec_ref[...] = jnp.exp(b_last)

    r_k = lax.broadcasted_iota(jnp.int32, (H * C, GLA_QK), 0) // C
    c_k = lax.broadcasted_iota(jnp.int32, (H * C, GLA_QK), 1) // DK
    mask_k = r_k == c_k
    r_v = lax.broadcasted_iota(jnp.int32, (H * C, GLA_OUT), 0) // C
    c_v = lax.broadcasted_iota(jnp.int32, (H * C, GLA_OUT), 1) // DV
    mask_v = r_v == c_v
    r_s = lax.broadcasted_iota(jnp.int32, (GLA_OUT, GLA_QK), 0) // DV
    c_s = lax.broadcasted_iota(jnp.int32, (GLA_OUT, GLA_QK), 1) // DK
    mask_s = r_s == c_s
    i_a = lax.broadcasted_iota(jnp.int32, (C, H * C), 0)
    j_a = lax.broadcasted_iota(jnp.int32, (C, H * C), 1) % C
    tril = j_a <= i_a

    def chunk(n, state_t):
        rows = pl.ds(pl.multiple_of(n * C, C), C)
        q_t = qt_ref[rows, :].astype(BF16)
        k_t = kt_ref[rows, :]
        k_d = kd_ref[rows, :].astype(BF16)
        v_c = v_ref[rows, :]
        k_bd = jnp.where(mask_k, jnp.concatenate([k_t] * H, axis=0), 0.0).astype(BF16)
        a_cat = jnp.where(tril, _dot_nt(q_t, k_bd), 0.0)
        v_bd = jnp.where(mask_v, jnp.concatenate([v_c] * H, axis=0), 0.0).astype(BF16)
        o_intra = _dot(a_cat.astype(BF16), v_bd)
        o_inter = _dot_nt(q_t, state_t.astype(BF16))
        oacc_ref[rows, :] = o_intra + o_inter
        d_state = jnp.where(mask_s, _dot(v_c.T.astype(BF16), k_d), 0.0)
        return state_t * dec_ref[n] + d_state

    lax.fori_loop(0, nc, chunk, jnp.zeros((GLA_OUT, GLA_QK), F32))

    o = oacc_ref[...]
    gr = lax.broadcasted_iota(jnp.int32, (GLA_OUT, GLA_OUT), 0) // DV
    gc = lax.broadcasted_iota(jnp.int32, (GLA_OUT, GLA_OUT), 1) // DV
    group_mean = jnp.where(gr == gc, 1.0 / DV, 0.0).astype(BF16)
    hi, mid, lo = _split3(o * o)
    ms = _dot(hi, group_mean) + _dot(mid, group_mean) + _dot(lo, group_mean)
    o = o * lax.rsqrt(ms + RMS_EPS) * ng_ref[...]
    r = r_ref[...]
    o_ref[...] = (o * (r * jax.nn.sigmoid(r))).astype(o_ref.dtype)


def _gla(h, wa2_pad, ba, norm_g, batch, seq):
    nc = seq // GLA_CHUNK
    return pl.pallas_call(
        functools.partial(_gla_kernel, seq=seq),
        grid=(batch,),
        in_specs=[
            pl.BlockSpec((seq, GLA_QK), lambda b: (b, 2048 // LANES)),
            pl.BlockSpec((seq, GLA_QK), lambda b: (b, 2176 // LANES)),
            pl.BlockSpec((seq, GLA_OUT), lambda b: (b, 1536 // 256)),
            pl.BlockSpec((seq, LANES), lambda b: (b, 2432 // LANES)),
            pl.BlockSpec((seq, GLA_OUT), lambda b: (b, 1792 // 256)),
            pl.BlockSpec((LANES, GLA_QK), lambda b: (0, 0)),
            pl.BlockSpec((1, GLA_QK), lambda b: (0, 0)),
            pl.BlockSpec((1, GLA_OUT), lambda b: (0, 0)),
        ],
        out_specs=pl.BlockSpec((seq, GLA_OUT), lambda b: (b, 0)),
        out_shape=jax.ShapeDtypeStruct((batch * seq, GLA_OUT), BF16),
        scratch_shapes=[
            pltpu.VMEM((seq, GLA_QK), F32),
            pltpu.VMEM((seq, GLA_QK), F32),
            pltpu.VMEM((seq, GLA_QK), F32),
            pltpu.VMEM((nc, 1, GLA_QK), F32),
            pltpu.VMEM((seq, GLA_OUT), F32),
        ],
        compiler_params=pltpu.CompilerParams(
            dimension_semantics=("parallel",), vmem_limit_bytes=VMEM_LIMIT),
        name="gla_mix",
    )(h, h, h, h, h, wa2_pad, ba, norm_g)


def _outproj_kernel(x_ref, oa_ref, ob_ref, oc_ref, wa_ref, wb_ref, wc_ref, g_ref, b_ref, o_ref):
    m = _dot(oa_ref[...], wa_ref[...]) + _dot(ob_ref[...], wb_ref[...]) + _dot(oc_ref[...], wc_ref[...])
    o_ref[...] = _layer_norm(ALPHA * x_ref[...] + m, g_ref[...], b_ref[...])


def _out_proj_ln(x, o_a, o_b, o_c, w_a, w_b, w_c, g, b, *, tm=512):
    n = x.shape[0]
    row = lambda i: (i, 0)
    const = lambda i: (0, 0)
    return pl.pallas_call(
        _outproj_kernel,
        grid=(n // tm,),
        in_specs=[
            pl.BlockSpec((tm, D_MODEL), row),
            pl.BlockSpec((tm, NSA_OUT), row),
            pl.BlockSpec((tm, POOL_DIM), row),
            pl.BlockSpec((tm, GLA_OUT), row),
            pl.BlockSpec((NSA_OUT, D_MODEL), const),
            pl.BlockSpec((POOL_DIM, D_MODEL), const),
            pl.BlockSpec((GLA_OUT, D_MODEL), const),
            pl.BlockSpec((1, D_MODEL), const),
            pl.BlockSpec((1, D_MODEL), const),
        ],
        out_specs=pl.BlockSpec((tm, D_MODEL), row),
        out_shape=jax.ShapeDtypeStruct((n, D_MODEL), F32),
        compiler_params=pltpu.CompilerParams(
            dimension_semantics=("parallel",), vmem_limit_bytes=VMEM_LIMIT),
        name="out_proj_ln",
    )(x, o_a, o_b, o_c, w_a, w_b, w_c, g, b)


def _pad_cols(w, width):
    return jnp.pad(w, ((0, 0), (0, width - w.shape[1])))


def _prep_w_in(w_in):
    parts = [w_in[:, 0:1280]]
    for name in ("u", "gv", "gr", "gq", "gk"):
        s, wd, _ = _IN_LAYOUT[name]
        parts.append(w_in[:, s:s + wd])
    for name in ("gl", "ga"):
        s, wd, _ = _IN_LAYOUT[name]
        parts.append(_pad_cols(w_in[:, s:s + wd], LANES))
    return jnp.concatenate(parts, axis=1).astype(BF16)


def _block_diag(blocks):
    n, r, c = blocks.shape
    eye = jnp.eye(n, dtype=blocks.dtype)
    return jnp.einsum("grc,gh->grhc", blocks, eye).reshape(n * r, n * c)


def _prep_compress(cmp_pe, cmp_w1, cmp_w2):
    G, HD = NSA_KV_GROUPS, NSA_HEAD_DIM
    pe = jnp.tile(cmp_pe[:, :, None, :], (1, 1, G, 1)).reshape(2, 1, CMP_BLOCK * G * HD)
    w1 = cmp_w1.reshape(2, CMP_BLOCK, HD, HD)
    eye = jnp.eye(G, dtype=cmp_w1.dtype)
    w1_bd = jnp.einsum("kpde,gh->kpgdhe", w1, eye).reshape(2, CMP_BLOCK * G * HD, G * HD)
    w2_bd = jnp.einsum("kde,gh->kgdhe", cmp_w2, eye).reshape(2, G * HD, G * HD)
    return pe, w1_bd.astype(BF16), w2_bd.astype(BF16)


def kernel(x, ln_g, ln_b, ffn_wg, ffn_wu, ffn_wd, w_in, w_out, cmp_pe, cmp_w1, cmp_w2,
           pool_w, pool_scale, gla_wa2, gla_ba, gla_norm_g):
    batch, seq, _ = x.shape
    xf = x.reshape(batch * seq, D_MODEL)
    for l in range(DEPTH):
        lg = lambda i: ln_g[l, i].reshape(1, D_MODEL)
        lb = lambda i: ln_b[l, i].reshape(1, D_MODEL)
        xf = _ffn_ln(xf, ffn_wg[l, 0].astype(BF16), ffn_wu[l, 0].astype(BF16), ffn_wd[l, 0].astype(BF16),
                     lg(0), lb(0))
        h = _in_proj(xf, _prep_w_in(w_in[l]))
        pe_rows, w1_bd, w2_bd = _prep_compress(cmp_pe[l], cmp_w1[l], cmp_w2[l])
        kc, vc = _compress(h, pe_rows, w1_bd, w2_bd, batch, seq)
        o_a = _nsa(h, kc, vc, batch, seq)
        o_b = _pool(h, _block_diag(pool_w[l]).astype(BF16), pool_scale[l].reshape(1, POOL_DIM), batch, seq)
        wa2_pad = jnp.pad(gla_wa2[l], ((0, LANES - GLA_GATE_RANK), (0, 0))).astype(BF16)
        o_c = _gla(h, wa2_pad, gla_ba[l].reshape(1, GLA_QK), gla_norm_g[l].reshape(1, GLA_OUT), batch, seq)
        wo = w_out[l].astype(BF16)
        xf = _out_proj_ln(xf, o_a, o_b, o_c, wo[:NSA_OUT], wo[NSA_OUT:NSA_OUT + POOL_DIM],
                          wo[NSA_OUT + POOL_DIM:], lg(1), lb(1))
        xf = _ffn_ln(xf, ffn_wg[l, 1].astype(BF16), ffn_wu[l, 1].astype(BF16), ffn_wd[l, 1].astype(BF16),
                     lg(2), lb(2))
    return xf.reshape(batch, seq, D_MODEL)
```

```python
import functools

import numpy as np
import jax
import jax.numpy as jnp
from jax import lax
from jax.experimental import pallas as pl
from jax.experimental.pallas import tpu as pltpu

F32 = jnp.float32
BF16 = jnp.bfloat16

D_MODEL = 1024
DEPTH = 4
D_FF = 2816
NSA_HEADS = 8
NSA_KV_GROUPS = 2
NSA_HEAD_DIM = 64
NSA_GROUP_SIZE = NSA_HEADS // NSA_KV_GROUPS
NSA_OUT = NSA_HEADS * NSA_HEAD_DIM
NSA_KV_DIM = NSA_KV_GROUPS * NSA_HEAD_DIM
CMP_BLOCK = 32
CMP_STRIDE = 16
SLC_BLOCK = 64
SLC_TOPK = 8
WINDOW = 512
Q_BLOCK = 128
POOL_WINDOWS = (2, 4, 8, 16)
POOL_GROUPS = 4
POOL_GROUP_DIM = 64
POOL_DIM = POOL_GROUPS * POOL_GROUP_DIM
GLA_HEADS = 4
GLA_KEY_DIM = 32
GLA_VAL_DIM = 64
GLA_GATE_RANK = 16
GLA_TAU = 16.0
GLA_CHUNK = 64
GLA_QK = GLA_HEADS * GLA_KEY_DIM
GLA_OUT = GLA_HEADS * GLA_VAL_DIM
ALPHA = (2.0 * DEPTH) ** 0.25
LN_EPS = 1e-5
RMS_EPS = 1e-6

LANES = 128
NEG = -1e30
VMEM_LIMIT = 56 * 1024 * 1024

_IN_LAYOUT = {
    "qkv": (0, 1280, 0),
    "u": (1304, 256, 1280),
    "gv": (1816, 256, 1536),
    "gr": (2088, 256, 1792),
    "gq": (1560, 128, 2048),
    "gk": (1688, 128, 2176),
    "gl": (1280, 24, 2304),
    "ga": (2072, 16, 2432),
}
D_IN_PAD = 2560


def _dot(a, b):
    return jnp.dot(a, b, preferred_element_type=F32)


def _dot_nt(a, b):
    return lax.dot_general(a, b, (((1,), (1,)), ((), ())), preferred_element_type=F32)


def _layer_norm(y, g, b):
    mu = jnp.mean(y, axis=-1, keepdims=True)
    yc = y - mu
    var = jnp.mean(yc * yc, axis=-1, keepdims=True)
    return yc * lax.rsqrt(var + LN_EPS) * g + b


def _split3(x):
    hi = x.astype(BF16)
    r1 = x - hi.astype(F32)
    mid = r1.astype(BF16)
    lo = (r1 - mid.astype(F32)).astype(BF16)
    return hi, mid, lo


FF_CHUNK = 256


def _ffn_kernel(x_ref, wg_ref, wu_ref, wd_ref, g_ref, b_ref, o_ref, xb_ref, acc_ref):
    xb_ref[...] = x_ref[...].astype(BF16)

    def contribution(j):
        xb = xb_ref[...]
        gate = _dot(xb, wg_ref[j])
        up = _dot(xb, wu_ref[j])
        act = (gate * jax.nn.sigmoid(gate)) * up
        return _dot(act.astype(BF16), wd_ref[j])

    acc_ref[...] = contribution(0)

    def step(j, carry):
        acc_ref[...] += contribution(j)
        return carry

    lax.fori_loop(1, D_FF // FF_CHUNK, step, 0)
    y = ALPHA * x_ref[...] + 0.5 * acc_ref[...]
    o_ref[...] = _layer_norm(y, g_ref[...], b_ref[...])


def _ffn_weights(wg, wu, wd):
    nch = D_FF // FF_CHUNK
    cols = lambda w: jnp.transpose(w.reshape(D_MODEL, nch, FF_CHUNK), (1, 0, 2)).astype(BF16)
    return cols(wg), cols(wu), wd.reshape(nch, FF_CHUNK, D_MODEL).astype(BF16)


def _ffn_ln(x, wg, wu, wd, g, b, *, tm=1024):
    n = x.shape[0]
    nch = D_FF // FF_CHUNK
    resident = dict(pipeline_mode=pl.Buffered(1))
    return pl.pallas_call(
        _ffn_kernel,
        grid=(n // tm,),
        in_specs=[
            pl.BlockSpec((tm, D_MODEL), lambda i: (i, 0)),
            pl.BlockSpec((nch, D_MODEL, FF_CHUNK), lambda i: (0, 0, 0), **resident),
            pl.BlockSpec((nch, D_MODEL, FF_CHUNK), lambda i: (0, 0, 0), **resident),
            pl.BlockSpec((nch, FF_CHUNK, D_MODEL), lambda i: (0, 0, 0), **resident),
            pl.BlockSpec((1, D_MODEL), lambda i: (0, 0)),
            pl.BlockSpec((1, D_MODEL), lambda i: (0, 0)),
        ],
        out_specs=pl.BlockSpec((tm, D_MODEL), lambda i: (i, 0)),
        out_shape=jax.ShapeDtypeStruct((n, D_MODEL), F32),
        scratch_shapes=[pltpu.VMEM((tm, D_MODEL), BF16), pltpu.VMEM((tm, D_MODEL), F32)],
        compiler_params=pltpu.CompilerParams(
            dimension_semantics=("parallel",), vmem_limit_bytes=VMEM_LIMIT),
        name="ffn_ln",
    )(x, wg, wu, wd, g, b)


def _inproj_kernel(x_ref, w_ref, o_ref):
    o_ref[...] = _dot(x_ref[...].astype(BF16), w_ref[...])


def _in_proj(x, w_pad, *, tm=512):
    n = x.shape[0]
    return pl.pallas_call(
        _inproj_kernel,
        grid=(n // tm,),
        in_specs=[
            pl.BlockSpec((tm, D_MODEL), lambda i: (i, 0)),
            pl.BlockSpec((D_MODEL, D_IN_PAD), lambda i: (0, 0)),
        ],
        out_specs=pl.BlockSpec((tm, D_IN_PAD), lambda i: (i, 0)),
        out_shape=jax.ShapeDtypeStruct((n, D_IN_PAD), F32),
        compiler_params=pltpu.CompilerParams(
            dimension_semantics=("parallel",), vmem_limit_bytes=VMEM_LIMIT),
        name="in_proj",
    )(x, w_pad)


N_CMP_PAD = 128


def _gelu_tanh(x):
    return 0.5 * x * (1.0 + jnp.tanh(np.sqrt(2.0 / np.pi) * (x + 0.044715 * (x * x * x))))


def _compress_kernel(zk_ref, zv_ref, pe_ref, w1_ref, w2_ref, ok_ref, ov_ref):
    def one(z_ref, which, o_ref):
        slabs = [z_ref[pl.ds(q, N_CMP_PAD, stride=CMP_STRIDE), :] for q in range(CMP_STRIDE)]
        cat = jnp.concatenate(slabs, axis=1)
        half = CMP_STRIDE * LANES
        top = _dot((cat + pe_ref[which, :, :half]).astype(BF16), w1_ref[which, :half, :])
        bot = _dot((cat + pe_ref[which, :, half:]).astype(BF16), w1_ref[which, half:, :])
        pre = top + pltpu.roll(bot, N_CMP_PAD - 1, 0)
        o_ref[...] = _dot(_gelu_tanh(pre).astype(BF16), w2_ref[which])

    one(zk_ref, 0, ok_ref)
    one(zv_ref, 1, ov_ref)


def _compress(h, pe_rows, w1_bd, w2_bd, batch, seq):
    kc_blk = 512 // LANES
    return pl.pallas_call(
        _compress_kernel,
        grid=(batch,),
        in_specs=[
            pl.BlockSpec((seq, LANES), lambda b: (b, kc_blk)),
            pl.BlockSpec((seq, LANES), lambda b: (b, kc_blk + 1)),
            pl.BlockSpec((2, 1, CMP_BLOCK * LANES), lambda b: (0, 0, 0)),
            pl.BlockSpec((2, CMP_BLOCK * LANES, LANES), lambda b: (0, 0, 0)),
            pl.BlockSpec((2, LANES, LANES), lambda b: (0, 0, 0)),
        ],
        out_specs=[
            pl.BlockSpec((N_CMP_PAD, LANES), lambda b: (b, 0)),
            pl.BlockSpec((N_CMP_PAD, LANES), lambda b: (b, 0)),
        ],
        out_shape=[jax.ShapeDtypeStruct((batch * N_CMP_PAD, LANES), F32)] * 2,
        compiler_params=pltpu.CompilerParams(
            dimension_semantics=("parallel",), vmem_limit_bytes=VMEM_LIMIT),
        name="nsa_compress",
    )(h, h, pe_rows, w1_bd, w2_bd)


N_SLC = 32
WIN_KEYS = WINDOW + Q_BLOCK
KEY_CHUNK = 512
GQ = NSA_GROUP_SIZE * Q_BLOCK
PANEL = 256
SCORE_LOOKAHEAD = 6


VT_ROWS = NSA_HEAD_DIM + 16
LOG2E = 1.4426950408889634


def _softmax_cols(s):
    m = jnp.max(s, axis=0, keepdims=True)
    m = jnp.where(m > 0.5 * NEG, m, 0.0)
    e = jnp.exp2(s - m)
    d = jnp.sum(e, axis=0, keepdims=True)
    inv = 1.0 / jnp.where(d > 0.0, d, 1.0)
    return e, inv


def _tile4(x):
    return jnp.concatenate([x] * NSA_GROUP_SIZE, axis=1)


def _values_t(v_both, g):
    n = v_both.shape[0]
    v_t = v_both.T[g * NSA_HEAD_DIM:(g + 1) * NSA_HEAD_DIM, :]
    extra = (lax.broadcasted_iota(jnp.int32, (VT_ROWS - NSA_HEAD_DIM, n), 0) == 0).astype(F32)
    return jnp.concatenate([v_t, extra], axis=0).astype(BF16)


def _nsa_kernel(q_ref, kc_ref, vc_ref, ksvs_ref, kwvw_ref, gl_ref, o_ref,
                ks_s, vst_s, kw_s, vwt_s, kc_s, vct_s, *, seq):
    qi = pl.program_id(1)
    start = qi * Q_BLOCK
    hd = NSA_HEAD_DIM
    n_chunks = seq // KEY_CHUNK
    n_qb = seq // Q_BLOCK

    @pl.when(qi == 0)
    def _():
        key_blk = lax.broadcasted_iota(jnp.int32, (seq, LANES), 0) // SLC_BLOCK
        blk = lax.broadcasted_iota(jnp.int32, (seq, LANES), 1)
        ks_s[:, :NSA_KV_DIM] = ksvs_ref[:, :NSA_KV_DIM].astype(BF16)
        ks_s[:, NSA_KV_DIM:] = (key_blk == blk).astype(BF16)
        kw_s[...] = kwvw_ref[:, :NSA_KV_DIM].astype(BF16)
        for g in range(NSA_KV_GROUPS):
            for c in range(n_chunks):
                vst_s[c, g] = _values_t(ksvs_ref[c * KEY_CHUNK:(c + 1) * KEY_CHUNK, NSA_KV_DIM:], g)
            for j in range(n_qb):
                vwt_s[j, g] = _values_t(kwvw_ref[j * Q_BLOCK:(j + 1) * Q_BLOCK, NSA_KV_DIM:], g)
        kc_s[...] = kc_ref[...].astype(BF16)
        vct_s[...] = vc_ref[...].T.astype(BF16)

    t_row = start + lax.broadcasted_iota(jnp.int32, (1, Q_BLOCK), 1)

    q_t = (q_ref[...] * (hd ** -0.5 * LOG2E)).T.astype(BF16)
    zeros_half = jnp.zeros((hd, GQ), BF16)
    q_ops = []
    for g in range(NSA_KV_GROUPS):
        top = jnp.concatenate([q_t[(g * NSA_GROUP_SIZE + r) * hd:(g * NSA_GROUP_SIZE + r + 1) * hd, :]
                               for r in range(NSA_GROUP_SIZE)], axis=1)
        q_ops.append(jnp.concatenate([top, zeros_half] if g == 0 else [zeros_half, top], axis=0))

    n_sub = lax.broadcasted_iota(jnp.int32, (N_CMP_PAD, 1), 0)
    bias_c = jnp.where(n_sub * CMP_STRIDE + (CMP_BLOCK - 1) <= t_row, 0.0, NEG)
    m_sub = lax.broadcasted_iota(jnp.int32, (N_SLC, N_CMP_PAD), 0)
    n_lane = lax.broadcasted_iota(jnp.int32, (N_SLC, N_CMP_PAD), 1)
    c0 = n_lane * CMP_STRIDE
    s0 = m_sub * SLC_BLOCK
    ov_t = ((c0 <= s0 + SLC_BLOCK - 1) & (c0 + CMP_BLOCK - 1 >= s0)
            & (n_lane < seq // CMP_STRIDE - 1)).astype(BF16)
    m_idx = lax.broadcasted_iota(jnp.int32, (N_SLC, Q_BLOCK), 0)
    cur = t_row // SLC_BLOCK
    forced = (m_idx == 0) | (m_idx == cur) | (m_idx == cur - 1)
    future = m_idx * SLC_BLOCK > t_row

    o_c = []
    sels = []
    for g in range(NSA_KV_GROUPS):
        e, inv = _softmax_cols(_dot(kc_s[...], q_ops[g]) + _tile4(bias_c))
        p_c = e * inv
        o_c.append(_dot(vct_s[g * hd:(g + 1) * hd, :], p_c.astype(BF16)))
        p_sum = (p_c[:, 0:Q_BLOCK] + p_c[:, Q_BLOCK:2 * Q_BLOCK]
                 + p_c[:, 2 * Q_BLOCK:3 * Q_BLOCK] + p_c[:, 3 * Q_BLOCK:])
        hi, mid, lo = _split3(p_sum)
        imp = _dot(ov_t, hi) + _dot(ov_t, mid) + _dot(ov_t, lo)
        imp = jnp.where(forced, -NEG, jnp.where(future, NEG, imp))
        rank = jnp.zeros((N_SLC, Q_BLOCK), jnp.int32)
        for m2 in range(N_SLC):
            row = imp[m2:m2 + 1, :]
            ahead = (row > imp) | ((row == imp) & (m2 < m_idx))
            rank = rank + ahead.astype(jnp.int32)
        sel_bias = _tile4(jnp.where((rank < SLC_TOPK) & jnp.logical_not(future), 0.0, NEG)).astype(BF16)
        sels.append(jnp.concatenate([q_ops[g], sel_bias, jnp.zeros((LANES - N_SLC, GQ), BF16)], axis=0))

    j0 = jnp.maximum(qi - WINDOW // Q_BLOCK, 0)
    w0 = pl.multiple_of(j0 * Q_BLOCK, Q_BLOCK)
    panels = [(g, hp) for g in range(NSA_KV_GROUPS) for hp in range(GQ // PANEL)]
    init = (jnp.full((1, PANEL), NEG, F32), jnp.zeros((VT_ROWS, PANEL), F32)) * len(panels)

    def online_update(s, m_p, acc, v_t):
        m_n = jnp.maximum(m_p, jnp.max(s, axis=0, keepdims=True))
        return m_n, jnp.exp2(m_p - m_n) * acc + _dot(v_t, jnp.exp2(s - m_n).astype(BF16))

    def normalised(state):
        outs = []
        for g in range(NSA_KV_GROUPS):
            accs = [state[2 * panels.index((g, hp)) + 1] for hp in range(GQ // PANEL)]
            outs.append(jnp.concatenate([a[:hd] * (1.0 / a[hd:hd + 1]) for a in accs], axis=1))
        return outs

    def run_items(items, state):
        state = list(state)
        ahead = [it[1]() for it in items[:SCORE_LOOKAHEAD]]
        for n, (i, _, values) in enumerate(items):
            s = ahead.pop(0)
            if n + SCORE_LOOKAHEAD < len(items):
                ahead.append(items[n + SCORE_LOOKAHEAD][1]())
            state[2 * i], state[2 * i + 1] = online_update(s, state[2 * i], state[2 * i + 1], values())
        return state

    items = []
    for o, n in ((0, 2 * Q_BLOCK), (2 * Q_BLOCK, 2 * Q_BLOCK), (4 * Q_BLOCK, Q_BLOCK)):
        kp_w = w0 + o + lax.broadcasted_iota(jnp.int32, (n, 1), 0)
        bias_w = jnp.where((kp_w <= t_row) & (kp_w > t_row - WINDOW), 0.0, NEG)
        bias_w = jnp.concatenate([bias_w, bias_w], axis=1)
        for i, (g, hp) in enumerate(panels):
            def score(o=o, n=n, g=g, hp=hp, bias_w=bias_w):
                return _dot(kw_s[pl.ds(w0 + o, n), :], q_ops[g][:, hp * PANEL:(hp + 1) * PANEL]) + bias_w

            def values(o=o, n=n, g=g):
                return jnp.concatenate([vwt_s[j0 + o // Q_BLOCK + j, g] for j in range(n // Q_BLOCK)], axis=1)
            items.append((i, score, values))
    window_items = items

    def chunk_items(c, diagonal, first_state):
        items = []
        for sub in range(KEY_CHUNK // PANEL):
            off = pl.multiple_of(c * KEY_CHUNK + sub * PANEL, PANEL)
            causal_bias = None
            if diagonal:
                kpos = off + lax.broadcasted_iota(jnp.int32, (PANEL, 1), 0)
                causal_bias = jnp.where(kpos <= t_row, 0.0, NEG)
                causal_bias = jnp.concatenate([causal_bias, causal_bias], axis=1)
            for i, (g, hp) in enumerate(panels):
                def score(off=off, g=g, hp=hp, causal_bias=causal_bias):
                    s = _dot(ks_s[pl.ds(off, PANEL), :], sels[g][:, hp * PANEL:(hp + 1) * PANEL])
                    return s if causal_bias is None else s + causal_bias

                def values(sub=sub, g=g):
                    return vst_s[c, g, :, sub * PANEL:(sub + 1) * PANEL]
                items.append((first_state + i, score, values))
        return items

    c_diag = qi // (KEY_CHUNK // Q_BLOCK)
    both = run_items(window_items + chunk_items(c_diag, True, len(panels)), init + init)
    o_w = normalised(both[:len(init)])
    fin = lax.fori_loop(0, c_diag, lambda c, carry: tuple(run_items(chunk_items(c, False, 0), carry)),
                        tuple(both[len(init):]))
    o_s = normalised(fin)

    gates_t = jax.nn.sigmoid(gl_ref[...]).T
    outs = []
    for h in range(NSA_HEADS):
        g, r = divmod(h, NSA_GROUP_SIZE)
        lanes = slice(r * Q_BLOCK, (r + 1) * Q_BLOCK)
        outs.append(gates_t[3 * h:3 * h + 1, :] * o_c[g][:, lanes]
                    + gates_t[3 * h + 1:3 * h + 2, :] * o_s[g][:, lanes]
                    + gates_t[3 * h + 2:3 * h + 3, :] * o_w[g][:, lanes])
    o_ref[...] = jnp.concatenate(outs, axis=0).T.astype(o_ref.dtype)


def _nsa(h, kc, vc, batch, seq):
    nqb = seq // Q_BLOCK
    return pl.pallas_call(
        functools.partial(_nsa_kernel, seq=seq),
        grid=(batch, nqb),
        in_specs=[
            pl.BlockSpec((Q_BLOCK, NSA_OUT), lambda b, i: (b * nqb + i, 0)),
            pl.BlockSpec((N_CMP_PAD, LANES), lambda b, i: (b, 0)),
            pl.BlockSpec((N_CMP_PAD, LANES), lambda b, i: (b, 0)),
            pl.BlockSpec((seq, 2 * NSA_KV_DIM), lambda b, i: (b, 768 // 256)),
            pl.BlockSpec((seq, 2 * NSA_KV_DIM), lambda b, i: (b, 1024 // 256)),
            pl.BlockSpec((Q_BLOCK, LANES), lambda b, i: (b * nqb + i, 2304 // LANES)),
        ],
        out_specs=pl.BlockSpec((Q_BLOCK, NSA_OUT), lambda b, i: (b * nqb + i, 0)),
        out_shape=jax.ShapeDtypeStruct((batch * seq, NSA_OUT), BF16),
        scratch_shapes=[
            pltpu.VMEM((seq, NSA_KV_DIM + LANES), BF16),
            pltpu.VMEM((seq // KEY_CHUNK, NSA_KV_GROUPS, VT_ROWS, KEY_CHUNK), BF16),
            pltpu.VMEM((seq, NSA_KV_DIM), BF16),
            pltpu.VMEM((nqb, NSA_KV_GROUPS, VT_ROWS, Q_BLOCK), BF16),
            pltpu.VMEM((N_CMP_PAD, NSA_KV_DIM), BF16),
            pltpu.VMEM((NSA_KV_DIM, N_CMP_PAD), BF16),
        ],
        compiler_params=pltpu.CompilerParams(
            dimension_semantics=("parallel", "arbitrary"), vmem_limit_bytes=VMEM_LIMIT),
        name="nsa_attn",
    )(h, kc, vc, h, h, h)


def _shift_rows(x, k, row):
    return jnp.where(row >= k, pltpu.roll(x, k, 0), 0.0)


def _pool_kernel(u_ref, w_ref, sc_ref, o_ref, *, seq):
    u = u_ref[...]
    row = lax.broadcasted_iota(jnp.int32, (seq, 1), 0)
    lane = lax.broadcasted_iota(jnp.int32, (1, POOL_DIM), 1)
    tp1 = (row + 1).astype(F32)
    acc = u
    mean = jnp.zeros_like(u)
    span = 1
    for gi, w in enumerate(POOL_WINDOWS):
        while span < w:
            acc = acc + _shift_rows(acc, span, row)
            span *= 2
        cnt = jnp.minimum(float(w), tp1)
        in_group = (lane >= gi * POOL_GROUP_DIM) & (lane < (gi + 1) * POOL_GROUP_DIM)
        mean = jnp.where(in_group, acc / cnt, mean)
    pooled = mean - u
    o_ref[...] = (_dot(pooled.astype(BF16), w_ref[...]) * sc_ref[...]).astype(o_ref.dtype)


def _pool(h, w_bd, scale, batch, seq):
    return pl.pallas_call(
        functools.partial(_pool_kernel, seq=seq),
        grid=(batch,),
        in_specs=[
            pl.BlockSpec((seq, POOL_DIM), lambda b: (b, 1280 // 256)),
            pl.BlockSpec((POOL_DIM, POOL_DIM), lambda b: (0, 0)),
            pl.BlockSpec((1, POOL_DIM), lambda b: (0, 0)),
        ],
        out_specs=pl.BlockSpec((seq, POOL_DIM), lambda b: (b, 0)),
        out_shape=jax.ShapeDtypeStruct((batch * seq, POOL_DIM), BF16),
        compiler_params=pltpu.CompilerParams(
            dimension_semantics=("parallel",), vmem_limit_bytes=VMEM_LIMIT),
        name="pool_mix",
    )(h, w_bd, scale)


def _gla_kernel(q_ref, k_ref, v_ref, a_ref, r_ref, wa2_ref, ba_ref, ng_ref, o_ref,
                qt_ref, kt_ref, kd_ref, dec_ref, oacc_ref, *, seq):
    C, H, DK, DV = GLA_CHUNK, GLA_HEADS, GLA_KEY_DIM, GLA_VAL_DIM
    nc = seq // C
    z = _dot(a_ref[...].astype(BF16), wa2_ref[...]) + ba_ref[...]
    log_a = (jnp.minimum(z, 0.0) - jnp.log(1.0 + jnp.exp(-jnp.abs(z)))) / GLA_TAU
    pos = lax.broadcasted_iota(jnp.int32, (seq, 1), 0) % C
    b = log_a
    step = 1
    while step < C:
        b = b + jnp.where(pos >= step, pltpu.roll(b, step, 0), 0.0)
        step *= 2
    b3 = b.reshape(nc, C, GLA_QK)
    b_last = b3[:, C - 1:C, :]
    qt_ref[...] = q_ref[...] * (DK ** -0.5) * jnp.exp(b)
    kt_ref[...] = k_ref[...] * jnp.exp(-b)
    kd_ref[...] = (k_ref[...].reshape(nc, C, GLA_QK) * jnp.exp(b_last - b3)).reshape(seq, GLA_QK)
    dec_ref[...] = jnp.exp(b_last)

    r_k = lax.broadcasted_iota(jnp.int32, (H * C, GLA_QK), 0) // C
    c_k = lax.broadcasted_iota(jnp.int32, (H * C, GLA_QK), 1) // DK
    mask_k = r_k == c_k
    r_v = lax.broadcasted_iota(jnp.int32, (H * C, GLA_OUT), 0) // C
    c_v = lax.broadcasted_iota(jnp.int32, (H * C, GLA_OUT), 1) // DV
    mask_v = r_v == c_v
    r_s = lax.broadcasted_iota(jnp.int32, (GLA_OUT, GLA_QK), 0) // DV
    c_s = lax.broadcasted_iota(jnp.int32, (GLA_OUT, GLA_QK), 1) // DK
    mask_s = r_s == c_s
    i_a = lax.broadcasted_iota(jnp.int32, (C, H * C), 0)
    j_a = lax.broadcasted_iota(jnp.int32, (C, H * C), 1) % C
    tril = j_a <= i_a

    def chunk(n, state_t):
        rows = pl.ds(pl.multiple_of(n * C, C), C)
        q_t = qt_ref[rows, :].astype(BF16)
        k_t = kt_ref[rows, :]
        k_d = kd_ref[rows, :].astype(BF16)
        v_c = v_ref[rows, :]
        k_bd = jnp.where(mask_k, jnp.concatenate([k_t] * H, axis=0), 0.0).astype(BF16)
        a_cat = jnp.where(tril, _dot_nt(q_t, k_bd), 0.0)
        v_bd = jnp.where(mask_v, jnp.concatenate([v_c] * H, axis=0), 0.0).astype(BF16)
        o_intra = _dot(a_cat.astype(BF16), v_bd)
        o_inter = _dot_nt(q_t, state_t.astype(BF16))
        oacc_ref[rows, :] = o_intra + o_inter
        d_state = jnp.where(mask_s, _dot(v_c.T.astype(BF16), k_d), 0.0)
        return state_t * dec_ref[n] + d_state

    lax.fori_loop(0, nc, chunk, jnp.zeros((GLA_OUT, GLA_QK), F32))

    o = oacc_ref[...]
    gr = lax.broadcasted_iota(jnp.int32, (GLA_OUT, GLA_OUT), 0) // DV
    gc = lax.broadcasted_iota(jnp.int32, (GLA_OUT, GLA_OUT), 1) // DV
    group_mean = jnp.where(gr == gc, 1.0 / DV, 0.0).astype(BF16)
    hi, mid, lo = _split3(o * o)
    ms = _dot(hi, group_mean) + _dot(mid, group_mean) + _dot(lo, group_mean)
    o = o * lax.rsqrt(ms + RMS_EPS) * ng_ref[...]
    r = r_ref[...]
    o_ref[...] = (o * (r * jax.nn.sigmoid(r))).astype(o_ref.dtype)


def _gla(h, wa2_pad, ba, norm_g, batch, seq):
    nc = seq // GLA_CHUNK
    return pl.pallas_call(
        functools.partial(_gla_kernel, seq=seq),
        grid=(batch,),
        in_specs=[
            pl.BlockSpec((seq, GLA_QK), lambda b: (b, 2048 // LANES)),
            pl.BlockSpec((seq, GLA_QK), lambda b: (b, 2176 // LANES)),
            pl.BlockSpec((seq, GLA_OUT), lambda b: (b, 1536 // 256)),
            pl.BlockSpec((seq, LANES), lambda b: (b, 2432 // LANES)),
            pl.BlockSpec((seq, GLA_OUT), lambda b: (b, 1792 // 256)),
            pl.BlockSpec((LANES, GLA_QK), lambda b: (0, 0)),
            pl.BlockSpec((1, GLA_QK), lambda b: (0, 0)),
            pl.BlockSpec((1, GLA_OUT), lambda b: (0, 0)),
        ],
        out_specs=pl.BlockSpec((seq, GLA_OUT), lambda b: (b, 0)),
        out_shape=jax.ShapeDtypeStruct((batch * seq, GLA_OUT), BF16),
        scratch_shapes=[
            pltpu.VMEM((seq, GLA_QK), F32),
            pltpu.VMEM((seq, GLA_QK), F32),
            pltpu.VMEM((seq, GLA_QK), F32),
            pltpu.VMEM((nc, 1, GLA_QK), F32),
            pltpu.VMEM((seq, GLA_OUT), F32),
        ],
        compiler_params=pltpu.CompilerParams(
            dimension_semantics=("parallel",), vmem_limit_bytes=VMEM_LIMIT),
        name="gla_mix",
    )(h, h, h, h, h, wa2_pad, ba, norm_g)


def _outproj_kernel(x_ref, oa_ref, ob_ref, oc_ref, wa_ref, wb_ref, wc_ref, g_ref, b_ref, o_ref):
    m = _dot(oa_ref[...], wa_ref[...]) + _dot(ob_ref[...], wb_ref[...]) + _dot(oc_ref[...], wc_ref[...])
    o_ref[...] = _layer_norm(ALPHA * x_ref[...] + m, g_ref[...], b_ref[...])


def _out_proj_ln(x, o_a, o_b, o_c, w_a, w_b, w_c, g, b, *, tm=512):
    n = x.shape[0]
    row = lambda i: (i, 0)
    const = lambda i: (0, 0)
    return pl.pallas_call(
        _outproj_kernel,
        grid=(n // tm,),
        in_specs=[
            pl.BlockSpec((tm, D_MODEL), row),
            pl.BlockSpec((tm, NSA_OUT), row),
            pl.BlockSpec((tm, POOL_DIM), row),
            pl.BlockSpec((tm, GLA_OUT), row),
            pl.BlockSpec((NSA_OUT, D_MODEL), const),
            pl.BlockSpec((POOL_DIM, D_MODEL), const),
            pl.BlockSpec((GLA_OUT, D_MODEL), const),
            pl.BlockSpec((1, D_MODEL), const),
            pl.BlockSpec((1, D_MODEL), const),
        ],
        out_specs=pl.BlockSpec((tm, D_MODEL), row),
        out_shape=jax.ShapeDtypeStruct((n, D_MODEL), F32),
        compiler_params=pltpu.CompilerParams(
            dimension_semantics=("parallel",), vmem_limit_bytes=VMEM_LIMIT),
        name="out_proj_ln",
    )(x, o_a, o_b, o_c, w_a, w_b, w_c, g, b)


def _pad_cols(w, width):
    return jnp.pad(w, ((0, 0), (0, width - w.shape[1])))


def _prep_w_in(w_in):
    parts = [w_in[:, 0:1280]]
    for name in ("u", "gv", "gr", "gq", "gk"):
        s, wd, _ = _IN_LAYOUT[name]
        parts.append(w_in[:, s:s + wd])
    for name in ("gl", "ga"):
        s, wd, _ = _IN_LAYOUT[name]
        parts.append(_pad_cols(w_in[:, s:s + wd], LANES))
    return jnp.concatenate(parts, axis=1).astype(BF16)


def _block_diag(blocks):
    n, r, c = blocks.shape
    eye = jnp.eye(n, dtype=blocks.dtype)
    return jnp.einsum("grc,gh->grhc", blocks, eye).reshape(n * r, n * c)


def _prep_compress(cmp_pe, cmp_w1, cmp_w2):
    G, HD = NSA_KV_GROUPS, NSA_HEAD_DIM
    pe = jnp.tile(cmp_pe[:, :, None, :], (1, 1, G, 1)).reshape(2, 1, CMP_BLOCK * G * HD)
    w1 = cmp_w1.reshape(2, CMP_BLOCK, HD, HD)
    eye = jnp.eye(G, dtype=cmp_w1.dtype)
    w1_bd = jnp.einsum("kpde,gh->kpgdhe", w1, eye).reshape(2, CMP_BLOCK * G * HD, G * HD)
    w2_bd = jnp.einsum("kde,gh->kgdhe", cmp_w2, eye).reshape(2, G * HD, G * HD)
    return pe, w1_bd.astype(BF16), w2_bd.astype(BF16)


def kernel(x, ln_g, ln_b, ffn_wg, ffn_wu, ffn_wd, w_in, w_out, cmp_pe, cmp_w1, cmp_w2,
           pool_w, pool_scale, gla_wa2, gla_ba, gla_norm_g):
    batch, seq, _ = x.shape
    xf = x.reshape(batch * seq, D_MODEL)
    for l in range(DEPTH):
        lg = lambda i: ln_g[l, i].reshape(1, D_MODEL)
        lb = lambda i: ln_b[l, i].reshape(1, D_MODEL)
        xf = _ffn_ln(xf, *_ffn_weights(ffn_wg[l, 0], ffn_wu[l, 0], ffn_wd[l, 0]), lg(0), lb(0))
        h = _in_proj(xf, _prep_w_in(w_in[l]))
        pe_rows, w1_bd, w2_bd = _prep_compress(cmp_pe[l], cmp_w1[l], cmp_w2[l])
        kc, vc = _compress(h, pe_rows, w1_bd, w2_bd, batch, seq)
        o_a = _nsa(h, kc, vc, batch, seq)
        o_b = _pool(h, _block_diag(pool_w[l]).astype(BF16), pool_scale[l].reshape(1, POOL_DIM), batch, seq)
        wa2_pad = jnp.pad(gla_wa2[l], ((0, LANES - GLA_GATE_RANK), (0, 0))).astype(BF16)
        o_c = _gla(h, wa2_pad, gla_ba[l].reshape(1, GLA_QK), gla_norm_g[l].reshape(1, GLA_OUT), batch, seq)
        wo = w_out[l].astype(BF16)
        xf = _out_proj_ln(xf, o_a, o_b, o_c, wo[:NSA_OUT], wo[NSA_OUT:NSA_OUT + POOL_DIM],
                          wo[NSA_OUT + POOL_DIM:], lg(1), lb(1))
        xf = _ffn_ln(xf, *_ffn_weights(ffn_wg[l, 1], ffn_wu[l, 1], ffn_wd[l, 1]), lg(2), lb(2))
    return xf.reshape(batch, seq, D_MODEL)
```

```python
import functools

import numpy as np
import jax
import jax.numpy as jnp
from jax import lax
from jax.experimental import pallas as pl
from jax.experimental.pallas import tpu as pltpu

F32 = jnp.float32
BF16 = jnp.bfloat16

D_MODEL = 1024
DEPTH = 4
D_FF = 2816
NSA_HEADS = 8
NSA_KV_GROUPS = 2
NSA_HEAD_DIM = 64
NSA_GROUP_SIZE = NSA_HEADS // NSA_KV_GROUPS
NSA_OUT = NSA_HEADS * NSA_HEAD_DIM
NSA_KV_DIM = NSA_KV_GROUPS * NSA_HEAD_DIM
CMP_BLOCK = 32
CMP_STRIDE = 16
SLC_BLOCK = 64
SLC_TOPK = 8
WINDOW = 512
Q_BLOCK = 128
POOL_WINDOWS = (2, 4, 8, 16)
POOL_GROUPS = 4
POOL_GROUP_DIM = 64
POOL_DIM = POOL_GROUPS * POOL_GROUP_DIM
GLA_HEADS = 4
GLA_KEY_DIM = 32
GLA_VAL_DIM = 64
GLA_GATE_RANK = 16
GLA_TAU = 16.0
GLA_CHUNK = 64
GLA_QK = GLA_HEADS * GLA_KEY_DIM
GLA_OUT = GLA_HEADS * GLA_VAL_DIM
ALPHA = (2.0 * DEPTH) ** 0.25
LN_EPS = 1e-5
RMS_EPS = 1e-6

LANES = 128
NEG = -1e30
VMEM_LIMIT = 56 * 1024 * 1024

_IN_LAYOUT = {
    "qkv": (0, 1280, 0),
    "u": (1304, 256, 1280),
    "gv": (1816, 256, 1536),
    "gr": (2088, 256, 1792),
    "gq": (1560, 128, 2048),
    "gk": (1688, 128, 2176),
    "gl": (1280, 24, 2304),
    "ga": (2072, 16, 2432),
}
D_IN_PAD = 2560


def _dot(a, b):
    return jnp.dot(a, b, preferred_element_type=F32)


def _dot_nt(a, b):
    return lax.dot_general(a, b, (((1,), (1,)), ((), ())), preferred_element_type=F32)


def _layer_norm(y, g, b):
    mu = jnp.mean(y, axis=-1, keepdims=True)
    yc = y - mu
    var = jnp.mean(yc * yc, axis=-1, keepdims=True)
    return yc * lax.rsqrt(var + LN_EPS) * g + b


def _split3(x):
    hi = x.astype(BF16)
    r1 = x - hi.astype(F32)
    mid = r1.astype(BF16)
    lo = (r1 - mid.astype(F32)).astype(BF16)
    return hi, mid, lo


FF_CHUNK = 256


N_FF_CHUNKS = D_FF // FF_CHUNK


def _ffn_kernel(x_ref, wg_hbm, wu_hbm, wd_hbm, g_ref, b_ref, o_ref,
                xb_ref, acc_ref, wg_s, wu_s, wd_s, sg_ref, su_ref, sd_ref, sem):
    def stage_copies(j, slot):
        cols = pl.ds(j * FF_CHUNK, FF_CHUNK)
        return (pltpu.make_async_copy(wg_hbm.at[:, cols], sg_ref.at[slot], sem.at[0, slot]),
                pltpu.make_async_copy(wu_hbm.at[:, cols], su_ref.at[slot], sem.at[1, slot]),
                pltpu.make_async_copy(wd_hbm.at[cols, :], sd_ref.at[slot], sem.at[2, slot]))

    @pl.when(pl.program_id(0) == 0)
    def _():
        for cp in stage_copies(0, 0):
            cp.start()
        for j in range(N_FF_CHUNKS):
            slot = j % 2
            if j + 1 < N_FF_CHUNKS:
                for cp in stage_copies(j + 1, 1 - slot):
                    cp.start()
            for cp in stage_copies(j, slot):
                cp.wait()
            wg_s[j] = sg_ref[slot].astype(BF16)
            wu_s[j] = su_ref[slot].astype(BF16)
            wd_s[j] = sd_ref[slot].astype(BF16)

    xb_ref[...] = x_ref[...].astype(BF16)

    def contribution(j):
        xb = xb_ref[...]
        gate = _dot(xb, wg_s[j])
        up = _dot(xb, wu_s[j])
        act = (gate * jax.nn.sigmoid(gate)) * up
        return _dot(act.astype(BF16), wd_s[j])

    acc_ref[...] = contribution(0)

    def step(j, carry):
        acc_ref[...] += contribution(j)
        return carry

    lax.fori_loop(1, N_FF_CHUNKS, step, 0, unroll=2)
    y = ALPHA * x_ref[...] + 0.5 * acc_ref[...]
    o_ref[...] = _layer_norm(y, g_ref[...], b_ref[...])


def _ffn_ln(x, wg, wu, wd, g, b, *, tm=1024):
    n = x.shape[0]
    return pl.pallas_call(
        _ffn_kernel,
        grid=(n // tm,),
        in_specs=[
            pl.BlockSpec((tm, D_MODEL), lambda i: (i, 0)),
            pl.BlockSpec(memory_space=pl.ANY),
            pl.BlockSpec(memory_space=pl.ANY),
            pl.BlockSpec(memory_space=pl.ANY),
            pl.BlockSpec((1, D_MODEL), lambda i: (0, 0)),
            pl.BlockSpec((1, D_MODEL), lambda i: (0, 0)),
        ],
        out_specs=pl.BlockSpec((tm, D_MODEL), lambda i: (i, 0)),
        out_shape=jax.ShapeDtypeStruct((n, D_MODEL), F32),
        scratch_shapes=[
            pltpu.VMEM((tm, D_MODEL), BF16),
            pltpu.VMEM((tm, D_MODEL), F32),
            pltpu.VMEM((N_FF_CHUNKS, D_MODEL, FF_CHUNK), BF16),
            pltpu.VMEM((N_FF_CHUNKS, D_MODEL, FF_CHUNK), BF16),
            pltpu.VMEM((N_FF_CHUNKS, FF_CHUNK, D_MODEL), BF16),
            pltpu.VMEM((2, D_MODEL, FF_CHUNK), F32),
            pltpu.VMEM((2, D_MODEL, FF_CHUNK), F32),
            pltpu.VMEM((2, FF_CHUNK, D_MODEL), F32),
            pltpu.SemaphoreType.DMA((3, 2)),
        ],
        compiler_params=pltpu.CompilerParams(
            dimension_semantics=("arbitrary",), vmem_limit_bytes=VMEM_LIMIT),
        name="ffn_ln",
    )(x, wg, wu, wd, g, b)


def _inproj_kernel(x_ref, w_ref, o_ref):
    o_ref[...] = _dot(x_ref[...].astype(BF16), w_ref[...])


def _in_proj(x, w_pad, *, tm=512):
    n = x.shape[0]
    return pl.pallas_call(
        _inproj_kernel,
        grid=(n // tm,),
        in_specs=[
            pl.BlockSpec((tm, D_MODEL), lambda i: (i, 0)),
            pl.BlockSpec((D_MODEL, D_IN_PAD), lambda i: (0, 0)),
        ],
        out_specs=pl.BlockSpec((tm, D_IN_PAD), lambda i: (i, 0)),
        out_shape=jax.ShapeDtypeStruct((n, D_IN_PAD), F32),
        compiler_params=pltpu.CompilerParams(
            dimension_semantics=("parallel",), vmem_limit_bytes=VMEM_LIMIT),
        name="in_proj",
    )(x, w_pad)


N_CMP_PAD = 128


def _gelu_tanh(x):
    return 0.5 * x * (1.0 + jnp.tanh(np.sqrt(2.0 / np.pi) * (x + 0.044715 * (x * x * x))))


def _compress_kernel(zk_ref, zv_ref, pe_ref, w1_ref, w2_ref, ok_ref, ov_ref):
    def one(z_ref, which, o_ref):
        slabs = [z_ref[pl.ds(q, N_CMP_PAD, stride=CMP_STRIDE), :] for q in range(CMP_STRIDE)]
        cat = jnp.concatenate(slabs, axis=1)
        half = CMP_STRIDE * LANES
        top = _dot((cat + pe_ref[which, :, :half]).astype(BF16), w1_ref[which, :half, :])
        bot = _dot((cat + pe_ref[which, :, half:]).astype(BF16), w1_ref[which, half:, :])
        pre = top + pltpu.roll(bot, N_CMP_PAD - 1, 0)
        o_ref[...] = _dot(_gelu_tanh(pre).astype(BF16), w2_ref[which])

    one(zk_ref, 0, ok_ref)
    one(zv_ref, 1, ov_ref)


def _compress(h, pe_rows, w1_bd, w2_bd, batch, seq):
    kc_blk = 512 // LANES
    return pl.pallas_call(
        _compress_kernel,
        grid=(batch,),
        in_specs=[
            pl.BlockSpec((seq, LANES), lambda b: (b, kc_blk)),
            pl.BlockSpec((seq, LANES), lambda b: (b, kc_blk + 1)),
            pl.BlockSpec((2, 1, CMP_BLOCK * LANES), lambda b: (0, 0, 0)),
            pl.BlockSpec((2, CMP_BLOCK * LANES, LANES), lambda b: (0, 0, 0)),
            pl.BlockSpec((2, LANES, LANES), lambda b: (0, 0, 0)),
        ],
        out_specs=[
            pl.BlockSpec((N_CMP_PAD, LANES), lambda b: (b, 0)),
            pl.BlockSpec((N_CMP_PAD, LANES), lambda b: (b, 0)),
        ],
        out_shape=[jax.ShapeDtypeStruct((batch * N_CMP_PAD, LANES), F32)] * 2,
        compiler_params=pltpu.CompilerParams(
            dimension_semantics=("parallel",), vmem_limit_bytes=VMEM_LIMIT),
        name="nsa_compress",
    )(h, h, pe_rows, w1_bd, w2_bd)


N_SLC = 32
WIN_KEYS = WINDOW + Q_BLOCK
KEY_CHUNK = 512
GQ = NSA_GROUP_SIZE * Q_BLOCK
PANEL = 256
SCORE_LOOKAHEAD = 6


VT_ROWS = NSA_HEAD_DIM + 16
LOG2E = 1.4426950408889634


def _softmax_cols(s):
    m = jnp.max(s, axis=0, keepdims=True)
    m = jnp.where(m > 0.5 * NEG, m, 0.0)
    e = jnp.exp2(s - m)
    d = jnp.sum(e, axis=0, keepdims=True)
    inv = 1.0 / jnp.where(d > 0.0, d, 1.0)
    return e, inv


def _tile4(x):
    return jnp.concatenate([x] * NSA_GROUP_SIZE, axis=1)


def _values_t(v_both, g):
    n = v_both.shape[0]
    v_t = v_both.T[g * NSA_HEAD_DIM:(g + 1) * NSA_HEAD_DIM, :]
    extra = (lax.broadcasted_iota(jnp.int32, (VT_ROWS - NSA_HEAD_DIM, n), 0) == 0).astype(F32)
    return jnp.concatenate([v_t, extra], axis=0).astype(BF16)


def _nsa_kernel(q_ref, kc_ref, vc_ref, ksvs_ref, kwvw_ref, gl_ref, o_ref,
                ks_s, vst_s, kw_s, vwt_s, kc_s, vct_s, *, seq):
    qi = pl.program_id(1)
    start = qi * Q_BLOCK
    hd = NSA_HEAD_DIM
    n_chunks = seq // KEY_CHUNK
    n_qb = seq // Q_BLOCK

    @pl.when(qi == 0)
    def _():
        key_blk = lax.broadcasted_iota(jnp.int32, (seq, LANES), 0) // SLC_BLOCK
        blk = lax.broadcasted_iota(jnp.int32, (seq, LANES), 1)
        ks_s[:, :NSA_KV_DIM] = ksvs_ref[:, :NSA_KV_DIM].astype(BF16)
        ks_s[:, NSA_KV_DIM:] = (key_blk == blk).astype(BF16)
        kw_s[...] = kwvw_ref[:, :NSA_KV_DIM].astype(BF16)
        for g in range(NSA_KV_GROUPS):
            for c in range(n_chunks):
                vst_s[c, g] = _values_t(ksvs_ref[c * KEY_CHUNK:(c + 1) * KEY_CHUNK, NSA_KV_DIM:], g)
            for j in range(n_qb):
                vwt_s[j, g] = _values_t(kwvw_ref[j * Q_BLOCK:(j + 1) * Q_BLOCK, NSA_KV_DIM:], g)
        kc_s[...] = kc_ref[...].astype(BF16)
        vct_s[...] = vc_ref[...].T.astype(BF16)

    t_row = start + lax.broadcasted_iota(jnp.int32, (1, Q_BLOCK), 1)

    q_t = (q_ref[...] * (hd ** -0.5 * LOG2E)).T.astype(BF16)
    zeros_half = jnp.zeros((hd, GQ), BF16)
    q_ops = []
    for g in range(NSA_KV_GROUPS):
        top = jnp.concatenate([q_t[(g * NSA_GROUP_SIZE + r) * hd:(g * NSA_GROUP_SIZE + r + 1) * hd, :]
                               for r in range(NSA_GROUP_SIZE)], axis=1)
        q_ops.append(jnp.concatenate([top, zeros_half] if g == 0 else [zeros_half, top], axis=0))

    n_sub = lax.broadcasted_iota(jnp.int32, (N_CMP_PAD, 1), 0)
    bias_c = jnp.where(n_sub * CMP_STRIDE + (CMP_BLOCK - 1) <= t_row, 0.0, NEG)
    m_sub = lax.broadcasted_iota(jnp.int32, (N_SLC, N_CMP_PAD), 0)
    n_lane = lax.broadcasted_iota(jnp.int32, (N_SLC, N_CMP_PAD), 1)
    c0 = n_lane * CMP_STRIDE
    s0 = m_sub * SLC_BLOCK
    ov_t = ((c0 <= s0 + SLC_BLOCK - 1) & (c0 + CMP_BLOCK - 1 >= s0)
            & (n_lane < seq // CMP_STRIDE - 1)).astype(BF16)
    m_idx = lax.broadcasted_iota(jnp.int32, (N_SLC, Q_BLOCK), 0)
    cur = t_row // SLC_BLOCK
    forced = (m_idx == 0) | (m_idx == cur) | (m_idx == cur - 1)
    future = m_idx * SLC_BLOCK > t_row

    o_c = []
    sels = []
    for g in range(NSA_KV_GROUPS):
        e, inv = _softmax_cols(_dot(kc_s[...], q_ops[g]) + _tile4(bias_c))
        p_c = e * inv
        o_c.append(_dot(vct_s[g * hd:(g + 1) * hd, :], p_c.astype(BF16)))
        p_sum = (p_c[:, 0:Q_BLOCK] + p_c[:, Q_BLOCK:2 * Q_BLOCK]
                 + p_c[:, 2 * Q_BLOCK:3 * Q_BLOCK] + p_c[:, 3 * Q_BLOCK:])
        hi, mid, lo = _split3(p_sum)
        imp = _dot(ov_t, hi) + _dot(ov_t, mid) + _dot(ov_t, lo)
        imp = jnp.where(forced, -NEG, jnp.where(future, NEG, imp))
        rank = jnp.zeros((N_SLC, Q_BLOCK), jnp.int32)
        for m2 in range(N_SLC):
            row = imp[m2:m2 + 1, :]
            ahead = (row > imp) | ((row == imp) & (m2 < m_idx))
            rank = rank + ahead.astype(jnp.int32)
        sel_bias = _tile4(jnp.where((rank < SLC_TOPK) & jnp.logical_not(future), 0.0, NEG)).astype(BF16)
        sels.append(jnp.concatenate([q_ops[g], sel_bias, jnp.zeros((LANES - N_SLC, GQ), BF16)], axis=0))

    j0 = jnp.maximum(qi - WINDOW // Q_BLOCK, 0)
    w0 = pl.multiple_of(j0 * Q_BLOCK, Q_BLOCK)
    panels = [(g, hp) for g in range(NSA_KV_GROUPS) for hp in range(GQ // PANEL)]
    init = (jnp.full((1, PANEL), NEG, F32), jnp.zeros((VT_ROWS, PANEL), F32)) * len(panels)

    def online_update(s, m_p, acc, v_t):
        m_n = jnp.maximum(m_p, jnp.max(s, axis=0, keepdims=True))
        return m_n, jnp.exp2(m_p - m_n) * acc + _dot(v_t, jnp.exp2(s - m_n).astype(BF16))

    def normalised(state):
        outs = []
        for g in range(NSA_KV_GROUPS):
            accs = [state[2 * panels.index((g, hp)) + 1] for hp in range(GQ // PANEL)]
            outs.append(jnp.concatenate([a[:hd] * (1.0 / a[hd:hd + 1]) for a in accs], axis=1))
        return outs

    def run_items(items, state):
        state = list(state)
        ahead = [it[1]() for it in items[:SCORE_LOOKAHEAD]]
        for n, (i, _, values) in enumerate(items):
            s = ahead.pop(0)
            if n + SCORE_LOOKAHEAD < len(items):
                ahead.append(items[n + SCORE_LOOKAHEAD][1]())
            state[2 * i], state[2 * i + 1] = online_update(s, state[2 * i], state[2 * i + 1], values())
        return state

    items = []
    for o, n in ((0, 2 * Q_BLOCK), (2 * Q_BLOCK, 2 * Q_BLOCK), (4 * Q_BLOCK, Q_BLOCK)):
        kp_w = w0 + o + lax.broadcasted_iota(jnp.int32, (n, 1), 0)
        bias_w = jnp.where((kp_w <= t_row) & (kp_w > t_row - WINDOW), 0.0, NEG)
        bias_w = jnp.concatenate([bias_w, bias_w], axis=1)
        for i, (g, hp) in enumerate(panels):
            def score(o=o, n=n, g=g, hp=hp, bias_w=bias_w):
                return _dot(kw_s[pl.ds(w0 + o, n), :], q_ops[g][:, hp * PANEL:(hp + 1) * PANEL]) + bias_w

            def values(o=o, n=n, g=g):
                return jnp.concatenate([vwt_s[j0 + o // Q_BLOCK + j, g] for j in range(n // Q_BLOCK)], axis=1)
            items.append((i, score, values))
    window_items = items

    def chunk_items(c, diagonal, first_state):
        items = []
        for sub in range(KEY_CHUNK // PANEL):
            off = pl.multiple_of(c * KEY_CHUNK + sub * PANEL, PANEL)
            causal_bias = None
            if diagonal:
                kpos = off + lax.broadcasted_iota(jnp.int32, (PANEL, 1), 0)
                causal_bias = jnp.where(kpos <= t_row, 0.0, NEG)
                causal_bias = jnp.concatenate([causal_bias, causal_bias], axis=1)
            for i, (g, hp) in enumerate(panels):
                def score(off=off, g=g, hp=hp, causal_bias=causal_bias):
                    s = _dot(ks_s[pl.ds(off, PANEL), :], sels[g][:, hp * PANEL:(hp + 1) * PANEL])
                    return s if causal_bias is None else s + causal_bias

                def values(sub=sub, g=g):
                    return vst_s[c, g, :, sub * PANEL:(sub + 1) * PANEL]
                items.append((first_state + i, score, values))
        return items

    c_diag = qi // (KEY_CHUNK // Q_BLOCK)
    both = run_items(window_items + chunk_items(c_diag, True, len(panels)), init + init)
    o_w = normalised(both[:len(init)])
    fin = lax.fori_loop(0, c_diag, lambda c, carry: tuple(run_items(chunk_items(c, False, 0), carry)),
                        tuple(both[len(init):]))
    o_s = normalised(fin)

    gates_t = jax.nn.sigmoid(gl_ref[...]).T
    outs = []
    for h in range(NSA_HEADS):
        g, r = divmod(h, NSA_GROUP_SIZE)
        lanes = slice(r * Q_BLOCK, (r + 1) * Q_BLOCK)
        outs.append(gates_t[3 * h:3 * h + 1, :] * o_c[g][:, lanes]
                    + gates_t[3 * h + 1:3 * h + 2, :] * o_s[g][:, lanes]
                    + gates_t[3 * h + 2:3 * h + 3, :] * o_w[g][:, lanes])
    o_ref[...] = jnp.concatenate(outs, axis=0).T.astype(o_ref.dtype)


def _nsa(h, kc, vc, batch, seq):
    nqb = seq // Q_BLOCK
    return pl.pallas_call(
        functools.partial(_nsa_kernel, seq=seq),
        grid=(batch, nqb),
        in_specs=[
            pl.BlockSpec((Q_BLOCK, NSA_OUT), lambda b, i: (b * nqb + i, 0)),
            pl.BlockSpec((N_CMP_PAD, LANES), lambda b, i: (b, 0)),
            pl.BlockSpec((N_CMP_PAD, LANES), lambda b, i: (b, 0)),
            pl.BlockSpec((seq, 2 * NSA_KV_DIM), lambda b, i: (b, 768 // 256)),
            pl.BlockSpec((seq, 2 * NSA_KV_DIM), lambda b, i: (b, 1024 // 256)),
            pl.BlockSpec((Q_BLOCK, LANES), lambda b, i: (b * nqb + i, 2304 // LANES)),
        ],
        out_specs=pl.BlockSpec((Q_BLOCK, NSA_OUT), lambda b, i: (b * nqb + i, 0)),
        out_shape=jax.ShapeDtypeStruct((batch * seq, NSA_OUT), BF16),
        scratch_shapes=[
            pltpu.VMEM((seq, NSA_KV_DIM + LANES), BF16),
            pltpu.VMEM((seq // KEY_CHUNK, NSA_KV_GROUPS, VT_ROWS, KEY_CHUNK), BF16),
            pltpu.VMEM((seq, NSA_KV_DIM), BF16),
            pltpu.VMEM((nqb, NSA_KV_GROUPS, VT_ROWS, Q_BLOCK), BF16),
            pltpu.VMEM((N_CMP_PAD, NSA_KV_DIM), BF16),
            pltpu.VMEM((NSA_KV_DIM, N_CMP_PAD), BF16),
        ],
        compiler_params=pltpu.CompilerParams(
            dimension_semantics=("parallel", "arbitrary"), vmem_limit_bytes=VMEM_LIMIT),
        name="nsa_attn",
    )(h, kc, vc, h, h, h)


def _shift_rows(x, k, row):
    return jnp.where(row >= k, pltpu.roll(x, k, 0), 0.0)


def _pool_kernel(u_ref, w_ref, sc_ref, o_ref, *, seq):
    u = u_ref[...]
    row = lax.broadcasted_iota(jnp.int32, (seq, 1), 0)
    lane = lax.broadcasted_iota(jnp.int32, (1, POOL_DIM), 1)
    tp1 = (row + 1).astype(F32)
    acc = u
    mean = jnp.zeros_like(u)
    span = 1
    for gi, w in enumerate(POOL_WINDOWS):
        while span < w:
            acc = acc + _shift_rows(acc, span, row)
            span *= 2
        cnt = jnp.minimum(float(w), tp1)
        in_group = (lane >= gi * POOL_GROUP_DIM) & (lane < (gi + 1) * POOL_GROUP_DIM)
        mean = jnp.where(in_group, acc / cnt, mean)
    pooled = mean - u
    o_ref[...] = (_dot(pooled.astype(BF16), w_ref[...]) * sc_ref[...]).astype(o_ref.dtype)


def _pool(h, w_bd, scale, batch, seq):
    return pl.pallas_call(
        functools.partial(_pool_kernel, seq=seq),
        grid=(batch,),
        in_specs=[
            pl.BlockSpec((seq, POOL_DIM), lambda b: (b, 1280 // 256)),
            pl.BlockSpec((POOL_DIM, POOL_DIM), lambda b: (0, 0)),
            pl.BlockSpec((1, POOL_DIM), lambda b: (0, 0)),
        ],
        out_specs=pl.BlockSpec((seq, POOL_DIM), lambda b: (b, 0)),
        out_shape=jax.ShapeDtypeStruct((batch * seq, POOL_DIM), BF16),
        compiler_params=pltpu.CompilerParams(
            dimension_semantics=("parallel",), vmem_limit_bytes=VMEM_LIMIT),
        name="pool_mix",
    )(h, w_bd, scale)


def _gla_kernel(q_ref, k_ref, v_ref, a_ref, r_ref, wa2_ref, ba_ref, ng_ref, o_ref,
                qt_ref, kt_ref, kd_ref, dec_ref, oacc_ref, *, seq):
    C, H, DK, DV = GLA_CHUNK, GLA_HEADS, GLA_KEY_DIM, GLA_VAL_DIM
    nc = seq // C
    z = _dot(a_ref[...].astype(BF16), wa2_ref[...]) + ba_ref[...]
    log_a = (jnp.minimum(z, 0.0) - jnp.log(1.0 + jnp.exp(-jnp.abs(z)))) / GLA_TAU
    pos = lax.broadcasted_iota(jnp.int32, (seq, 1), 0) % C
    b = log_a
    step = 1
    while step < C:
        b = b + jnp.where(pos >= step, pltpu.roll(b, step, 0), 0.0)
        step *= 2
    b3 = b.reshape(nc, C, GLA_QK)
    b_last = b3[:, C - 1:C, :]
    qt_ref[...] = q_ref[...] * (DK ** -0.5) * jnp.exp(b)
    kt_ref[...] = k_ref[...] * jnp.exp(-b)
    kd_ref[...] = (k_ref[...].reshape(nc, C, GLA_QK) * jnp.exp(b_last - b3)).reshape(seq, GLA_QK)
    dec_ref[...] = jnp.exp(b_last)

    r_k = lax.broadcasted_iota(jnp.int32, (H * C, GLA_QK), 0) // C
    c_k = lax.broadcasted_iota(jnp.int32, (H * C, GLA_QK), 1) // DK
    mask_k = r_k == c_k
    r_v = lax.broadcasted_iota(jnp.int32, (H * C, GLA_OUT), 0) // C
    c_v = lax.broadcasted_iota(jnp.int32, (H * C, GLA_OUT), 1) // DV
    mask_v = r_v == c_v
    r_s = lax.broadcasted_iota(jnp.int32, (GLA_OUT, GLA_QK), 0) // DV
    c_s = lax.broadcasted_iota(jnp.int32, (GLA_OUT, GLA_QK), 1) // DK
    mask_s = r_s == c_s
    i_a = lax.broadcasted_iota(jnp.int32, (C, H * C), 0)
    j_a = lax.broadcasted_iota(jnp.int32, (C, H * C), 1) % C
    tril = j_a <= i_a

    def chunk(n, state_t):
        rows = pl.ds(pl.multiple_of(n * C, C), C)
        q_t = qt_ref[rows, :].astype(BF16)
        k_t = kt_ref[rows, :]
        k_d = kd_ref[rows, :].astype(BF16)
        v_c = v_ref[rows, :]
        k_bd = jnp.where(mask_k, jnp.concatenate([k_t] * H, axis=0), 0.0).astype(BF16)
        a_cat = jnp.where(tril, _dot_nt(q_t, k_bd), 0.0)
        v_bd = jnp.where(mask_v, jnp.concatenate([v_c] * H, axis=0), 0.0).astype(BF16)
        o_intra = _dot(a_cat.astype(BF16), v_bd)
        o_inter = _dot_nt(q_t, state_t.astype(BF16))
        oacc_ref[rows, :] = o_intra + o_inter
        d_state = jnp.where(mask_s, _dot(v_c.T.astype(BF16), k_d), 0.0)
        return state_t * dec_ref[n] + d_state

    lax.fori_loop(0, nc, chunk, jnp.zeros((GLA_OUT, GLA_QK), F32), unroll=8)

    o = oacc_ref[...]
    gr = lax.broadcasted_iota(jnp.int32, (GLA_OUT, GLA_OUT), 0) // DV
    gc = lax.broadcasted_iota(jnp.int32, (GLA_OUT, GLA_OUT), 1) // DV
    group_mean = jnp.where(gr == gc, 1.0 / DV, 0.0).astype(BF16)
    hi, mid, lo = _split3(o * o)
    ms = _dot(hi, group_mean) + _dot(mid, group_mean) + _dot(lo, group_mean)
    o = o * lax.rsqrt(ms + RMS_EPS) * ng_ref[...]
    r = r_ref[...]
    o_ref[...] = (o * (r * jax.nn.sigmoid(r))).astype(o_ref.dtype)


def _gla(h, wa2_pad, ba, norm_g, batch, seq):
    nc = seq // GLA_CHUNK
    return pl.pallas_call(
        functools.partial(_gla_kernel, seq=seq),
        grid=(batch,),
        in_specs=[
            pl.BlockSpec((seq, GLA_QK), lambda b: (b, 2048 // LANES)),
            pl.BlockSpec((seq, GLA_QK), lambda b: (b, 2176 // LANES)),
            pl.BlockSpec((seq, GLA_OUT), lambda b: (b, 1536 // 256)),
            pl.BlockSpec((seq, LANES), lambda b: (b, 2432 // LANES)),
            pl.BlockSpec((seq, GLA_OUT), lambda b: (b, 1792 // 256)),
            pl.BlockSpec((LANES, GLA_QK), lambda b: (0, 0)),
            pl.BlockSpec((1, GLA_QK), lambda b: (0, 0)),
            pl.BlockSpec((1, GLA_OUT), lambda b: (0, 0)),
        ],
        out_specs=pl.BlockSpec((seq, GLA_OUT), lambda b: (b, 0)),
        out_shape=jax.ShapeDtypeStruct((batch * seq, GLA_OUT), BF16),
        scratch_shapes=[
            pltpu.VMEM((seq, GLA_QK), F32),
            pltpu.VMEM((seq, GLA_QK), F32),
            pltpu.VMEM((seq, GLA_QK), F32),
            pltpu.VMEM((nc, 1, GLA_QK), F32),
            pltpu.VMEM((seq, GLA_OUT), F32),
        ],
        compiler_params=pltpu.CompilerParams(
            dimension_semantics=("parallel",), vmem_limit_bytes=VMEM_LIMIT),
        name="gla_mix",
    )(h, h, h, h, h, wa2_pad, ba, norm_g)


def _outproj_kernel(x_ref, oa_ref, ob_ref, oc_ref, wa_ref, wb_ref, wc_ref, g_ref, b_ref, o_ref):
    m = _dot(oa_ref[...], wa_ref[...]) + _dot(ob_ref[...], wb_ref[...]) + _dot(oc_ref[...], wc_ref[...])
    o_ref[...] = _layer_norm(ALPHA * x_ref[...] + m, g_ref[...], b_ref[...])


def _out_proj_ln(x, o_a, o_b, o_c, w_a, w_b, w_c, g, b, *, tm=512):
    n = x.shape[0]
    row = lambda i: (i, 0)
    const = lambda i: (0, 0)
    return pl.pallas_call(
        _outproj_kernel,
        grid=(n // tm,),
        in_specs=[
            pl.BlockSpec((tm, D_MODEL), row),
            pl.BlockSpec((tm, NSA_OUT), row),
            pl.BlockSpec((tm, POOL_DIM), row),
            pl.BlockSpec((tm, GLA_OUT), row),
            pl.BlockSpec((NSA_OUT, D_MODEL), const),
            pl.BlockSpec((POOL_DIM, D_MODEL), const),
            pl.BlockSpec((GLA_OUT, D_MODEL), const),
            pl.BlockSpec((1, D_MODEL), const),
            pl.BlockSpec((1, D_MODEL), const),
        ],
        out_specs=pl.BlockSpec((tm, D_MODEL), row),
        out_shape=jax.ShapeDtypeStruct((n, D_MODEL), F32),
        compiler_params=pltpu.CompilerParams(
            dimension_semantics=("parallel",), vmem_limit_bytes=VMEM_LIMIT),
        name="out_proj_ln",
    )(x, o_a, o_b, o_c, w_a, w_b, w_c, g, b)


def _pad_cols(w, width):
    return jnp.pad(w, ((0, 0), (0, width - w.shape[1])))


def _prep_w_in(w_in):
    parts = [w_in[:, 0:1280]]
    for name in ("u", "gv", "gr", "gq", "gk"):
        s, wd, _ = _IN_LAYOUT[name]
        parts.append(w_in[:, s:s + wd])
    for name in ("gl", "ga"):
        s, wd, _ = _IN_LAYOUT[name]
        parts.append(_pad_cols(w_in[:, s:s + wd], LANES))
    return jnp.concatenate(parts, axis=1).astype(BF16)


def _block_diag(blocks):
    n, r, c = blocks.shape
    eye = jnp.eye(n, dtype=blocks.dtype)
    return jnp.einsum("grc,gh->grhc", blocks, eye).reshape(n * r, n * c)


def _prep_compress(cmp_pe, cmp_w1, cmp_w2):
    G, HD = NSA_KV_GROUPS, NSA_HEAD_DIM
    pe = jnp.tile(cmp_pe[:, :, None, :], (1, 1, G, 1)).reshape(2, 1, CMP_BLOCK * G * HD)
    w1 = cmp_w1.reshape(2, CMP_BLOCK, HD, HD)
    eye = jnp.eye(G, dtype=cmp_w1.dtype)
    w1_bd = jnp.einsum("kpde,gh->kpgdhe", w1, eye).reshape(2, CMP_BLOCK * G * HD, G * HD)
    w2_bd = jnp.einsum("kde,gh->kgdhe", cmp_w2, eye).reshape(2, G * HD, G * HD)
    return pe, w1_bd.astype(BF16), w2_bd.astype(BF16)


def kernel(x, ln_g, ln_b, ffn_wg, ffn_wu, ffn_wd, w_in, w_out, cmp_pe, cmp_w1, cmp_w2,
           pool_w, pool_scale, gla_wa2, gla_ba, gla_norm_g):
    batch, seq, _ = x.shape
    xf = x.reshape(batch * seq, D_MODEL)
    for l in range(DEPTH):
        lg = lambda i: ln_g[l, i].reshape(1, D_MODEL)
        lb = lambda i: ln_b[l, i].reshape(1, D_MODEL)
        xf = _ffn_ln(xf, ffn_wg[l, 0], ffn_wu[l, 0], ffn_wd[l, 0], lg(0), lb(0))
        h = _in_proj(xf, _prep_w_in(w_in[l]))
        pe_rows, w1_bd, w2_bd = _prep_compress(cmp_pe[l], cmp_w1[l], cmp_w2[l])
        kc, vc = _compress(h, pe_rows, w1_bd, w2_bd, batch, seq)
        o_a = _nsa(h, kc, vc, batch, seq)
        o_b = _pool(h, _block_diag(pool_w[l]).astype(BF16), pool_scale[l].reshape(1, POOL_DIM), batch, seq)
        wa2_pad = jnp.pad(gla_wa2[l], ((0, LANES - GLA_GATE_RANK), (0, 0))).astype(BF16)
        o_c = _gla(h, wa2_pad, gla_ba[l].reshape(1, GLA_QK), gla_norm_g[l].reshape(1, GLA_OUT), batch, seq)
        wo = w_out[l].astype(BF16)
        xf = _out_proj_ln(xf, o_a, o_b, o_c, wo[:NSA_OUT], wo[NSA_OUT:NSA_OUT + POOL_DIM],
                          wo[NSA_OUT + POOL_DIM:], lg(1), lb(1))
        xf = _ffn_ln(xf, ffn_wg[l, 1], ffn_wu[l, 1], ffn_wd[l, 1], lg(2), lb(2))
    return xf.reshape(batch, seq, D_MODEL)
```

```python
import functools

import numpy as np
import jax
import jax.numpy as jnp
from jax import lax
from jax.experimental import pallas as pl
from jax.experimental.pallas import tpu as pltpu

F32 = jnp.float32
BF16 = jnp.bfloat16

D_MODEL = 1024
DEPTH = 4
D_FF = 2816
NSA_HEADS = 8
NSA_KV_GROUPS = 2
NSA_HEAD_DIM = 64
NSA_GROUP_SIZE = NSA_HEADS // NSA_KV_GROUPS
NSA_OUT = NSA_HEADS * NSA_HEAD_DIM
NSA_KV_DIM = NSA_KV_GROUPS * NSA_HEAD_DIM
CMP_BLOCK = 32
CMP_STRIDE = 16
SLC_BLOCK = 64
SLC_TOPK = 8
WINDOW = 512
Q_BLOCK = 128
POOL_WINDOWS = (2, 4, 8, 16)
POOL_GROUPS = 4
POOL_GROUP_DIM = 64
POOL_DIM = POOL_GROUPS * POOL_GROUP_DIM
GLA_HEADS = 4
GLA_KEY_DIM = 32
GLA_VAL_DIM = 64
GLA_GATE_RANK = 16
GLA_TAU = 16.0
GLA_CHUNK = 64
GLA_QK = GLA_HEADS * GLA_KEY_DIM
GLA_OUT = GLA_HEADS * GLA_VAL_DIM
ALPHA = (2.0 * DEPTH) ** 0.25
LN_EPS = 1e-5
RMS_EPS = 1e-6

LANES = 128
NEG = -1e30
VMEM_LIMIT = 56 * 1024 * 1024

_IN_LAYOUT = {
    "qkv": (0, 1280, 0),
    "u": (1304, 256, 1280),
    "gv": (1816, 256, 1536),
    "gr": (2088, 256, 1792),
    "gq": (1560, 128, 2048),
    "gk": (1688, 128, 2176),
    "gl": (1280, 24, 2304),
    "ga": (2072, 16, 2432),
}
D_IN_PAD = 2560


def _dot(a, b):
    return jnp.dot(a, b, preferred_element_type=F32)


def _dot_nt(a, b):
    return lax.dot_general(a, b, (((1,), (1,)), ((), ())), preferred_element_type=F32)


def _layer_norm(y, g, b):
    mu = jnp.mean(y, axis=-1, keepdims=True)
    yc = y - mu
    var = jnp.mean(yc * yc, axis=-1, keepdims=True)
    return yc * lax.rsqrt(var + LN_EPS) * g + b


def _split3(x):
    hi = x.astype(BF16)
    r1 = x - hi.astype(F32)
    mid = r1.astype(BF16)
    lo = (r1 - mid.astype(F32)).astype(BF16)
    return hi, mid, lo


FF_CHUNK = 256


N_FF_CHUNKS = D_FF // FF_CHUNK
FF_UNROLL = 5


def _ffn_kernel(x_ref, wg_hbm, wu_hbm, wd_hbm, g_ref, b_ref, o_ref,
                xb_ref, acc_ref, wg_s, wu_s, wd_s, sg_ref, su_ref, sd_ref, sem, *, layer, which):
    def stage_copies(j, slot):
        cols = pl.ds(j * FF_CHUNK, FF_CHUNK)
        return (pltpu.make_async_copy(wg_hbm.at[layer, which, :, cols], sg_ref.at[slot], sem.at[0, slot]),
                pltpu.make_async_copy(wu_hbm.at[layer, which, :, cols], su_ref.at[slot], sem.at[1, slot]),
                pltpu.make_async_copy(wd_hbm.at[layer, which, cols, :], sd_ref.at[slot], sem.at[2, slot]))

    @pl.when(pl.program_id(0) == 0)
    def _():
        for cp in stage_copies(0, 0):
            cp.start()
        for j in range(N_FF_CHUNKS):
            slot = j % 2
            if j + 1 < N_FF_CHUNKS:
                for cp in stage_copies(j + 1, 1 - slot):
                    cp.start()
            for cp in stage_copies(j, slot):
                cp.wait()
            wg_s[j] = sg_ref[slot].astype(BF16)
            wu_s[j] = su_ref[slot].astype(BF16)
            wd_s[j] = sd_ref[slot].astype(BF16)

    xb_ref[...] = x_ref[...].astype(BF16)

    def contribution(j):
        xb = xb_ref[...]
        gate = _dot(xb, wg_s[j])
        up = _dot(xb, wu_s[j])
        act = (gate * jax.nn.sigmoid(gate)) * up
        return _dot(act.astype(BF16), wd_s[j])

    acc_ref[...] = contribution(0)

    def step(j, carry):
        acc_ref[...] += contribution(j)
        return carry

    lax.fori_loop(1, N_FF_CHUNKS, step, 0, unroll=FF_UNROLL)
    y = ALPHA * x_ref[...] + 0.5 * acc_ref[...]
    o_ref[...] = _layer_norm(y, g_ref[...], b_ref[...])


def _ffn_ln(x, wg, wu, wd, g, b, layer, which, *, tm=1024):
    n = x.shape[0]
    return pl.pallas_call(
        functools.partial(_ffn_kernel, layer=layer, which=which),
        grid=(n // tm,),
        in_specs=[
            pl.BlockSpec((tm, D_MODEL), lambda i: (i, 0)),
            pl.BlockSpec(memory_space=pl.ANY),
            pl.BlockSpec(memory_space=pl.ANY),
            pl.BlockSpec(memory_space=pl.ANY),
            pl.BlockSpec((1, D_MODEL), lambda i: (0, 0)),
            pl.BlockSpec((1, D_MODEL), lambda i: (0, 0)),
        ],
        out_specs=pl.BlockSpec((tm, D_MODEL), lambda i: (i, 0)),
        out_shape=jax.ShapeDtypeStruct((n, D_MODEL), F32),
        scratch_shapes=[
            pltpu.VMEM((tm, D_MODEL), BF16),
            pltpu.VMEM((tm, D_MODEL), F32),
            pltpu.VMEM((N_FF_CHUNKS, D_MODEL, FF_CHUNK), BF16),
            pltpu.VMEM((N_FF_CHUNKS, D_MODEL, FF_CHUNK), BF16),
            pltpu.VMEM((N_FF_CHUNKS, FF_CHUNK, D_MODEL), BF16),
            pltpu.VMEM((2, D_MODEL, FF_CHUNK), F32),
            pltpu.VMEM((2, D_MODEL, FF_CHUNK), F32),
            pltpu.VMEM((2, FF_CHUNK, D_MODEL), F32),
            pltpu.SemaphoreType.DMA((3, 2)),
        ],
        compiler_params=pltpu.CompilerParams(
            dimension_semantics=("arbitrary",), vmem_limit_bytes=VMEM_LIMIT),
        name="ffn_ln",
    )(x, wg, wu, wd, g, b)


def _inproj_kernel(x_ref, w_ref, o_ref):
    o_ref[...] = _dot(x_ref[...].astype(BF16), w_ref[...])


def _in_proj(x, w_pad, *, tm=512):
    n = x.shape[0]
    return pl.pallas_call(
        _inproj_kernel,
        grid=(n // tm,),
        in_specs=[
            pl.BlockSpec((tm, D_MODEL), lambda i: (i, 0)),
            pl.BlockSpec((D_MODEL, D_IN_PAD), lambda i: (0, 0)),
        ],
        out_specs=pl.BlockSpec((tm, D_IN_PAD), lambda i: (i, 0)),
        out_shape=jax.ShapeDtypeStruct((n, D_IN_PAD), F32),
        compiler_params=pltpu.CompilerParams(
            dimension_semantics=("parallel",), vmem_limit_bytes=VMEM_LIMIT),
        name="in_proj",
    )(x, w_pad)


N_CMP_PAD = 128


def _gelu_tanh(x):
    return 0.5 * x * (1.0 + jnp.tanh(np.sqrt(2.0 / np.pi) * (x + 0.044715 * (x * x * x))))


def _compress_kernel(zk_ref, zv_ref, pe_ref, w1_ref, w2_ref, ok_ref, ov_ref):
    def one(z_ref, which, o_ref):
        slabs = [z_ref[pl.ds(q, N_CMP_PAD, stride=CMP_STRIDE), :] for q in range(CMP_STRIDE)]
        cat = jnp.concatenate(slabs, axis=1)
        half = CMP_STRIDE * LANES
        top = _dot((cat + pe_ref[which, :, :half]).astype(BF16), w1_ref[which, :half, :])
        bot = _dot((cat + pe_ref[which, :, half:]).astype(BF16), w1_ref[which, half:, :])
        pre = top + pltpu.roll(bot, N_CMP_PAD - 1, 0)
        o_ref[...] = _dot(_gelu_tanh(pre).astype(BF16), w2_ref[which])

    one(zk_ref, 0, ok_ref)
    one(zv_ref, 1, ov_ref)


def _compress(h, pe_rows, w1_bd, w2_bd, batch, seq):
    kc_blk = 512 // LANES
    return pl.pallas_call(
        _compress_kernel,
        grid=(batch,),
        in_specs=[
            pl.BlockSpec((seq, LANES), lambda b: (b, kc_blk)),
            pl.BlockSpec((seq, LANES), lambda b: (b, kc_blk + 1)),
            pl.BlockSpec((2, 1, CMP_BLOCK * LANES), lambda b: (0, 0, 0)),
            pl.BlockSpec((2, CMP_BLOCK * LANES, LANES), lambda b: (0, 0, 0)),
            pl.BlockSpec((2, LANES, LANES), lambda b: (0, 0, 0)),
        ],
        out_specs=[
            pl.BlockSpec((N_CMP_PAD, LANES), lambda b: (b, 0)),
            pl.BlockSpec((N_CMP_PAD, LANES), lambda b: (b, 0)),
        ],
        out_shape=[jax.ShapeDtypeStruct((batch * N_CMP_PAD, LANES), F32)] * 2,
        compiler_params=pltpu.CompilerParams(
            dimension_semantics=("parallel",), vmem_limit_bytes=VMEM_LIMIT),
        name="nsa_compress",
    )(h, h, pe_rows, w1_bd, w2_bd)


N_SLC = 32
WIN_KEYS = WINDOW + Q_BLOCK
KEY_CHUNK = 512
GQ = NSA_GROUP_SIZE * Q_BLOCK
PANEL = 256
SCORE_LOOKAHEAD = 9


VT_ROWS = NSA_HEAD_DIM + 16
LOG2E = 1.4426950408889634


def _softmax_cols(s):
    m = jnp.max(s, axis=0, keepdims=True)
    m = jnp.where(m > 0.5 * NEG, m, 0.0)
    e = jnp.exp2(s - m)
    d = jnp.sum(e, axis=0, keepdims=True)
    inv = 1.0 / jnp.where(d > 0.0, d, 1.0)
    return e, inv


def _tile4(x):
    return jnp.concatenate([x] * NSA_GROUP_SIZE, axis=1)


def _values_t(v_both, g):
    n = v_both.shape[0]
    v_t = v_both.T[g * NSA_HEAD_DIM:(g + 1) * NSA_HEAD_DIM, :]
    extra = (lax.broadcasted_iota(jnp.int32, (VT_ROWS - NSA_HEAD_DIM, n), 0) == 0).astype(F32)
    return jnp.concatenate([v_t, extra], axis=0).astype(BF16)


def _nsa_kernel(q_ref, kc_ref, vc_ref, ksvs_ref, kwvw_ref, gl_ref, o_ref,
                ks_s, vst_s, kw_s, vwt_s, kc_s, vct_s, *, seq):
    qi = pl.program_id(1)
    start = qi * Q_BLOCK
    hd = NSA_HEAD_DIM
    n_chunks = seq // KEY_CHUNK
    n_qb = seq // Q_BLOCK

    @pl.when(qi == 0)
    def _():
        key_blk = lax.broadcasted_iota(jnp.int32, (seq, LANES), 0) // SLC_BLOCK
        blk = lax.broadcasted_iota(jnp.int32, (seq, LANES), 1)
        ks_s[:, :NSA_KV_DIM] = ksvs_ref[:, :NSA_KV_DIM].astype(BF16)
        ks_s[:, NSA_KV_DIM:] = (key_blk == blk).astype(BF16)
        kw_s[...] = kwvw_ref[:, :NSA_KV_DIM].astype(BF16)
        for g in range(NSA_KV_GROUPS):
            for c in range(n_chunks):
                vst_s[c, g] = _values_t(ksvs_ref[c * KEY_CHUNK:(c + 1) * KEY_CHUNK, NSA_KV_DIM:], g)
            for j in range(n_qb):
                vwt_s[j, g] = _values_t(kwvw_ref[j * Q_BLOCK:(j + 1) * Q_BLOCK, NSA_KV_DIM:], g)
        kc_s[...] = kc_ref[...].astype(BF16)
        vct_s[...] = vc_ref[...].T.astype(BF16)

    t_row = start + lax.broadcasted_iota(jnp.int32, (1, Q_BLOCK), 1)

    q_t = (q_ref[...] * (hd ** -0.5 * LOG2E)).T.astype(BF16)
    zeros_half = jnp.zeros((hd, GQ), BF16)
    q_ops = []
    for g in range(NSA_KV_GROUPS):
        top = jnp.concatenate([q_t[(g * NSA_GROUP_SIZE + r) * hd:(g * NSA_GROUP_SIZE + r + 1) * hd, :]
                               for r in range(NSA_GROUP_SIZE)], axis=1)
        q_ops.append(jnp.concatenate([top, zeros_half] if g == 0 else [zeros_half, top], axis=0))

    n_sub = lax.broadcasted_iota(jnp.int32, (N_CMP_PAD, 1), 0)
    bias_c = jnp.where(n_sub * CMP_STRIDE + (CMP_BLOCK - 1) <= t_row, 0.0, NEG)
    m_sub = lax.broadcasted_iota(jnp.int32, (N_SLC, N_CMP_PAD), 0)
    n_lane = lax.broadcasted_iota(jnp.int32, (N_SLC, N_CMP_PAD), 1)
    c0 = n_lane * CMP_STRIDE
    s0 = m_sub * SLC_BLOCK
    ov_t = ((c0 <= s0 + SLC_BLOCK - 1) & (c0 + CMP_BLOCK - 1 >= s0)
            & (n_lane < seq // CMP_STRIDE - 1)).astype(BF16)
    m_idx = lax.broadcasted_iota(jnp.int32, (N_SLC, Q_BLOCK), 0)
    m_idx_f = m_idx.astype(F32)
    cur = t_row // SLC_BLOCK
    forced = (m_idx == 0) | (m_idx == cur) | (m_idx == cur - 1)
    future = m_idx * SLC_BLOCK > t_row

    o_c = []
    sels = []
    for g in range(NSA_KV_GROUPS):
        e, inv = _softmax_cols(_dot(kc_s[...], q_ops[g]) + _tile4(bias_c))
        p_c = e * inv
        o_c.append(_dot(vct_s[g * hd:(g + 1) * hd, :], p_c.astype(BF16)))
        p_sum = (p_c[:, 0:Q_BLOCK] + p_c[:, Q_BLOCK:2 * Q_BLOCK]
                 + p_c[:, 2 * Q_BLOCK:3 * Q_BLOCK] + p_c[:, 3 * Q_BLOCK:])
        hi, mid, lo = _split3(p_sum)
        imp = _dot(ov_t, hi) + _dot(ov_t, mid) + _dot(ov_t, lo)
        imp = jnp.where(forced, -NEG, jnp.where(future, NEG, imp))
        chosen = jnp.zeros((N_SLC, Q_BLOCK), jnp.bool_)
        for _ in range(SLC_TOPK):
            top = jnp.max(imp, axis=0, keepdims=True)
            first = jnp.min(jnp.where(imp == top, m_idx_f, float(N_SLC)), axis=0, keepdims=True)
            hit = m_idx_f == first
            chosen = chosen | hit
            imp = jnp.where(hit, 2.0 * NEG, imp)
        sel_bias = _tile4(jnp.where(chosen & jnp.logical_not(future), 0.0, NEG)).astype(BF16)
        sels.append(jnp.concatenate([q_ops[g], sel_bias, jnp.zeros((LANES - N_SLC, GQ), BF16)], axis=0))

    j0 = jnp.maximum(qi - WINDOW // Q_BLOCK, 0)
    w0 = pl.multiple_of(j0 * Q_BLOCK, Q_BLOCK)
    panels = [(g, hp) for g in range(NSA_KV_GROUPS) for hp in range(GQ // PANEL)]
    init = (jnp.full((1, PANEL), NEG, F32), jnp.zeros((VT_ROWS, PANEL), F32)) * len(panels)

    def online_update(s, m_p, acc, v_t):
        m_n = jnp.maximum(m_p, jnp.max(s, axis=0, keepdims=True))
        return m_n, jnp.exp2(m_p - m_n) * acc + _dot(v_t, jnp.exp2(s - m_n).astype(BF16))

    def normalised(state):
        outs = []
        for g in range(NSA_KV_GROUPS):
            accs = [state[2 * panels.index((g, hp)) + 1] for hp in range(GQ // PANEL)]
            outs.append(jnp.concatenate([a[:hd] * (1.0 / a[hd:hd + 1]) for a in accs], axis=1))
        return outs

    def run_items(items, state):
        state = list(state)
        ahead = [it[1]() for it in items[:SCORE_LOOKAHEAD]]
        for n, (i, _, values) in enumerate(items):
            s = ahead.pop(0)
            if n + SCORE_LOOKAHEAD < len(items):
                ahead.append(items[n + SCORE_LOOKAHEAD][1]())
            state[2 * i], state[2 * i + 1] = online_update(s, state[2 * i], state[2 * i + 1], values())
        return state

    items = []
    for o, n in ((0, 2 * Q_BLOCK), (2 * Q_BLOCK, 2 * Q_BLOCK), (4 * Q_BLOCK, Q_BLOCK)):
        kp_w = w0 + o + lax.broadcasted_iota(jnp.int32, (n, 1), 0)
        bias_w = jnp.where((kp_w <= t_row) & (kp_w > t_row - WINDOW), 0.0, NEG)
        bias_w = jnp.concatenate([bias_w, bias_w], axis=1)
        for i, (g, hp) in enumerate(panels):
            def score(o=o, n=n, g=g, hp=hp, bias_w=bias_w):
                return _dot(kw_s[pl.ds(w0 + o, n), :], q_ops[g][:, hp * PANEL:(hp + 1) * PANEL]) + bias_w

            def values(o=o, n=n, g=g):
                return jnp.concatenate([vwt_s[j0 + o // Q_BLOCK + j, g] for j in range(n // Q_BLOCK)], axis=1)
            items.append((i, score, values))
    window_items = items

    def chunk_items(c, diagonal, first_state):
        items = []
        for sub in range(KEY_CHUNK // PANEL):
            off = pl.multiple_of(c * KEY_CHUNK + sub * PANEL, PANEL)
            causal_bias = None
            if diagonal:
                kpos = off + lax.broadcasted_iota(jnp.int32, (PANEL, 1), 0)
                causal_bias = jnp.where(kpos <= t_row, 0.0, NEG)
                causal_bias = jnp.concatenate([causal_bias, causal_bias], axis=1)
            for i, (g, hp) in enumerate(panels):
                def score(off=off, g=g, hp=hp, causal_bias=causal_bias):
                    s = _dot(ks_s[pl.ds(off, PANEL), :], sels[g][:, hp * PANEL:(hp + 1) * PANEL])
                    return s if causal_bias is None else s + causal_bias

                def values(sub=sub, g=g):
                    return vst_s[c, g, :, sub * PANEL:(sub + 1) * PANEL]
                items.append((first_state + i, score, values))
        return items

    c_diag = qi // (KEY_CHUNK // Q_BLOCK)
    both = run_items(window_items + chunk_items(c_diag, True, len(panels)), init + init)
    o_w = normalised(both[:len(init)])
    fin = lax.fori_loop(0, c_diag, lambda c, carry: tuple(run_items(chunk_items(c, False, 0), carry)),
                        tuple(both[len(init):]))
    o_s = normalised(fin)

    gates_t = jax.nn.sigmoid(gl_ref[...]).T
    outs = []
    for h in range(NSA_HEADS):
        g, r = divmod(h, NSA_GROUP_SIZE)
        lanes = slice(r * Q_BLOCK, (r + 1) * Q_BLOCK)
        outs.append(gates_t[3 * h:3 * h + 1, :] * o_c[g][:, lanes]
                    + gates_t[3 * h + 1:3 * h + 2, :] * o_s[g][:, lanes]
                    + gates_t[3 * h + 2:3 * h + 3, :] * o_w[g][:, lanes])
    o_ref[...] = jnp.concatenate(outs, axis=0).T.astype(o_ref.dtype)


def _nsa(h, kc, vc, batch, seq):
    nqb = seq // Q_BLOCK
    return pl.pallas_call(
        functools.partial(_nsa_kernel, seq=seq),
        grid=(batch, nqb),
        in_specs=[
            pl.BlockSpec((Q_BLOCK, NSA_OUT), lambda b, i: (b * nqb + i, 0)),
            pl.BlockSpec((N_CMP_PAD, LANES), lambda b, i: (b, 0)),
            pl.BlockSpec((N_CMP_PAD, LANES), lambda b, i: (b, 0)),
            pl.BlockSpec((seq, 2 * NSA_KV_DIM), lambda b, i: (b, 768 // 256)),
            pl.BlockSpec((seq, 2 * NSA_KV_DIM), lambda b, i: (b, 1024 // 256)),
            pl.BlockSpec((Q_BLOCK, LANES), lambda b, i: (b * nqb + i, 2304 // LANES)),
        ],
        out_specs=pl.BlockSpec((Q_BLOCK, NSA_OUT), lambda b, i: (b * nqb + i, 0)),
        out_shape=jax.ShapeDtypeStruct((batch * seq, NSA_OUT), BF16),
        scratch_shapes=[
            pltpu.VMEM((seq, NSA_KV_DIM + LANES), BF16),
            pltpu.VMEM((seq // KEY_CHUNK, NSA_KV_GROUPS, VT_ROWS, KEY_CHUNK), BF16),
            pltpu.VMEM((seq, NSA_KV_DIM), BF16),
            pltpu.VMEM((nqb, NSA_KV_GROUPS, VT_ROWS, Q_BLOCK), BF16),
            pltpu.VMEM((N_CMP_PAD, NSA_KV_DIM), BF16),
            pltpu.VMEM((NSA_KV_DIM, N_CMP_PAD), BF16),
        ],
        compiler_params=pltpu.CompilerParams(
            dimension_semantics=("parallel", "arbitrary"), vmem_limit_bytes=VMEM_LIMIT),
        name="nsa_attn",
    )(h, kc, vc, h, h, h)


def _shift_rows(x, k, row):
    return jnp.where(row >= k, pltpu.roll(x, k, 0), 0.0)


def _pool_kernel(u_ref, w_ref, sc_ref, o_ref, *, seq):
    u = u_ref[...]
    row = lax.broadcasted_iota(jnp.int32, (seq, 1), 0)
    lane = lax.broadcasted_iota(jnp.int32, (1, POOL_DIM), 1)
    tp1 = (row + 1).astype(F32)
    acc = u
    mean = jnp.zeros_like(u)
    span = 1
    for gi, w in enumerate(POOL_WINDOWS):
        while span < w:
            acc = acc + _shift_rows(acc, span, row)
            span *= 2
        cnt = jnp.minimum(float(w), tp1)
        in_group = (lane >= gi * POOL_GROUP_DIM) & (lane < (gi + 1) * POOL_GROUP_DIM)
        mean = jnp.where(in_group, acc / cnt, mean)
    pooled = mean - u
    o_ref[...] = (_dot(pooled.astype(BF16), w_ref[...]) * sc_ref[...]).astype(o_ref.dtype)


def _pool(h, w_bd, scale, batch, seq):
    return pl.pallas_call(
        functools.partial(_pool_kernel, seq=seq),
        grid=(batch,),
        in_specs=[
            pl.BlockSpec((seq, POOL_DIM), lambda b: (b, 1280 // 256)),
            pl.BlockSpec((POOL_DIM, POOL_DIM), lambda b: (0, 0)),
            pl.BlockSpec((1, POOL_DIM), lambda b: (0, 0)),
        ],
        out_specs=pl.BlockSpec((seq, POOL_DIM), lambda b: (b, 0)),
        out_shape=jax.ShapeDtypeStruct((batch * seq, POOL_DIM), BF16),
        compiler_params=pltpu.CompilerParams(
            dimension_semantics=("parallel",), vmem_limit_bytes=VMEM_LIMIT),
        name="pool_mix",
    )(h, w_bd, scale)


def _gla_kernel(q_ref, k_ref, v_ref, a_ref, r_ref, wa2_ref, ba_ref, ng_ref, o_ref,
                qt_ref, kt_ref, kd_ref, dec_ref, oacc_ref, *, seq):
    C, H, DK, DV = GLA_CHUNK, GLA_HEADS, GLA_KEY_DIM, GLA_VAL_DIM
    nc = seq // C
    z = _dot(a_ref[...].astype(BF16), wa2_ref[...]) + ba_ref[...]
    log_a = (jnp.minimum(z, 0.0) - jnp.log(1.0 + jnp.exp(-jnp.abs(z)))) / GLA_TAU
    pos = lax.broadcasted_iota(jnp.int32, (seq, 1), 0) % C
    b = log_a
    step = 1
    while step < C:
        b = b + jnp.where(pos >= step, pltpu.roll(b, step, 0), 0.0)
        step *= 2
    b3 = b.reshape(nc, C, GLA_QK)
    b_last = b3[:, C - 1:C, :]
    qt_ref[...] = q_ref[...] * (DK ** -0.5) * jnp.exp(b)
    kt_ref[...] = k_ref[...] * jnp.exp(-b)
    kd_ref[...] = (k_ref[...].reshape(nc, C, GLA_QK) * jnp.exp(b_last - b3)).reshape(seq, GLA_QK)
    dec_ref[...] = jnp.exp(b_last)

    r_k = lax.broadcasted_iota(jnp.int32, (H * C, GLA_QK), 0) // C
    c_k = lax.broadcasted_iota(jnp.int32, (H * C, GLA_QK), 1) // DK
    mask_k = r_k == c_k
    r_v = lax.broadcasted_iota(jnp.int32, (H * C, GLA_OUT), 0) // C
    c_v = lax.broadcasted_iota(jnp.int32, (H * C, GLA_OUT), 1) // DV
    mask_v = r_v == c_v
    r_s = lax.broadcasted_iota(jnp.int32, (GLA_OUT, GLA_QK), 0) // DV
    c_s = lax.broadcasted_iota(jnp.int32, (GLA_OUT, GLA_QK), 1) // DK
    mask_s = r_s == c_s
    i_a = lax.broadcasted_iota(jnp.int32, (C, H * C), 0)
    j_a = lax.broadcasted_iota(jnp.int32, (C, H * C), 1) % C
    tril = j_a <= i_a

    def chunk(n, state_t):
        rows = pl.ds(pl.multiple_of(n * C, C), C)
        q_t = qt_ref[rows, :].astype(BF16)
        k_t = kt_ref[rows, :]
        k_d = kd_ref[rows, :].astype(BF16)
        v_c = v_ref[rows, :]
        k_bd = jnp.where(mask_k, jnp.concatenate([k_t] * H, axis=0), 0.0).astype(BF16)
        a_cat = jnp.where(tril, _dot_nt(q_t, k_bd), 0.0)
        v_bd = jnp.where(mask_v, jnp.concatenate([v_c] * H, axis=0), 0.0).astype(BF16)
        o_intra = _dot(a_cat.astype(BF16), v_bd)
        o_inter = _dot_nt(q_t, state_t.astype(BF16))
        oacc_ref[rows, :] = o_intra + o_inter
        d_state = jnp.where(mask_s, _dot(v_c.T.astype(BF16), k_d), 0.0)
        return state_t * dec_ref[n] + d_state

    lax.fori_loop(0, nc, chunk, jnp.zeros((GLA_OUT, GLA_QK), F32), unroll=8)

    o = oacc_ref[...]
    gr = lax.broadcasted_iota(jnp.int32, (GLA_OUT, GLA_OUT), 0) // DV
    gc = lax.broadcasted_iota(jnp.int32, (GLA_OUT, GLA_OUT), 1) // DV
    group_mean = jnp.where(gr == gc, 1.0 / DV, 0.0).astype(BF16)
    hi, mid, lo = _split3(o * o)
    ms = _dot(hi, group_mean) + _dot(mid, group_mean) + _dot(lo, group_mean)
    o = o * lax.rsqrt(ms + RMS_EPS) * ng_ref[...]
    r = r_ref[...]
    o_ref[...] = (o * (r * jax.nn.sigmoid(r))).astype(o_ref.dtype)


def _gla(h, wa2_pad, ba, norm_g, batch, seq):
    nc = seq // GLA_CHUNK
    return pl.pallas_call(
        functools.partial(_gla_kernel, seq=seq),
        grid=(batch,),
        in_specs=[
            pl.BlockSpec((seq, GLA_QK), lambda b: (b, 2048 // LANES)),
            pl.BlockSpec((seq, GLA_QK), lambda b: (b, 2176 // LANES)),
            pl.BlockSpec((seq, GLA_OUT), lambda b: (b, 1536 // 256)),
            pl.BlockSpec((seq, LANES), lambda b: (b, 2432 // LANES)),
            pl.BlockSpec((seq, GLA_OUT), lambda b: (b, 1792 // 256)),
            pl.BlockSpec((LANES, GLA_QK), lambda b: (0, 0)),
            pl.BlockSpec((1, GLA_QK), lambda b: (0, 0)),
            pl.BlockSpec((1, GLA_OUT), lambda b: (0, 0)),
        ],
        out_specs=pl.BlockSpec((seq, GLA_OUT), lambda b: (b, 0)),
        out_shape=jax.ShapeDtypeStruct((batch * seq, GLA_OUT), BF16),
        scratch_shapes=[
            pltpu.VMEM((seq, GLA_QK), F32),
            pltpu.VMEM((seq, GLA_QK), F32),
            pltpu.VMEM((seq, GLA_QK), F32),
            pltpu.VMEM((nc, 1, GLA_QK), F32),
            pltpu.VMEM((seq, GLA_OUT), F32),
        ],
        compiler_params=pltpu.CompilerParams(
            dimension_semantics=("parallel",), vmem_limit_bytes=VMEM_LIMIT),
        name="gla_mix",
    )(h, h, h, h, h, wa2_pad, ba, norm_g)


def _outproj_kernel(x_ref, oa_ref, ob_ref, oc_ref, wa_ref, wb_ref, wc_ref, g_ref, b_ref, o_ref):
    m = _dot(oa_ref[...], wa_ref[...]) + _dot(ob_ref[...], wb_ref[...]) + _dot(oc_ref[...], wc_ref[...])
    o_ref[...] = _layer_norm(ALPHA * x_ref[...] + m, g_ref[...], b_ref[...])


def _out_proj_ln(x, o_a, o_b, o_c, w_a, w_b, w_c, g, b, *, tm=512):
    n = x.shape[0]
    row = lambda i: (i, 0)
    const = lambda i: (0, 0)
    return pl.pallas_call(
        _outproj_kernel,
        grid=(n // tm,),
        in_specs=[
            pl.BlockSpec((tm, D_MODEL), row),
            pl.BlockSpec((tm, NSA_OUT), row),
            pl.BlockSpec((tm, POOL_DIM), row),
            pl.BlockSpec((tm, GLA_OUT), row),
            pl.BlockSpec((NSA_OUT, D_MODEL), const),
            pl.BlockSpec((POOL_DIM, D_MODEL), const),
            pl.BlockSpec((GLA_OUT, D_MODEL), const),
            pl.BlockSpec((1, D_MODEL), const),
            pl.BlockSpec((1, D_MODEL), const),
        ],
        out_specs=pl.BlockSpec((tm, D_MODEL), row),
        out_shape=jax.ShapeDtypeStruct((n, D_MODEL), F32),
        compiler_params=pltpu.CompilerParams(
            dimension_semantics=("parallel",), vmem_limit_bytes=VMEM_LIMIT),
        name="out_proj_ln",
    )(x, o_a, o_b, o_c, w_a, w_b, w_c, g, b)


def _pad_cols(w, width):
    return jnp.pad(w, ((0, 0), (0, width - w.shape[1])))


def _prep_w_in(w_in):
    parts = [w_in[:, 0:1280]]
    for name in ("u", "gv", "gr", "gq", "gk"):
        s, wd, _ = _IN_LAYOUT[name]
        parts.append(w_in[:, s:s + wd])
    for name in ("gl", "ga"):
        s, wd, _ = _IN_LAYOUT[name]
        parts.append(_pad_cols(w_in[:, s:s + wd], LANES))
    return jnp.concatenate(parts, axis=1).astype(BF16)


def _block_diag(blocks):
    n, r, c = blocks.shape
    eye = jnp.eye(n, dtype=blocks.dtype)
    return jnp.einsum("grc,gh->grhc", blocks, eye).reshape(n * r, n * c)


def _prep_compress(cmp_pe, cmp_w1, cmp_w2):
    G, HD = NSA_KV_GROUPS, NSA_HEAD_DIM
    pe = jnp.tile(cmp_pe[:, :, None, :], (1, 1, G, 1)).reshape(2, 1, CMP_BLOCK * G * HD)
    w1 = cmp_w1.reshape(2, CMP_BLOCK, HD, HD)
    eye = jnp.eye(G, dtype=cmp_w1.dtype)
    w1_bd = jnp.einsum("kpde,gh->kpgdhe", w1, eye).reshape(2, CMP_BLOCK * G * HD, G * HD)
    w2_bd = jnp.einsum("kde,gh->kgdhe", cmp_w2, eye).reshape(2, G * HD, G * HD)
    return pe, w1_bd.astype(BF16), w2_bd.astype(BF16)


def kernel(x, ln_g, ln_b, ffn_wg, ffn_wu, ffn_wd, w_in, w_out, cmp_pe, cmp_w1, cmp_w2,
           pool_w, pool_scale, gla_wa2, gla_ba, gla_norm_g):
    batch, seq, _ = x.shape
    xf = x.reshape(batch * seq, D_MODEL)
    for l in range(DEPTH):
        lg = lambda i: ln_g[l, i].reshape(1, D_MODEL)
        lb = lambda i: ln_b[l, i].reshape(1, D_MODEL)
        xf = _ffn_ln(xf, ffn_wg, ffn_wu, ffn_wd, lg(0), lb(0), l, 0)
        h = _in_proj(xf, _prep_w_in(w_in[l]))
        pe_rows, w1_bd, w2_bd = _prep_compress(cmp_pe[l], cmp_w1[l], cmp_w2[l])
        kc, vc = _compress(h, pe_rows, w1_bd, w2_bd, batch, seq)
        o_a = _nsa(h, kc, vc, batch, seq)
        o_b = _pool(h, _block_diag(pool_w[l]).astype(BF16), pool_scale[l].reshape(1, POOL_DIM), batch, seq)
        wa2_pad = jnp.pad(gla_wa2[l], ((0, LANES - GLA_GATE_RANK), (0, 0))).astype(BF16)
        o_c = _gla(h, wa2_pad, gla_ba[l].reshape(1, GLA_QK), gla_norm_g[l].reshape(1, GLA_OUT), batch, seq)
        wo = w_out[l].astype(BF16)
        xf = _out_proj_ln(xf, o_a, o_b, o_c, wo[:NSA_OUT], wo[NSA_OUT:NSA_OUT + POOL_DIM],
                          wo[NSA_OUT + POOL_DIM:], lg(1), lb(1))
        xf = _ffn_ln(xf, ffn_wg, ffn_wu, ffn_wd, lg(2), lb(2), l, 1)
    return xf.reshape(batch, seq, D_MODEL)
```

```python
import functools

import numpy as np
import jax
import jax.numpy as jnp
from jax import lax
from jax.experimental import pallas as pl
from jax.experimental.pallas import tpu as pltpu

F32 = jnp.float32
BF16 = jnp.bfloat16

D_MODEL = 1024
DEPTH = 4
D_FF = 2816
NSA_HEADS = 8
NSA_KV_GROUPS = 2
NSA_HEAD_DIM = 64
NSA_GROUP_SIZE = NSA_HEADS // NSA_KV_GROUPS
NSA_OUT = NSA_HEADS * NSA_HEAD_DIM
NSA_KV_DIM = NSA_KV_GROUPS * NSA_HEAD_DIM
CMP_BLOCK = 32
CMP_STRIDE = 16
SLC_BLOCK = 64
SLC_TOPK = 8
WINDOW = 512
Q_BLOCK = 128
POOL_WINDOWS = (2, 4, 8, 16)
POOL_GROUPS = 4
POOL_GROUP_DIM = 64
POOL_DIM = POOL_GROUPS * POOL_GROUP_DIM
GLA_HEADS = 4
GLA_KEY_DIM = 32
GLA_VAL_DIM = 64
GLA_GATE_RANK = 16
GLA_TAU = 16.0
GLA_CHUNK = 64
GLA_QK = GLA_HEADS * GLA_KEY_DIM
GLA_OUT = GLA_HEADS * GLA_VAL_DIM
ALPHA = (2.0 * DEPTH) ** 0.25
LN_EPS = 1e-5
RMS_EPS = 1e-6

LANES = 128
NEG = -1e30
VMEM_LIMIT = 56 * 1024 * 1024

_IN_LAYOUT = {
    "qkv": (0, 1280, 0),
    "u": (1304, 256, 1280),
    "gv": (1816, 256, 1536),
    "gr": (2088, 256, 1792),
    "gq": (1560, 128, 2048),
    "gk": (1688, 128, 2176),
    "gl": (1280, 24, 2304),
    "ga": (2072, 16, 2432),
}
D_IN_PAD = 2560


def _dot(a, b):
    return jnp.dot(a, b, preferred_element_type=F32)


def _dot_nt(a, b):
    return lax.dot_general(a, b, (((1,), (1,)), ((), ())), preferred_element_type=F32)


def _layer_norm(y, g, b):
    mu = jnp.mean(y, axis=-1, keepdims=True)
    yc = y - mu
    var = jnp.mean(yc * yc, axis=-1, keepdims=True)
    return yc * lax.rsqrt(var + LN_EPS) * g + b


def _split3(x):
    hi = x.astype(BF16)
    r1 = x - hi.astype(F32)
    mid = r1.astype(BF16)
    lo = (r1 - mid.astype(F32)).astype(BF16)
    return hi, mid, lo


FF_CHUNK = 256


N_FF_CHUNKS = D_FF // FF_CHUNK
FF_UNROLL = 5


def _ffn_kernel(x_ref, wg_hbm, wu_hbm, wd_hbm, g_ref, b_ref, o_ref,
                xb_ref, acc_ref, wg_s, wu_s, wd_s, sg_ref, su_ref, sd_ref, sem, *, layer, which):
    def stage_copies(j, slot):
        cols = pl.ds(j * FF_CHUNK, FF_CHUNK)
        return (pltpu.make_async_copy(wg_hbm.at[layer, which, :, cols], sg_ref.at[slot], sem.at[0, slot]),
                pltpu.make_async_copy(wu_hbm.at[layer, which, :, cols], su_ref.at[slot], sem.at[1, slot]),
                pltpu.make_async_copy(wd_hbm.at[layer, which, cols, :], sd_ref.at[slot], sem.at[2, slot]))

    @pl.when(pl.program_id(0) == 0)
    def _():
        for cp in stage_copies(0, 0):
            cp.start()
        for j in range(N_FF_CHUNKS):
            slot = j % 2
            if j + 1 < N_FF_CHUNKS:
                for cp in stage_copies(j + 1, 1 - slot):
                    cp.start()
            for cp in stage_copies(j, slot):
                cp.wait()
            wg_s[j] = sg_ref[slot].astype(BF16)
            wu_s[j] = su_ref[slot].astype(BF16)
            wd_s[j] = sd_ref[slot].astype(BF16)

    xb_ref[...] = x_ref[...].astype(BF16)

    def contribution(j):
        xb = xb_ref[...]
        gate = _dot(xb, wg_s[j])
        up = _dot(xb, wu_s[j])
        act = (gate * jax.nn.sigmoid(gate)) * up
        return _dot(act.astype(BF16), wd_s[j])

    acc_ref[...] = contribution(0)

    def step(j, carry):
        acc_ref[...] += contribution(j)
        return carry

    lax.fori_loop(1, N_FF_CHUNKS, step, 0, unroll=FF_UNROLL)
    y = ALPHA * x_ref[...] + 0.5 * acc_ref[...]
    o_ref[...] = _layer_norm(y, g_ref[...], b_ref[...])


def _ffn_ln(x, wg, wu, wd, g, b, layer, which, *, tm=1024):
    n = x.shape[0]
    return pl.pallas_call(
        functools.partial(_ffn_kernel, layer=layer, which=which),
        grid=(n // tm,),
        in_specs=[
            pl.BlockSpec((tm, D_MODEL), lambda i: (i, 0)),
            pl.BlockSpec(memory_space=pl.ANY),
            pl.BlockSpec(memory_space=pl.ANY),
            pl.BlockSpec(memory_space=pl.ANY),
            pl.BlockSpec((1, D_MODEL), lambda i: (0, 0)),
            pl.BlockSpec((1, D_MODEL), lambda i: (0, 0)),
        ],
        out_specs=pl.BlockSpec((tm, D_MODEL), lambda i: (i, 0)),
        out_shape=jax.ShapeDtypeStruct((n, D_MODEL), F32),
        scratch_shapes=[
            pltpu.VMEM((tm, D_MODEL), BF16),
            pltpu.VMEM((tm, D_MODEL), F32),
            pltpu.VMEM((N_FF_CHUNKS, D_MODEL, FF_CHUNK), BF16),
            pltpu.VMEM((N_FF_CHUNKS, D_MODEL, FF_CHUNK), BF16),
            pltpu.VMEM((N_FF_CHUNKS, FF_CHUNK, D_MODEL), BF16),
            pltpu.VMEM((2, D_MODEL, FF_CHUNK), F32),
            pltpu.VMEM((2, D_MODEL, FF_CHUNK), F32),
            pltpu.VMEM((2, FF_CHUNK, D_MODEL), F32),
            pltpu.SemaphoreType.DMA((3, 2)),
        ],
        compiler_params=pltpu.CompilerParams(
            dimension_semantics=("arbitrary",), vmem_limit_bytes=VMEM_LIMIT),
        name="ffn_ln",
    )(x, wg, wu, wd, g, b)


def _inproj_kernel(x_ref, w_ref, o_ref):
    o_ref[...] = _dot(x_ref[...].astype(BF16), w_ref[...])


def _in_proj(x, w_pad, *, tm=512):
    n = x.shape[0]
    return pl.pallas_call(
        _inproj_kernel,
        grid=(n // tm,),
        in_specs=[
            pl.BlockSpec((tm, D_MODEL), lambda i: (i, 0)),
            pl.BlockSpec((D_MODEL, D_IN_PAD), lambda i: (0, 0)),
        ],
        out_specs=pl.BlockSpec((tm, D_IN_PAD), lambda i: (i, 0)),
        out_shape=jax.ShapeDtypeStruct((n, D_IN_PAD), F32),
        compiler_params=pltpu.CompilerParams(
            dimension_semantics=("parallel",), vmem_limit_bytes=VMEM_LIMIT),
        name="in_proj",
    )(x, w_pad)


N_CMP_PAD = 128


def _gelu_tanh(x):
    return 0.5 * x * (1.0 + jnp.tanh(np.sqrt(2.0 / np.pi) * (x + 0.044715 * (x * x * x))))


def _compress_kernel(zk_ref, zv_ref, pe_ref, w1_ref, w2_ref, ok_ref, ov_ref):
    def one(z_ref, which, o_ref):
        slabs = [z_ref[pl.ds(q, N_CMP_PAD, stride=CMP_STRIDE), :] for q in range(CMP_STRIDE)]
        cat = jnp.concatenate(slabs, axis=1)
        half = CMP_STRIDE * LANES
        top = _dot((cat + pe_ref[which, :, :half]).astype(BF16), w1_ref[which, :half, :])
        bot = _dot((cat + pe_ref[which, :, half:]).astype(BF16), w1_ref[which, half:, :])
        pre = top + pltpu.roll(bot, N_CMP_PAD - 1, 0)
        o_ref[...] = _dot(_gelu_tanh(pre).astype(BF16), w2_ref[which])

    one(zk_ref, 0, ok_ref)
    one(zv_ref, 1, ov_ref)


def _compress(h, pe_rows, w1_bd, w2_bd, batch, seq):
    kc_blk = 512 // LANES
    return pl.pallas_call(
        _compress_kernel,
        grid=(batch,),
        in_specs=[
            pl.BlockSpec((seq, LANES), lambda b: (b, kc_blk)),
            pl.BlockSpec((seq, LANES), lambda b: (b, kc_blk + 1)),
            pl.BlockSpec((2, 1, CMP_BLOCK * LANES), lambda b: (0, 0, 0)),
            pl.BlockSpec((2, CMP_BLOCK * LANES, LANES), lambda b: (0, 0, 0)),
            pl.BlockSpec((2, LANES, LANES), lambda b: (0, 0, 0)),
        ],
        out_specs=[
            pl.BlockSpec((N_CMP_PAD, LANES), lambda b: (b, 0)),
            pl.BlockSpec((N_CMP_PAD, LANES), lambda b: (b, 0)),
        ],
        out_shape=[jax.ShapeDtypeStruct((batch * N_CMP_PAD, LANES), F32)] * 2,
        compiler_params=pltpu.CompilerParams(
            dimension_semantics=("parallel",), vmem_limit_bytes=VMEM_LIMIT),
        name="nsa_compress",
    )(h, h, pe_rows, w1_bd, w2_bd)


N_SLC = 32
WIN_KEYS = WINDOW + Q_BLOCK
KEY_CHUNK = 512
GQ = NSA_GROUP_SIZE * Q_BLOCK
PANEL = 256
SCORE_LOOKAHEAD = 9


VT_ROWS = NSA_HEAD_DIM + 16
LOG2E = 1.4426950408889634


def _softmax_cols(s):
    m = jnp.max(s, axis=0, keepdims=True)
    m = jnp.where(m > 0.5 * NEG, m, 0.0)
    e = jnp.exp2(s - m)
    d = jnp.sum(e, axis=0, keepdims=True)
    inv = 1.0 / jnp.where(d > 0.0, d, 1.0)
    return e, inv


def _tile4(x):
    return jnp.concatenate([x] * NSA_GROUP_SIZE, axis=1)


def _values_t(v_both, g):
    n = v_both.shape[0]
    v_t = v_both.T[g * NSA_HEAD_DIM:(g + 1) * NSA_HEAD_DIM, :]
    extra = (lax.broadcasted_iota(jnp.int32, (VT_ROWS - NSA_HEAD_DIM, n), 0) == 0).astype(F32)
    return jnp.concatenate([v_t, extra], axis=0).astype(BF16)


def _nsa_kernel(q_ref, kc_ref, vc_ref, ksvs_ref, kwvw_ref, gl_ref, o_ref,
                ks_s, vst_s, kw_s, vwt_s, kc_s, vct_s, *, seq):
    qi = pl.program_id(1)
    start = qi * Q_BLOCK
    hd = NSA_HEAD_DIM
    n_chunks = seq // KEY_CHUNK
    n_qb = seq // Q_BLOCK

    @pl.when(qi == 0)
    def _():
        key_blk = lax.broadcasted_iota(jnp.int32, (seq, LANES), 0) // SLC_BLOCK
        blk = lax.broadcasted_iota(jnp.int32, (seq, LANES), 1)
        ks_s[:, :NSA_KV_DIM] = ksvs_ref[:, :NSA_KV_DIM].astype(BF16)
        ks_s[:, NSA_KV_DIM:] = (key_blk == blk).astype(BF16)
        kw_s[...] = kwvw_ref[:, :NSA_KV_DIM].astype(BF16)
        for g in range(NSA_KV_GROUPS):
            for c in range(n_chunks):
                vst_s[c, g] = _values_t(ksvs_ref[c * KEY_CHUNK:(c + 1) * KEY_CHUNK, NSA_KV_DIM:], g)
            for j in range(n_qb):
                vwt_s[j, g] = _values_t(kwvw_ref[j * Q_BLOCK:(j + 1) * Q_BLOCK, NSA_KV_DIM:], g)
        kc_s[...] = kc_ref[...].astype(BF16)
        vct_s[...] = vc_ref[...].T.astype(BF16)

    t_row = start + lax.broadcasted_iota(jnp.int32, (1, Q_BLOCK), 1)

    q_t = (q_ref[...] * (hd ** -0.5 * LOG2E)).T.astype(BF16)
    zeros_half = jnp.zeros((hd, GQ), BF16)
    q_ops = []
    for g in range(NSA_KV_GROUPS):
        top = jnp.concatenate([q_t[(g * NSA_GROUP_SIZE + r) * hd:(g * NSA_GROUP_SIZE + r + 1) * hd, :]
                               for r in range(NSA_GROUP_SIZE)], axis=1)
        q_ops.append(jnp.concatenate([top, zeros_half] if g == 0 else [zeros_half, top], axis=0))

    n_sub = lax.broadcasted_iota(jnp.int32, (N_CMP_PAD, 1), 0)
    bias_c = jnp.where(n_sub * CMP_STRIDE + (CMP_BLOCK - 1) <= t_row, 0.0, NEG)
    m_sub = lax.broadcasted_iota(jnp.int32, (N_SLC, N_CMP_PAD), 0)
    n_lane = lax.broadcasted_iota(jnp.int32, (N_SLC, N_CMP_PAD), 1)
    c0 = n_lane * CMP_STRIDE
    s0 = m_sub * SLC_BLOCK
    ov_t = ((c0 <= s0 + SLC_BLOCK - 1) & (c0 + CMP_BLOCK - 1 >= s0)
            & (n_lane < seq // CMP_STRIDE - 1)).astype(BF16)
    m_idx = lax.broadcasted_iota(jnp.int32, (N_SLC, Q_BLOCK), 0)
    m_idx_f = m_idx.astype(F32)
    cur = t_row // SLC_BLOCK
    forced = (m_idx == 0) | (m_idx == cur) | (m_idx == cur - 1)
    future = m_idx * SLC_BLOCK > t_row

    def compressed_scores(g):
        return _dot(kc_s[...], q_ops[g]) + _tile4(bias_c)

    def compressed_branch(g, s_c):
        e, inv = _softmax_cols(s_c)
        p_c = e * inv
        o_cg = _dot(vct_s[g * hd:(g + 1) * hd, :], p_c.astype(BF16))
        p_sum = (p_c[:, 0:Q_BLOCK] + p_c[:, Q_BLOCK:2 * Q_BLOCK]
                 + p_c[:, 2 * Q_BLOCK:3 * Q_BLOCK] + p_c[:, 3 * Q_BLOCK:])
        hi, mid, lo = _split3(p_sum)
        imp = _dot(ov_t, hi) + _dot(ov_t, mid) + _dot(ov_t, lo)
        imp = jnp.where(forced, -NEG, jnp.where(future, NEG, imp))
        chosen = jnp.zeros((N_SLC, Q_BLOCK), jnp.bool_)
        for _ in range(SLC_TOPK):
            top = jnp.max(imp, axis=0, keepdims=True)
            first = jnp.min(jnp.where(imp == top, m_idx_f, float(N_SLC)), axis=0, keepdims=True)
            hit = m_idx_f == first
            chosen = chosen | hit
            imp = jnp.where(hit, 2.0 * NEG, imp)
        sel_bias = _tile4(jnp.where(chosen & jnp.logical_not(future), 0.0, NEG)).astype(BF16)
        return o_cg, jnp.concatenate([q_ops[g], sel_bias, jnp.zeros((LANES - N_SLC, GQ), BF16)], axis=0)

    j0 = jnp.maximum(qi - WINDOW // Q_BLOCK, 0)
    w0 = pl.multiple_of(j0 * Q_BLOCK, Q_BLOCK)
    panels = [(g, hp) for g in range(NSA_KV_GROUPS) for hp in range(GQ // PANEL)]
    init = (jnp.full((1, PANEL), NEG, F32), jnp.zeros((VT_ROWS, PANEL), F32)) * len(panels)


    def normalised(state):
        outs = []
        for g in range(NSA_KV_GROUPS):
            accs = [state[2 * panels.index((g, hp)) + 1] for hp in range(GQ // PANEL)]
            outs.append(jnp.concatenate([a[:hd] * (1.0 / a[hd:hd + 1]) for a in accs], axis=1))
        return outs

    def run_items(items, state, hooks=None):
        state = list(state)
        ahead = [it[1]() for it in items[:SCORE_LOOKAHEAD]]
        pending = None

        def flush(pending):
            i, alpha, p, values = pending
            state[2 * i + 1] = alpha * state[2 * i + 1] + _dot(values(), p)

        for n, (i, _, values) in enumerate(items):
            s = ahead.pop(0)
            if n + SCORE_LOOKAHEAD < len(items):
                ahead.append(items[n + SCORE_LOOKAHEAD][1]())
            m_p = state[2 * i]
            m_n = jnp.maximum(m_p, jnp.max(s, axis=0, keepdims=True))
            state[2 * i] = m_n
            p = jnp.exp2(s - m_n).astype(BF16)
            if pending is not None:
                flush(pending)
            pending = (i, jnp.exp2(m_p - m_n), p, values)
            if hooks and n in hooks:
                hooks[n]()
        flush(pending)
        return state

    items = []
    for o, n in ((0, 2 * Q_BLOCK), (2 * Q_BLOCK, 2 * Q_BLOCK), (4 * Q_BLOCK, Q_BLOCK)):
        kp_w = w0 + o + lax.broadcasted_iota(jnp.int32, (n, 1), 0)
        bias_w = jnp.where((kp_w <= t_row) & (kp_w > t_row - WINDOW), 0.0, NEG)
        bias_w = jnp.concatenate([bias_w, bias_w], axis=1)
        for i, (g, hp) in enumerate(panels):
            def score(o=o, n=n, g=g, hp=hp, bias_w=bias_w):
                return _dot(kw_s[pl.ds(w0 + o, n), :], q_ops[g][:, hp * PANEL:(hp + 1) * PANEL]) + bias_w

            def values(o=o, n=n, g=g):
                return jnp.concatenate([vwt_s[j0 + o // Q_BLOCK + j, g] for j in range(n // Q_BLOCK)], axis=1)
            items.append((i, score, values))
    window_items = items

    def chunk_items(c, diagonal, sels):
        items = []
        for sub in range(KEY_CHUNK // PANEL):
            off = pl.multiple_of(c * KEY_CHUNK + sub * PANEL, PANEL)
            causal_bias = None
            if diagonal:
                kpos = off + lax.broadcasted_iota(jnp.int32, (PANEL, 1), 0)
                causal_bias = jnp.where(kpos <= t_row, 0.0, NEG)
                causal_bias = jnp.concatenate([causal_bias, causal_bias], axis=1)
            for i, (g, hp) in enumerate(panels):
                def score(off=off, g=g, hp=hp, causal_bias=causal_bias):
                    s = _dot(ks_s[pl.ds(off, PANEL), :], sels[g][:, hp * PANEL:(hp + 1) * PANEL])
                    return s if causal_bias is None else s + causal_bias

                def values(sub=sub, g=g):
                    return vst_s[c, g, :, sub * PANEL:(sub + 1) * PANEL]
                items.append((len(panels) + i, score, values))
        return items

    s_cs = [compressed_scores(g) for g in range(NSA_KV_GROUPS)]
    cmp_out = {}

    def hook(g):
        return lambda: cmp_out.__setitem__(g, compressed_branch(g, s_cs[g]))

    o_w = normalised(run_items(window_items, init, hooks={1 + 4 * g: hook(g) for g in range(NSA_KV_GROUPS)}))
    o_c, sels = zip(*[cmp_out[g] for g in range(NSA_KV_GROUPS)])

    def sequence(n_past):
        def run():
            past = [it for c in range(n_past) for it in chunk_items(c, False, sels)]
            return tuple(run_items(chunk_items(n_past, True, sels) + past, init + init)[len(init):])
        return run

    o_s = normalised(lax.switch(qi // (KEY_CHUNK // Q_BLOCK), [sequence(k) for k in range(n_chunks)]))

    gates_t = jax.nn.sigmoid(gl_ref[...]).T
    outs = []
    for h in range(NSA_HEADS):
        g, r = divmod(h, NSA_GROUP_SIZE)
        lanes = slice(r * Q_BLOCK, (r + 1) * Q_BLOCK)
        outs.append(gates_t[3 * h:3 * h + 1, :] * o_c[g][:, lanes]
                    + gates_t[3 * h + 1:3 * h + 2, :] * o_s[g][:, lanes]
                    + gates_t[3 * h + 2:3 * h + 3, :] * o_w[g][:, lanes])
    o_ref[...] = jnp.concatenate(outs, axis=0).T.astype(o_ref.dtype)


def _nsa(h, kc, vc, batch, seq):
    nqb = seq // Q_BLOCK
    return pl.pallas_call(
        functools.partial(_nsa_kernel, seq=seq),
        grid=(batch, nqb),
        in_specs=[
            pl.BlockSpec((Q_BLOCK, NSA_OUT), lambda b, i: (b * nqb + i, 0)),
            pl.BlockSpec((N_CMP_PAD, LANES), lambda b, i: (b, 0)),
            pl.BlockSpec((N_CMP_PAD, LANES), lambda b, i: (b, 0)),
            pl.BlockSpec((seq, 2 * NSA_KV_DIM), lambda b, i: (b, 768 // 256)),
            pl.BlockSpec((seq, 2 * NSA_KV_DIM), lambda b, i: (b, 1024 // 256)),
            pl.BlockSpec((Q_BLOCK, LANES), lambda b, i: (b * nqb + i, 2304 // LANES)),
        ],
        out_specs=pl.BlockSpec((Q_BLOCK, NSA_OUT), lambda b, i: (b * nqb + i, 0)),
        out_shape=jax.ShapeDtypeStruct((batch * seq, NSA_OUT), BF16),
        scratch_shapes=[
            pltpu.VMEM((seq, NSA_KV_DIM + LANES), BF16),
            pltpu.VMEM((seq // KEY_CHUNK, NSA_KV_GROUPS, VT_ROWS, KEY_CHUNK), BF16),
            pltpu.VMEM((seq, NSA_KV_DIM), BF16),
            pltpu.VMEM((nqb, NSA_KV_GROUPS, VT_ROWS, Q_BLOCK), BF16),
            pltpu.VMEM((N_CMP_PAD, NSA_KV_DIM), BF16),
            pltpu.VMEM((NSA_KV_DIM, N_CMP_PAD), BF16),
        ],
        compiler_params=pltpu.CompilerParams(
            dimension_semantics=("parallel", "arbitrary"), vmem_limit_bytes=VMEM_LIMIT),
        name="nsa_attn",
    )(h, kc, vc, h, h, h)


def _shift_rows(x, k, row):
    return jnp.where(row >= k, pltpu.roll(x, k, 0), 0.0)


def _pool_kernel(u_ref, w_ref, sc_ref, o_ref, *, seq):
    u = u_ref[...]
    row = lax.broadcasted_iota(jnp.int32, (seq, 1), 0)
    lane = lax.broadcasted_iota(jnp.int32, (1, POOL_DIM), 1)
    tp1 = (row + 1).astype(F32)
    acc = u
    mean = jnp.zeros_like(u)
    span = 1
    for gi, w in enumerate(POOL_WINDOWS):
        while span < w:
            acc = acc + _shift_rows(acc, span, row)
            span *= 2
        cnt = jnp.minimum(float(w), tp1)
        in_group = (lane >= gi * POOL_GROUP_DIM) & (lane < (gi + 1) * POOL_GROUP_DIM)
        mean = jnp.where(in_group, acc / cnt, mean)
    pooled = mean - u
    o_ref[...] = (_dot(pooled.astype(BF16), w_ref[...]) * sc_ref[...]).astype(o_ref.dtype)


def _pool(h, w_bd, scale, batch, seq):
    return pl.pallas_call(
        functools.partial(_pool_kernel, seq=seq),
        grid=(batch,),
        in_specs=[
            pl.BlockSpec((seq, POOL_DIM), lambda b: (b, 1280 // 256)),
            pl.BlockSpec((POOL_DIM, POOL_DIM), lambda b: (0, 0)),
            pl.BlockSpec((1, POOL_DIM), lambda b: (0, 0)),
        ],
        out_specs=pl.BlockSpec((seq, POOL_DIM), lambda b: (b, 0)),
        out_shape=jax.ShapeDtypeStruct((batch * seq, POOL_DIM), BF16),
        compiler_params=pltpu.CompilerParams(
            dimension_semantics=("parallel",), vmem_limit_bytes=VMEM_LIMIT),
        name="pool_mix",
    )(h, w_bd, scale)


def _gla_kernel(q_ref, k_ref, v_ref, a_ref, r_ref, wa2_ref, ba_ref, ng_ref, o_ref,
                qt_ref, kt_ref, kd_ref, dec_ref, oacc_ref, *, seq):
    C, H, DK, DV = GLA_CHUNK, GLA_HEADS, GLA_KEY_DIM, GLA_VAL_DIM
    nc = seq // C
    z = _dot(a_ref[...].astype(BF16), wa2_ref[...]) + ba_ref[...]
    log_a = (jnp.minimum(z, 0.0) - jnp.log(1.0 + jnp.exp(-jnp.abs(z)))) / GLA_TAU
    pos = lax.broadcasted_iota(jnp.int32, (seq, 1), 0) % C
    b = log_a
    step = 1
    while step < C:
        b = b + jnp.where(pos >= step, pltpu.roll(b, step, 0), 0.0)
        step *= 2
    b3 = b.reshape(nc, C, GLA_QK)
    b_last = b3[:, C - 1:C, :]
    qt_ref[...] = q_ref[...] * (DK ** -0.5) * jnp.exp(b)
    kt_ref[...] = k_ref[...] * jnp.exp(-b)
    kd_ref[...] = (k_ref[...].reshape(nc, C, GLA_QK) * jnp.exp(b_last - b3)).reshape(seq, GLA_QK)
    dec_ref[...] = jnp.exp(b_last)

    r_k = lax.broadcasted_iota(jnp.int32, (H * C, GLA_QK), 0) // C
    c_k = lax.broadcasted_iota(jnp.int32, (H * C, GLA_QK), 1) // DK
    mask_k = r_k == c_k
    r_v = lax.broadcasted_iota(jnp.int32, (H * C, GLA_OUT), 0) // C
    c_v = lax.broadcasted_iota(jnp.int32, (H * C, GLA_OUT), 1) // DV
    mask_v = r_v == c_v
    r_s = lax.broadcasted_iota(jnp.int32, (GLA_OUT, GLA_QK), 0) // DV
    c_s = lax.broadcasted_iota(jnp.int32, (GLA_OUT, GLA_QK), 1) // DK
    mask_s = r_s == c_s
    i_a = lax.broadcasted_iota(jnp.int32, (C, H * C), 0)
    j_a = lax.broadcasted_iota(jnp.int32, (C, H * C), 1) % C
    tril = j_a <= i_a

    def chunk(n, state_t):
        rows = pl.ds(pl.multiple_of(n * C, C), C)
        q_t = qt_ref[rows, :].astype(BF16)
        k_t = kt_ref[rows, :]
        k_d = kd_ref[rows, :].astype(BF16)
        v_c = v_ref[rows, :]
        k_bd = jnp.where(mask_k, jnp.concatenate([k_t] * H, axis=0), 0.0).astype(BF16)
        a_cat = jnp.where(tril, _dot_nt(q_t, k_bd), 0.0)
        v_bd = jnp.where(mask_v, jnp.concatenate([v_c] * H, axis=0), 0.0).astype(BF16)
        o_intra = _dot(a_cat.astype(BF16), v_bd)
        o_inter = _dot_nt(q_t, state_t.astype(BF16))
        oacc_ref[rows, :] = o_intra + o_inter
        d_state = jnp.where(mask_s, _dot(v_c.T.astype(BF16), k_d), 0.0)
        return state_t * dec_ref[n] + d_state

    lax.fori_loop(0, nc, chunk, jnp.zeros((GLA_OUT, GLA_QK), F32), unroll=8)

    o = oacc_ref[...]
    gr = lax.broadcasted_iota(jnp.int32, (GLA_OUT, GLA_OUT), 0) // DV
    gc = lax.broadcasted_iota(jnp.int32, (GLA_OUT, GLA_OUT), 1) // DV
    group_mean = jnp.where(gr == gc, 1.0 / DV, 0.0).astype(BF16)
    hi, mid, lo = _split3(o * o)
    ms = _dot(hi, group_mean) + _dot(mid, group_mean) + _dot(lo, group_mean)
    o = o * lax.rsqrt(ms + RMS_EPS) * ng_ref[...]
    r = r_ref[...]
    o_ref[...] = (o * (r * jax.nn.sigmoid(r))).astype(o_ref.dtype)


def _gla(h, wa2_pad, ba, norm_g, batch, seq):
    nc = seq // GLA_CHUNK
    return pl.pallas_call(
        functools.partial(_gla_kernel, seq=seq),
        grid=(batch,),
        in_specs=[
            pl.BlockSpec((seq, GLA_QK), lambda b: (b, 2048 // LANES)),
            pl.BlockSpec((seq, GLA_QK), lambda b: (b, 2176 // LANES)),
            pl.BlockSpec((seq, GLA_OUT), lambda b: (b, 1536 // 256)),
            pl.BlockSpec((seq, LANES), lambda b: (b, 2432 // LANES)),
            pl.BlockSpec((seq, GLA_OUT), lambda b: (b, 1792 // 256)),
            pl.BlockSpec((LANES, GLA_QK), lambda b: (0, 0)),
            pl.BlockSpec((1, GLA_QK), lambda b: (0, 0)),
            pl.BlockSpec((1, GLA_OUT), lambda b: (0, 0)),
        ],
        out_specs=pl.BlockSpec((seq, GLA_OUT), lambda b: (b, 0)),
        out_shape=jax.ShapeDtypeStruct((batch * seq, GLA_OUT), BF16),
        scratch_shapes=[
            pltpu.VMEM((seq, GLA_QK), F32),
            pltpu.VMEM((seq, GLA_QK), F32),
            pltpu.VMEM((seq, GLA_QK), F32),
            pltpu.VMEM((nc, 1, GLA_QK), F32),
            pltpu.VMEM((seq, GLA_OUT), F32),
        ],
        compiler_params=pltpu.CompilerParams(
            dimension_semantics=("parallel",), vmem_limit_bytes=VMEM_LIMIT),
        name="gla_mix",
    )(h, h, h, h, h, wa2_pad, ba, norm_g)


def _outproj_kernel(x_ref, oa_ref, ob_ref, oc_ref, wa_ref, wb_ref, wc_ref, g_ref, b_ref, o_ref):
    m = _dot(oa_ref[...], wa_ref[...]) + _dot(ob_ref[...], wb_ref[...]) + _dot(oc_ref[...], wc_ref[...])
    o_ref[...] = _layer_norm(ALPHA * x_ref[...] + m, g_ref[...], b_ref[...])


def _out_proj_ln(x, o_a, o_b, o_c, w_a, w_b, w_c, g, b, *, tm=512):
    n = x.shape[0]
    row = lambda i: (i, 0)
    const = lambda i: (0, 0)
    return pl.pallas_call(
        _outproj_kernel,
        grid=(n // tm,),
        in_specs=[
            pl.BlockSpec((tm, D_MODEL), row),
            pl.BlockSpec((tm, NSA_OUT), row),
            pl.BlockSpec((tm, POOL_DIM), row),
            pl.BlockSpec((tm, GLA_OUT), row),
            pl.BlockSpec((NSA_OUT, D_MODEL), const),
            pl.BlockSpec((POOL_DIM, D_MODEL), const),
            pl.BlockSpec((GLA_OUT, D_MODEL), const),
            pl.BlockSpec((1, D_MODEL), const),
            pl.BlockSpec((1, D_MODEL), const),
        ],
        out_specs=pl.BlockSpec((tm, D_MODEL), row),
        out_shape=jax.ShapeDtypeStruct((n, D_MODEL), F32),
        compiler_params=pltpu.CompilerParams(
            dimension_semantics=("parallel",), vmem_limit_bytes=VMEM_LIMIT),
        name="out_proj_ln",
    )(x, o_a, o_b, o_c, w_a, w_b, w_c, g, b)


def _pad_cols(w, width):
    return jnp.pad(w, ((0, 0), (0, width - w.shape[1])))


def _prep_w_in(w_in):
    parts = [w_in[:, 0:1280]]
    for name in ("u", "gv", "gr", "gq", "gk"):
        s, wd, _ = _IN_LAYOUT[name]
        parts.append(w_in[:, s:s + wd])
    for name in ("gl", "ga"):
        s, wd, _ = _IN_LAYOUT[name]
        parts.append(_pad_cols(w_in[:, s:s + wd], LANES))
    return jnp.concatenate(parts, axis=1).astype(BF16)


def _block_diag(blocks):
    n, r, c = blocks.shape
    eye = jnp.eye(n, dtype=blocks.dtype)
    return jnp.einsum("grc,gh->grhc", blocks, eye).reshape(n * r, n * c)


def _prep_compress(cmp_pe, cmp_w1, cmp_w2):
    G, HD = NSA_KV_GROUPS, NSA_HEAD_DIM
    pe = jnp.tile(cmp_pe[:, :, None, :], (1, 1, G, 1)).reshape(2, 1, CMP_BLOCK * G * HD)
    w1 = cmp_w1.reshape(2, CMP_BLOCK, HD, HD)
    eye = jnp.eye(G, dtype=cmp_w1.dtype)
    w1_bd = jnp.einsum("kpde,gh->kpgdhe", w1, eye).reshape(2, CMP_BLOCK * G * HD, G * HD)
    w2_bd = jnp.einsum("kde,gh->kgdhe", cmp_w2, eye).reshape(2, G * HD, G * HD)
    return pe, w1_bd.astype(BF16), w2_bd.astype(BF16)


def kernel(x, ln_g, ln_b, ffn_wg, ffn_wu, ffn_wd, w_in, w_out, cmp_pe, cmp_w1, cmp_w2,
           pool_w, pool_scale, gla_wa2, gla_ba, gla_norm_g):
    batch, seq, _ = x.shape
    xf = x.reshape(batch * seq, D_MODEL)
    for l in range(DEPTH):
        lg = lambda i: ln_g[l, i].reshape(1, D_MODEL)
        lb = lambda i: ln_b[l, i].reshape(1, D_MODEL)
        xf = _ffn_ln(xf, ffn_wg, ffn_wu, ffn_wd, lg(0), lb(0), l, 0)
        h = _in_proj(xf, _prep_w_in(w_in[l]))
        pe_rows, w1_bd, w2_bd = _prep_compress(cmp_pe[l], cmp_w1[l], cmp_w2[l])
        kc, vc = _compress(h, pe_rows, w1_bd, w2_bd, batch, seq)
        o_a = _nsa(h, kc, vc, batch, seq)
        o_b = _pool(h, _block_diag(pool_w[l]).astype(BF16), pool_scale[l].reshape(1, POOL_DIM), batch, seq)
        wa2_pad = jnp.pad(gla_wa2[l], ((0, LANES - GLA_GATE_RANK), (0, 0))).astype(BF16)
        o_c = _gla(h, wa2_pad, gla_ba[l].reshape(1, GLA_QK), gla_norm_g[l].reshape(1, GLA_OUT), batch, seq)
        wo = w_out[l].astype(BF16)
        xf = _out_proj_ln(xf, o_a, o_b, o_c, wo[:NSA_OUT], wo[NSA_OUT:NSA_OUT + POOL_DIM],
                          wo[NSA_OUT + POOL_DIM:], lg(1), lb(1))
        xf = _ffn_ln(xf, ffn_wg, ffn_wu, ffn_wd, lg(2), lb(2), l, 1)
    return xf.reshape(batch, seq, D_MODEL)
```

```python
import functools

import numpy as np
import jax
import jax.numpy as jnp
from jax import lax
from jax.experimental import pallas as pl
from jax.experimental.pallas import tpu as pltpu

F32 = jnp.float32
BF16 = jnp.bfloat16

D_MODEL = 1024
DEPTH = 4
D_FF = 2816
NSA_HEADS = 8
NSA_KV_GROUPS = 2
NSA_HEAD_DIM = 64
NSA_GROUP_SIZE = NSA_HEADS // NSA_KV_GROUPS
NSA_OUT = NSA_HEADS * NSA_HEAD_DIM
NSA_KV_DIM = NSA_KV_GROUPS * NSA_HEAD_DIM
CMP_BLOCK = 32
CMP_STRIDE = 16
SLC_BLOCK = 64
SLC_TOPK = 8
WINDOW = 512
Q_BLOCK = 128
POOL_WINDOWS = (2, 4, 8, 16)
POOL_GROUPS = 4
POOL_GROUP_DIM = 64
POOL_DIM = POOL_GROUPS * POOL_GROUP_DIM
GLA_HEADS = 4
GLA_KEY_DIM = 32
GLA_VAL_DIM = 64
GLA_GATE_RANK = 16
GLA_TAU = 16.0
GLA_CHUNK = 64
GLA_QK = GLA_HEADS * GLA_KEY_DIM
GLA_OUT = GLA_HEADS * GLA_VAL_DIM
ALPHA = (2.0 * DEPTH) ** 0.25
LN_EPS = 1e-5
RMS_EPS = 1e-6

LANES = 128
NEG = -1e30
VMEM_LIMIT = 56 * 1024 * 1024

_IN_LAYOUT = {
    "qkv": (0, 1280, 0),
    "u": (1304, 256, 1280),
    "gv": (1816, 256, 1536),
    "gr": (2088, 256, 1792),
    "gq": (1560, 128, 2048),
    "gk": (1688, 128, 2176),
    "gl": (1280, 24, 2304),
    "ga": (2072, 16, 2432),
}
D_IN_PAD = 2560


def _dot(a, b):
    return jnp.dot(a, b, preferred_element_type=F32)


def _dot_nt(a, b):
    return lax.dot_general(a, b, (((1,), (1,)), ((), ())), preferred_element_type=F32)


def _layer_norm(y, g, b):
    mu = jnp.mean(y, axis=-1, keepdims=True)
    yc = y - mu
    var = jnp.mean(yc * yc, axis=-1, keepdims=True)
    return yc * lax.rsqrt(var + LN_EPS) * g + b


def _split3(x):
    hi = x.astype(BF16)
    r1 = x - hi.astype(F32)
    mid = r1.astype(BF16)
    lo = (r1 - mid.astype(F32)).astype(BF16)
    return hi, mid, lo


FF_CHUNK = 256


N_FF_CHUNKS = D_FF // FF_CHUNK
FF_UNROLL = 5


def _ffn_kernel(x_ref, wg_hbm, wu_hbm, wd_hbm, g_ref, b_ref, o_ref,
                xb_ref, acc_ref, wg_s, wu_s, wd_s, sg_ref, su_ref, sd_ref, sem, *, layer, which):
    def stage_copies(j, slot):
        cols = pl.ds(j * FF_CHUNK, FF_CHUNK)
        return (pltpu.make_async_copy(wg_hbm.at[layer, which, :, cols], sg_ref.at[slot], sem.at[0, slot]),
                pltpu.make_async_copy(wu_hbm.at[layer, which, :, cols], su_ref.at[slot], sem.at[1, slot]),
                pltpu.make_async_copy(wd_hbm.at[layer, which, cols, :], sd_ref.at[slot], sem.at[2, slot]))

    @pl.when(pl.program_id(0) == 0)
    def _():
        for cp in stage_copies(0, 0):
            cp.start()
        for j in range(N_FF_CHUNKS):
            slot = j % 2
            if j + 1 < N_FF_CHUNKS:
                for cp in stage_copies(j + 1, 1 - slot):
                    cp.start()
            for cp in stage_copies(j, slot):
                cp.wait()
            wg_s[j] = sg_ref[slot].astype(BF16)
            wu_s[j] = su_ref[slot].astype(BF16)
            wd_s[j] = (0.5 * sd_ref[slot]).astype(BF16)

    xb_ref[...] = x_ref[...].astype(BF16)

    def contribution(j):
        xb = xb_ref[...]
        gate = _dot(xb, wg_s[j])
        up = _dot(xb, wu_s[j])
        act = (gate * jax.nn.sigmoid(gate)) * up
        return _dot(act.astype(BF16), wd_s[j])

    acc_ref[...] = contribution(0)

    def step(j, carry):
        acc_ref[...] += contribution(j)
        return carry

    lax.fori_loop(1, N_FF_CHUNKS, step, 0, unroll=FF_UNROLL)
    y = ALPHA * x_ref[...] + acc_ref[...]
    o_ref[...] = _layer_norm(y, g_ref[...], b_ref[...])


def _ffn_ln(x, wg, wu, wd, g, b, layer, which, *, tm=1024):
    n = x.shape[0]
    return pl.pallas_call(
        functools.partial(_ffn_kernel, layer=layer, which=which),
        grid=(n // tm,),
        in_specs=[
            pl.BlockSpec((tm, D_MODEL), lambda i: (i, 0)),
            pl.BlockSpec(memory_space=pl.ANY),
            pl.BlockSpec(memory_space=pl.ANY),
            pl.BlockSpec(memory_space=pl.ANY),
            pl.BlockSpec((1, D_MODEL), lambda i: (0, 0)),
            pl.BlockSpec((1, D_MODEL), lambda i: (0, 0)),
        ],
        out_specs=pl.BlockSpec((tm, D_MODEL), lambda i: (i, 0)),
        out_shape=jax.ShapeDtypeStruct((n, D_MODEL), F32),
        scratch_shapes=[
            pltpu.VMEM((tm, D_MODEL), BF16),
            pltpu.VMEM((tm, D_MODEL), F32),
            pltpu.VMEM((N_FF_CHUNKS, D_MODEL, FF_CHUNK), BF16),
            pltpu.VMEM((N_FF_CHUNKS, D_MODEL, FF_CHUNK), BF16),
            pltpu.VMEM((N_FF_CHUNKS, FF_CHUNK, D_MODEL), BF16),
            pltpu.VMEM((2, D_MODEL, FF_CHUNK), F32),
            pltpu.VMEM((2, D_MODEL, FF_CHUNK), F32),
            pltpu.VMEM((2, FF_CHUNK, D_MODEL), F32),
            pltpu.SemaphoreType.DMA((3, 2)),
        ],
        compiler_params=pltpu.CompilerParams(
            dimension_semantics=("arbitrary",), vmem_limit_bytes=VMEM_LIMIT),
        name="ffn_ln",
    )(x, wg, wu, wd, g, b)


D_IN = 2344
CMP_COLS = (512, 768)


def _inproj_kernel(x_ref, w_hbm, o_ref, okv_ref, w_s, stage_ref, sem, *, layer):
    @pl.when(pl.program_id(0) == 0)
    def _():
        cp = pltpu.make_async_copy(w_hbm.at[layer], stage_ref, sem.at[0])
        cp.start()
        cp.wait()
        w_s[...] = jnp.zeros_like(w_s)
        for src, width, dst in _IN_LAYOUT.values():
            w_s[:, dst:dst + width] = stage_ref[:, src:src + width].astype(BF16)

    h = _dot(x_ref[...].astype(BF16), w_s[...])
    o_ref[...] = h.astype(o_ref.dtype)
    okv_ref[...] = h[:, CMP_COLS[0]:CMP_COLS[1]]


def _in_proj(x, w_in, layer, *, tm=512):
    n = x.shape[0]
    n_cmp = CMP_COLS[1] - CMP_COLS[0]
    return pl.pallas_call(
        functools.partial(_inproj_kernel, layer=layer),
        grid=(n // tm,),
        in_specs=[
            pl.BlockSpec((tm, D_MODEL), lambda i: (i, 0)),
            pl.BlockSpec(memory_space=pl.ANY),
        ],
        out_specs=[pl.BlockSpec((tm, D_IN_PAD), lambda i: (i, 0)),
                   pl.BlockSpec((tm, n_cmp), lambda i: (i, 0))],
        out_shape=[jax.ShapeDtypeStruct((n, D_IN_PAD), BF16), jax.ShapeDtypeStruct((n, n_cmp), F32)],
        scratch_shapes=[
            pltpu.VMEM((D_MODEL, D_IN_PAD), BF16),
            pltpu.VMEM((D_MODEL, D_IN), F32),
            pltpu.SemaphoreType.DMA((1,)),
        ],
        compiler_params=pltpu.CompilerParams(
            dimension_semantics=("arbitrary",), vmem_limit_bytes=VMEM_LIMIT),
        name="in_proj",
    )(x, w_in)


N_CMP_PAD = 128


def _gelu_tanh(x):
    return 0.5 * x * (1.0 + jnp.tanh(np.sqrt(2.0 / np.pi) * (x + 0.044715 * (x * x * x))))


def _compress_kernel(zk_ref, zv_ref, pe_ref, w1_ref, w2_ref, ok_ref, ov_ref):
    def one(z_ref, which, o_ref):
        slabs = [z_ref[pl.ds(q, N_CMP_PAD, stride=CMP_STRIDE), :] for q in range(CMP_STRIDE)]
        cat = jnp.concatenate(slabs, axis=1)
        half = CMP_STRIDE * LANES
        top = _dot((cat + pe_ref[which, :, :half]).astype(BF16), w1_ref[which, :half, :])
        bot = _dot((cat + pe_ref[which, :, half:]).astype(BF16), w1_ref[which, half:, :])
        pre = top + pltpu.roll(bot, N_CMP_PAD - 1, 0)
        o_ref[...] = _dot(_gelu_tanh(pre).astype(BF16), w2_ref[which])

    one(zk_ref, 0, ok_ref)
    one(zv_ref, 1, ov_ref)


def _compress(h_kv, pe_rows, w1_bd, w2_bd, batch, seq):
    return pl.pallas_call(
        _compress_kernel,
        grid=(batch,),
        in_specs=[
            pl.BlockSpec((seq, LANES), lambda b: (b, 0)),
            pl.BlockSpec((seq, LANES), lambda b: (b, 1)),
            pl.BlockSpec((2, 1, CMP_BLOCK * LANES), lambda b: (0, 0, 0)),
            pl.BlockSpec((2, CMP_BLOCK * LANES, LANES), lambda b: (0, 0, 0)),
            pl.BlockSpec((2, LANES, LANES), lambda b: (0, 0, 0)),
        ],
        out_specs=[
            pl.BlockSpec((N_CMP_PAD, LANES), lambda b: (b, 0)),
            pl.BlockSpec((N_CMP_PAD, LANES), lambda b: (b, 0)),
        ],
        out_shape=[jax.ShapeDtypeStruct((batch * N_CMP_PAD, LANES), F32)] * 2,
        compiler_params=pltpu.CompilerParams(
            dimension_semantics=("parallel",), vmem_limit_bytes=VMEM_LIMIT),
        name="nsa_compress",
    )(h_kv, h_kv, pe_rows, w1_bd, w2_bd)


N_SLC = 32
WIN_KEYS = WINDOW + Q_BLOCK
KEY_CHUNK = 512
GQ = NSA_GROUP_SIZE * Q_BLOCK
PANEL = 256
SCORE_LOOKAHEAD = 9


VT_ROWS = NSA_HEAD_DIM + 16
LOG2E = 1.4426950408889634


def _softmax_cols(s):
    m = jnp.max(s, axis=0, keepdims=True)
    m = jnp.where(m > 0.5 * NEG, m, 0.0)
    e = jnp.exp2(s - m)
    d = jnp.sum(e, axis=0, keepdims=True)
    inv = 1.0 / jnp.where(d > 0.0, d, 1.0)
    return e, inv


def _tile4(x):
    return jnp.concatenate([x] * NSA_GROUP_SIZE, axis=1)


def _values_t(v_both, g):
    n = v_both.shape[0]
    v_t = v_both.T[g * NSA_HEAD_DIM:(g + 1) * NSA_HEAD_DIM, :]
    extra = (lax.broadcasted_iota(jnp.int32, (VT_ROWS - NSA_HEAD_DIM, n), 0) == 0).astype(F32)
    return jnp.concatenate([v_t, extra], axis=0).astype(BF16)


def _nsa_kernel(q_ref, kc_ref, vc_ref, ksvs_ref, kwvw_ref, gl_ref, o_ref,
                ks_s, vst_s, kw_s, vwt_s, kc_s, vct_s, *, seq):
    qi = pl.program_id(1)
    start = qi * Q_BLOCK
    hd = NSA_HEAD_DIM
    n_chunks = seq // KEY_CHUNK
    n_qb = seq // Q_BLOCK

    @pl.when(qi == 0)
    def _():
        key_blk = lax.broadcasted_iota(jnp.int32, (seq, LANES), 0) // SLC_BLOCK
        blk = lax.broadcasted_iota(jnp.int32, (seq, LANES), 1)
        ks_s[:, :NSA_KV_DIM] = ksvs_ref[:, :NSA_KV_DIM].astype(BF16)
        ks_s[:, NSA_KV_DIM:] = (key_blk == blk).astype(BF16)
        kw_s[...] = kwvw_ref[:, :NSA_KV_DIM].astype(BF16)
        for g in range(NSA_KV_GROUPS):
            for c in range(n_chunks):
                vst_s[c, g] = _values_t(ksvs_ref[c * KEY_CHUNK:(c + 1) * KEY_CHUNK, NSA_KV_DIM:].astype(F32), g)
            for j in range(n_qb):
                vwt_s[j, g] = _values_t(kwvw_ref[j * Q_BLOCK:(j + 1) * Q_BLOCK, NSA_KV_DIM:].astype(F32), g)
        kc_s[...] = kc_ref[...].astype(BF16)
        vct_s[...] = vc_ref[...].T.astype(BF16)

    t_row = start + lax.broadcasted_iota(jnp.int32, (1, Q_BLOCK), 1)

    q_t = (q_ref[...].astype(F32) * (hd ** -0.5 * LOG2E)).T.astype(BF16)
    zeros_half = jnp.zeros((hd, GQ), BF16)
    q_ops = []
    for g in range(NSA_KV_GROUPS):
        top = jnp.concatenate([q_t[(g * NSA_GROUP_SIZE + r) * hd:(g * NSA_GROUP_SIZE + r + 1) * hd, :]
                               for r in range(NSA_GROUP_SIZE)], axis=1)
        q_ops.append(jnp.concatenate([top, zeros_half] if g == 0 else [zeros_half, top], axis=0))

    n_sub = lax.broadcasted_iota(jnp.int32, (N_CMP_PAD, 1), 0)
    bias_c = jnp.where(n_sub * CMP_STRIDE + (CMP_BLOCK - 1) <= t_row, 0.0, NEG)
    m_sub = lax.broadcasted_iota(jnp.int32, (N_SLC, N_CMP_PAD), 0)
    n_lane = lax.broadcasted_iota(jnp.int32, (N_SLC, N_CMP_PAD), 1)
    c0 = n_lane * CMP_STRIDE
    s0 = m_sub * SLC_BLOCK
    ov_t = ((c0 <= s0 + SLC_BLOCK - 1) & (c0 + CMP_BLOCK - 1 >= s0)
            & (n_lane < seq // CMP_STRIDE - 1)).astype(BF16)
    m_idx = lax.broadcasted_iota(jnp.int32, (N_SLC, Q_BLOCK), 0)
    m_idx_f = m_idx.astype(F32)
    cur = t_row // SLC_BLOCK
    forced = (m_idx == 0) | (m_idx == cur) | (m_idx == cur - 1)
    future = m_idx * SLC_BLOCK > t_row

    def compressed_scores(g):
        return _dot(kc_s[...], q_ops[g]) + _tile4(bias_c)

    def compressed_branch(g, s_c):
        e, inv = _softmax_cols(s_c)
        p_c = e * inv
        o_cg = _dot(vct_s[g * hd:(g + 1) * hd, :], p_c.astype(BF16))
        p_sum = (p_c[:, 0:Q_BLOCK] + p_c[:, Q_BLOCK:2 * Q_BLOCK]
                 + p_c[:, 2 * Q_BLOCK:3 * Q_BLOCK] + p_c[:, 3 * Q_BLOCK:])
        hi, mid, lo = _split3(p_sum)
        imp = _dot(ov_t, hi) + _dot(ov_t, mid) + _dot(ov_t, lo)
        imp = jnp.where(forced, -NEG, jnp.where(future, NEG, imp))
        chosen = jnp.zeros((N_SLC, Q_BLOCK), jnp.bool_)
        for _ in range(SLC_TOPK):
            top = jnp.max(imp, axis=0, keepdims=True)
            first = jnp.min(jnp.where(imp == top, m_idx_f, float(N_SLC)), axis=0, keepdims=True)
            hit = m_idx_f == first
            chosen = chosen | hit
            imp = jnp.where(hit, 2.0 * NEG, imp)
        sel_bias = _tile4(jnp.where(chosen & jnp.logical_not(future), 0.0, NEG)).astype(BF16)
        return o_cg, jnp.concatenate([q_ops[g], sel_bias, jnp.zeros((LANES - N_SLC, GQ), BF16)], axis=0)

    j0 = jnp.maximum(qi - WINDOW // Q_BLOCK, 0)
    w0 = pl.multiple_of(j0 * Q_BLOCK, Q_BLOCK)
    panels = [(g, hp) for g in range(NSA_KV_GROUPS) for hp in range(GQ // PANEL)]
    init = (jnp.full((1, PANEL), NEG, F32), jnp.zeros((VT_ROWS, PANEL), F32)) * len(panels)


    def normalised(state):
        outs = []
        for g in range(NSA_KV_GROUPS):
            accs = [state[2 * panels.index((g, hp)) + 1] for hp in range(GQ // PANEL)]
            outs.append(jnp.concatenate([a[:hd] * (1.0 / a[hd:hd + 1]) for a in accs], axis=1))
        return outs

    def run_items(items, state, hooks=None):
        state = list(state)
        ahead = [it[1]() for it in items[:SCORE_LOOKAHEAD]]
        pending = None

        def flush(pending):
            i, alpha, p, values = pending
            state[2 * i + 1] = alpha * state[2 * i + 1] + _dot(values(), p)

        for n, (i, _, values) in enumerate(items):
            s = ahead.pop(0)
            if n + SCORE_LOOKAHEAD < len(items):
                ahead.append(items[n + SCORE_LOOKAHEAD][1]())
            m_p = state[2 * i]
            m_n = jnp.maximum(m_p, jnp.max(s, axis=0, keepdims=True))
            state[2 * i] = m_n
            p = jnp.exp2(s - m_n).astype(BF16)
            if pending is not None:
                flush(pending)
            pending = (i, jnp.exp2(m_p - m_n), p, values)
            if hooks and n in hooks:
                hooks[n]()
        flush(pending)
        return state

    items = []
    for o, n in ((0, 2 * Q_BLOCK), (2 * Q_BLOCK, 2 * Q_BLOCK), (4 * Q_BLOCK, Q_BLOCK)):
        kp_w = w0 + o + lax.broadcasted_iota(jnp.int32, (n, 1), 0)
        bias_w = jnp.where((kp_w <= t_row) & (kp_w > t_row - WINDOW), 0.0, NEG)
        bias_w = jnp.concatenate([bias_w, bias_w], axis=1)
        for i, (g, hp) in enumerate(panels):
            def score(o=o, n=n, g=g, hp=hp, bias_w=bias_w):
                return _dot(kw_s[pl.ds(w0 + o, n), :], q_ops[g][:, hp * PANEL:(hp + 1) * PANEL]) + bias_w

            def values(o=o, n=n, g=g):
                return jnp.concatenate([vwt_s[j0 + o // Q_BLOCK + j, g] for j in range(n // Q_BLOCK)], axis=1)
            items.append((i, score, values))
    window_items = items

    def chunk_items(c, diagonal, sels):
        items = []
        for sub in range(KEY_CHUNK // PANEL):
            off = pl.multiple_of(c * KEY_CHUNK + sub * PANEL, PANEL)
            causal_bias = None
            if diagonal:
                kpos = off + lax.broadcasted_iota(jnp.int32, (PANEL, 1), 0)
                causal_bias = jnp.where(kpos <= t_row, 0.0, NEG)
                causal_bias = jnp.concatenate([causal_bias, causal_bias], axis=1)
            for i, (g, hp) in enumerate(panels):
                def score(off=off, g=g, hp=hp, causal_bias=causal_bias):
                    s = _dot(ks_s[pl.ds(off, PANEL), :], sels[g][:, hp * PANEL:(hp + 1) * PANEL])
                    return s if causal_bias is None else s + causal_bias

                def values(sub=sub, g=g):
                    return vst_s[c, g, :, sub * PANEL:(sub + 1) * PANEL]
                items.append((len(panels) + i, score, values))
        return items

    s_cs = [compressed_scores(g) for g in range(NSA_KV_GROUPS)]
    cmp_out = {}

    def hook(g):
        return lambda: cmp_out.__setitem__(g, compressed_branch(g, s_cs[g]))

    o_w = normalised(run_items(window_items, init, hooks={1 + 4 * g: hook(g) for g in range(NSA_KV_GROUPS)}))
    o_c, sels = zip(*[cmp_out[g] for g in range(NSA_KV_GROUPS)])

    def sequence(n_past):
        def run():
            past = [it for c in range(n_past) for it in chunk_items(c, False, sels)]
            return tuple(run_items(chunk_items(n_past, True, sels) + past, init + init)[len(init):])
        return run

    o_s = normalised(lax.switch(qi // (KEY_CHUNK // Q_BLOCK), [sequence(k) for k in range(n_chunks)]))

    gates_t = jax.nn.sigmoid(gl_ref[...].astype(F32)).T
    outs = []
    for h in range(NSA_HEADS):
        g, r = divmod(h, NSA_GROUP_SIZE)
        lanes = slice(r * Q_BLOCK, (r + 1) * Q_BLOCK)
        outs.append(gates_t[3 * h:3 * h + 1, :] * o_c[g][:, lanes]
                    + gates_t[3 * h + 1:3 * h + 2, :] * o_s[g][:, lanes]
                    + gates_t[3 * h + 2:3 * h + 3, :] * o_w[g][:, lanes])
    o_ref[...] = jnp.concatenate(outs, axis=0).T.astype(o_ref.dtype)


def _nsa(h, kc, vc, batch, seq):
    nqb = seq // Q_BLOCK
    return pl.pallas_call(
        functools.partial(_nsa_kernel, seq=seq),
        grid=(batch, nqb),
        in_specs=[
            pl.BlockSpec((Q_BLOCK, NSA_OUT), lambda b, i: (b * nqb + i, 0)),
            pl.BlockSpec((N_CMP_PAD, LANES), lambda b, i: (b, 0)),
            pl.BlockSpec((N_CMP_PAD, LANES), lambda b, i: (b, 0)),
            pl.BlockSpec((seq, 2 * NSA_KV_DIM), lambda b, i: (b, 768 // 256)),
            pl.BlockSpec((seq, 2 * NSA_KV_DIM), lambda b, i: (b, 1024 // 256)),
            pl.BlockSpec((Q_BLOCK, LANES), lambda b, i: (b * nqb + i, 2304 // LANES)),
        ],
        out_specs=pl.BlockSpec((Q_BLOCK, NSA_OUT), lambda b, i: (b * nqb + i, 0)),
        out_shape=jax.ShapeDtypeStruct((batch * seq, NSA_OUT), BF16),
        scratch_shapes=[
            pltpu.VMEM((seq, NSA_KV_DIM + LANES), BF16),
            pltpu.VMEM((seq // KEY_CHUNK, NSA_KV_GROUPS, VT_ROWS, KEY_CHUNK), BF16),
            pltpu.VMEM((seq, NSA_KV_DIM), BF16),
            pltpu.VMEM((nqb, NSA_KV_GROUPS, VT_ROWS, Q_BLOCK), BF16),
            pltpu.VMEM((N_CMP_PAD, NSA_KV_DIM), BF16),
            pltpu.VMEM((NSA_KV_DIM, N_CMP_PAD), BF16),
        ],
        compiler_params=pltpu.CompilerParams(
            dimension_semantics=("parallel", "arbitrary"), vmem_limit_bytes=VMEM_LIMIT),
        name="nsa_attn",
    )(h, kc, vc, h, h, h)


def _shift_rows(x, k, row):
    return jnp.where(row >= k, pltpu.roll(x, k, 0), 0.0)


def _pool_kernel(u_ref, w_ref, sc_ref, o_ref, *, seq):
    u = u_ref[...].astype(F32)
    row = lax.broadcasted_iota(jnp.int32, (seq, 1), 0)
    lane = lax.broadcasted_iota(jnp.int32, (1, POOL_DIM), 1)
    tp1 = (row + 1).astype(F32)
    acc = u
    mean = jnp.zeros_like(u)
    span = 1
    for gi, w in enumerate(POOL_WINDOWS):
        while span < w:
            acc = acc + _shift_rows(acc, span, row)
            span *= 2
        cnt = jnp.minimum(float(w), tp1)
        in_group = (lane >= gi * POOL_GROUP_DIM) & (lane < (gi + 1) * POOL_GROUP_DIM)
        mean = jnp.where(in_group, acc / cnt, mean)
    pooled = mean - u
    o_ref[...] = (_dot(pooled.astype(BF16), w_ref[...]) * sc_ref[...]).astype(o_ref.dtype)


def _pool(h, w_bd, scale, batch, seq):
    return pl.pallas_call(
        functools.partial(_pool_kernel, seq=seq),
        grid=(batch,),
        in_specs=[
            pl.BlockSpec((seq, POOL_DIM), lambda b: (b, 1280 // 256)),
            pl.BlockSpec((POOL_DIM, POOL_DIM), lambda b: (0, 0)),
            pl.BlockSpec((1, POOL_DIM), lambda b: (0, 0)),
        ],
        out_specs=pl.BlockSpec((seq, POOL_DIM), lambda b: (b, 0)),
        out_shape=jax.ShapeDtypeStruct((batch * seq, POOL_DIM), BF16),
        compiler_params=pltpu.CompilerParams(
            dimension_semantics=("parallel",), vmem_limit_bytes=VMEM_LIMIT),
        name="pool_mix",
    )(h, w_bd, scale)


def _gla_kernel(q_ref, k_ref, v_ref, a_ref, r_ref, wa2_ref, ba_ref, ng_ref, o_ref,
                qt_ref, kt_ref, kd_ref, dec_ref, oacc_ref, *, seq):
    C, H, DK, DV = GLA_CHUNK, GLA_HEADS, GLA_KEY_DIM, GLA_VAL_DIM
    nc = seq // C
    z = _dot(a_ref[...].astype(BF16), wa2_ref[...]) + ba_ref[...]
    log_a = (jnp.minimum(z, 0.0) - jnp.log(1.0 + jnp.exp(-jnp.abs(z)))) / GLA_TAU
    pos = lax.broadcasted_iota(jnp.int32, (seq, 1), 0) % C
    b = log_a
    step = 1
    while step < C:
        b = b + jnp.where(pos >= step, pltpu.roll(b, step, 0), 0.0)
        step *= 2
    b3 = b.reshape(nc, C, GLA_QK)
    b_last = b3[:, C - 1:C, :]
    k_all = k_ref[...].astype(F32)
    qt_ref[...] = q_ref[...].astype(F32) * (DK ** -0.5) * jnp.exp(b)
    kt_ref[...] = k_all * jnp.exp(-b)
    kd_ref[...] = (k_all.reshape(nc, C, GLA_QK) * jnp.exp(b_last - b3)).reshape(seq, GLA_QK)
    dec_ref[...] = jnp.exp(b_last)

    r_k = lax.broadcasted_iota(jnp.int32, (H * C, GLA_QK), 0) // C
    c_k = lax.broadcasted_iota(jnp.int32, (H * C, GLA_QK), 1) // DK
    mask_k = r_k == c_k
    r_v = lax.broadcasted_iota(jnp.int32, (H * C, GLA_OUT), 0) // C
    c_v = lax.broadcasted_iota(jnp.int32, (H * C, GLA_OUT), 1) // DV
    mask_v = r_v == c_v
    r_s = lax.broadcasted_iota(jnp.int32, (GLA_OUT, GLA_QK), 0) // DV
    c_s = lax.broadcasted_iota(jnp.int32, (GLA_OUT, GLA_QK), 1) // DK
    mask_s = r_s == c_s
    i_a = lax.broadcasted_iota(jnp.int32, (C, H * C), 0)
    j_a = lax.broadcasted_iota(jnp.int32, (C, H * C), 1) % C
    tril = j_a <= i_a

    def chunk(n, state_t):
        rows = pl.ds(pl.multiple_of(n * C, C), C)
        q_t = qt_ref[rows, :].astype(BF16)
        k_t = kt_ref[rows, :]
        k_d = kd_ref[rows, :].astype(BF16)
        v_c = v_ref[rows, :].astype(F32)
        k_bd = jnp.where(mask_k, jnp.concatenate([k_t] * H, axis=0), 0.0).astype(BF16)
        a_cat = jnp.where(tril, _dot_nt(q_t, k_bd), 0.0)
        v_bd = jnp.where(mask_v, jnp.concatenate([v_c] * H, axis=0), 0.0).astype(BF16)
        o_intra = _dot(a_cat.astype(BF16), v_bd)
        o_inter = _dot_nt(q_t, state_t.astype(BF16))
        oacc_ref[rows, :] = o_intra + o_inter
        d_state = jnp.where(mask_s, _dot(v_c.T.astype(BF16), k_d), 0.0)
        return state_t * dec_ref[n] + d_state

    lax.fori_loop(0, nc, chunk, jnp.zeros((GLA_OUT, GLA_QK), F32), unroll=8)

    o = oacc_ref[...]
    gr = lax.broadcasted_iota(jnp.int32, (GLA_OUT, GLA_OUT), 0) // DV
    gc = lax.broadcasted_iota(jnp.int32, (GLA_OUT, GLA_OUT), 1) // DV
    group_mean = jnp.where(gr == gc, 1.0 / DV, 0.0).astype(BF16)
    hi, mid, lo = _split3(o * o)
    ms = _dot(hi, group_mean) + _dot(mid, group_mean) + _dot(lo, group_mean)
    o = o * lax.rsqrt(ms + RMS_EPS) * ng_ref[...]
    r = r_ref[...].astype(F32)
    o_ref[...] = (o * (r * jax.nn.sigmoid(r))).astype(o_ref.dtype)


def _gla(h, wa2_pad, ba, norm_g, batch, seq):
    nc = seq // GLA_CHUNK
    return pl.pallas_call(
        functools.partial(_gla_kernel, seq=seq),
        grid=(batch,),
        in_specs=[
            pl.BlockSpec((seq, GLA_QK), lambda b: (b, 2048 // LANES)),
            pl.BlockSpec((seq, GLA_QK), lambda b: (b, 2176 // LANES)),
            pl.BlockSpec((seq, GLA_OUT), lambda b: (b, 1536 // 256)),
            pl.BlockSpec((seq, LANES), lambda b: (b, 2432 // LANES)),
            pl.BlockSpec((seq, GLA_OUT), lambda b: (b, 1792 // 256)),
            pl.BlockSpec((LANES, GLA_QK), lambda b: (0, 0)),
            pl.BlockSpec((1, GLA_QK), lambda b: (0, 0)),
            pl.BlockSpec((1, GLA_OUT), lambda b: (0, 0)),
        ],
        out_specs=pl.BlockSpec((seq, GLA_OUT), lambda b: (b, 0)),
        out_shape=jax.ShapeDtypeStruct((batch * seq, GLA_OUT), BF16),
        scratch_shapes=[
            pltpu.VMEM((seq, GLA_QK), F32),
            pltpu.VMEM((seq, GLA_QK), F32),
            pltpu.VMEM((seq, GLA_QK), F32),
            pltpu.VMEM((nc, 1, GLA_QK), F32),
            pltpu.VMEM((seq, GLA_OUT), F32),
        ],
        compiler_params=pltpu.CompilerParams(
            dimension_semantics=("parallel",), vmem_limit_bytes=VMEM_LIMIT),
        name="gla_mix",
    )(h, h, h, h, h, wa2_pad, ba, norm_g)


def _outproj_kernel(x_ref, oa_ref, ob_ref, oc_ref, wa_ref, wb_ref, wc_ref, g_ref, b_ref, o_ref):
    m = _dot(oa_ref[...], wa_ref[...]) + _dot(ob_ref[...], wb_ref[...]) + _dot(oc_ref[...], wc_ref[...])
    o_ref[...] = _layer_norm(ALPHA * x_ref[...] + m, g_ref[...], b_ref[...])


def _out_proj_ln(x, o_a, o_b, o_c, w_a, w_b, w_c, g, b, *, tm=512):
    n = x.shape[0]
    row = lambda i: (i, 0)
    const = lambda i: (0, 0)
    return pl.pallas_call(
        _outproj_kernel,
        grid=(n // tm,),
        in_specs=[
            pl.BlockSpec((tm, D_MODEL), row),
            pl.BlockSpec((tm, NSA_OUT), row),
            pl.BlockSpec((tm, POOL_DIM), row),
            pl.BlockSpec((tm, GLA_OUT), row),
            pl.BlockSpec((NSA_OUT, D_MODEL), const),
            pl.BlockSpec((POOL_DIM, D_MODEL), const),
            pl.BlockSpec((GLA_OUT, D_MODEL), const),
            pl.BlockSpec((1, D_MODEL), const),
            pl.BlockSpec((1, D_MODEL), const),
        ],
        out_specs=pl.BlockSpec((tm, D_MODEL), row),
        out_shape=jax.ShapeDtypeStruct((n, D_MODEL), F32),
        compiler_params=pltpu.CompilerParams(
            dimension_semantics=("parallel",), vmem_limit_bytes=VMEM_LIMIT),
        name="out_proj_ln",
    )(x, o_a, o_b, o_c, w_a, w_b, w_c, g, b)


def _block_diag(blocks):
    n, r, c = blocks.shape
    eye = jnp.eye(n, dtype=blocks.dtype)
    return jnp.einsum("grc,gh->grhc", blocks, eye).reshape(n * r, n * c)


def _prep_compress(cmp_pe, cmp_w1, cmp_w2):
    G, HD = NSA_KV_GROUPS, NSA_HEAD_DIM
    pe = jnp.tile(cmp_pe[:, :, None, :], (1, 1, G, 1)).reshape(2, 1, CMP_BLOCK * G * HD)
    w1 = cmp_w1.reshape(2, CMP_BLOCK, HD, HD)
    eye = jnp.eye(G, dtype=cmp_w1.dtype)
    w1_bd = jnp.einsum("kpde,gh->kpgdhe", w1, eye).reshape(2, CMP_BLOCK * G * HD, G * HD)
    w2_bd = jnp.einsum("kde,gh->kgdhe", cmp_w2, eye).reshape(2, G * HD, G * HD)
    return pe, w1_bd.astype(BF16), w2_bd.astype(BF16)


def kernel(x, ln_g, ln_b, ffn_wg, ffn_wu, ffn_wd, w_in, w_out, cmp_pe, cmp_w1, cmp_w2,
           pool_w, pool_scale, gla_wa2, gla_ba, gla_norm_g):
    batch, seq, _ = x.shape
    xf = x.reshape(batch * seq, D_MODEL)
    for l in range(DEPTH):
        lg = lambda i: ln_g[l, i].reshape(1, D_MODEL)
        lb = lambda i: ln_b[l, i].reshape(1, D_MODEL)
        xf = _ffn_ln(xf, ffn_wg, ffn_wu, ffn_wd, lg(0), lb(0), l, 0)
        h, h_kv = _in_proj(xf, w_in, l)
        pe_rows, w1_bd, w2_bd = _prep_compress(cmp_pe[l], cmp_w1[l], cmp_w2[l])
        kc, vc = _compress(h_kv, pe_rows, w1_bd, w2_bd, batch, seq)
        o_a = _nsa(h, kc, vc, batch, seq)
        o_b = _pool(h, _block_diag(pool_w[l]).astype(BF16), pool_scale[l].reshape(1, POOL_DIM), batch, seq)
        wa2_pad = jnp.pad(gla_wa2[l], ((0, LANES - GLA_GATE_RANK), (0, 0))).astype(BF16)
        o_c = _gla(h, wa2_pad, gla_ba[l].reshape(1, GLA_QK), gla_norm_g[l].reshape(1, GLA_OUT), batch, seq)
        wo = w_out[l].astype(BF16)
        xf = _out_proj_ln(xf, o_a, o_b, o_c, wo[:NSA_OUT], wo[NSA_OUT:NSA_OUT + POOL_DIM],
                          wo[NSA_OUT + POOL_DIM:], lg(1), lb(1))
        xf = _ffn_ln(xf, ffn_wg, ffn_wu, ffn_wd, lg(2), lb(2), l, 1)
    return xf.reshape(batch, seq, D_MODEL)
```

```python
import functools

import numpy as np
import jax
import jax.numpy as jnp
from jax import lax
from jax.experimental import pallas as pl
from jax.experimental.pallas import tpu as pltpu

F32 = jnp.float32
BF16 = jnp.bfloat16

D_MODEL = 1024
DEPTH = 4
D_FF = 2816
NSA_HEADS = 8
NSA_KV_GROUPS = 2
NSA_HEAD_DIM = 64
NSA_GROUP_SIZE = NSA_HEADS // NSA_KV_GROUPS
NSA_OUT = NSA_HEADS * NSA_HEAD_DIM
NSA_KV_DIM = NSA_KV_GROUPS * NSA_HEAD_DIM
CMP_BLOCK = 32
CMP_STRIDE = 16
SLC_BLOCK = 64
SLC_TOPK = 8
WINDOW = 512
Q_BLOCK = 128
POOL_WINDOWS = (2, 4, 8, 16)
POOL_GROUPS = 4
POOL_GROUP_DIM = 64
POOL_DIM = POOL_GROUPS * POOL_GROUP_DIM
GLA_HEADS = 4
GLA_KEY_DIM = 32
GLA_VAL_DIM = 64
GLA_GATE_RANK = 16
GLA_TAU = 16.0
GLA_CHUNK = 64
GLA_QK = GLA_HEADS * GLA_KEY_DIM
GLA_OUT = GLA_HEADS * GLA_VAL_DIM
ALPHA = (2.0 * DEPTH) ** 0.25
LN_EPS = 1e-5
RMS_EPS = 1e-6

LANES = 128
NEG = -1e30
VMEM_LIMIT = 56 * 1024 * 1024

_IN_LAYOUT = {
    "qkv": (0, 1280, 0),
    "u": (1304, 256, 1280),
    "gv": (1816, 256, 1536),
    "gr": (2088, 256, 1792),
    "gq": (1560, 128, 2048),
    "gk": (1688, 128, 2176),
    "gl": (1280, 24, 2304),
    "ga": (2072, 16, 2432),
}
D_IN_PAD = 2560


def _dot(a, b):
    return jnp.dot(a, b, preferred_element_type=F32)


def _dot_nt(a, b):
    return lax.dot_general(a, b, (((1,), (1,)), ((), ())), preferred_element_type=F32)


def _layer_norm(y, g, b):
    mu = jnp.mean(y, axis=-1, keepdims=True)
    yc = y - mu
    var = jnp.mean(yc * yc, axis=-1, keepdims=True)
    return yc * lax.rsqrt(var + LN_EPS) * g + b


def _split3(x):
    hi = x.astype(BF16)
    r1 = x - hi.astype(F32)
    mid = r1.astype(BF16)
    lo = (r1 - mid.astype(F32)).astype(BF16)
    return hi, mid, lo


FF_CHUNK = 256


N_FF_CHUNKS = D_FF // FF_CHUNK
FF_UNROLL = 5


def _ffn_kernel(x_ref, wg_hbm, wu_hbm, wd_hbm, g_ref, b_ref, o_ref,
                xb_ref, acc_ref, wg_s, wu_s, wd_s, sg_ref, su_ref, sd_ref, sem, *, layer, which):
    def stage_copies(j, slot):
        cols = pl.ds(j * FF_CHUNK, FF_CHUNK)
        return (pltpu.make_async_copy(wg_hbm.at[layer, which, :, cols], sg_ref.at[slot], sem.at[0, slot]),
                pltpu.make_async_copy(wu_hbm.at[layer, which, :, cols], su_ref.at[slot], sem.at[1, slot]),
                pltpu.make_async_copy(wd_hbm.at[layer, which, cols, :], sd_ref.at[slot], sem.at[2, slot]))

    @pl.when(pl.program_id(0) == 0)
    def _():
        for cp in stage_copies(0, 0):
            cp.start()
        for j in range(N_FF_CHUNKS):
            slot = j % 2
            if j + 1 < N_FF_CHUNKS:
                for cp in stage_copies(j + 1, 1 - slot):
                    cp.start()
            for cp in stage_copies(j, slot):
                cp.wait()
            wg_s[j] = sg_ref[slot].astype(BF16)
            wu_s[j] = su_ref[slot].astype(BF16)
            wd_s[j] = (0.5 * sd_ref[slot]).astype(BF16)

    xb_ref[...] = x_ref[...].astype(BF16)

    def contribution(j):
        xb = xb_ref[...]
        gate = _dot(xb, wg_s[j])
        up = _dot(xb, wu_s[j])
        act = (gate * jax.nn.sigmoid(gate)) * up
        return _dot(act.astype(BF16), wd_s[j])

    acc_ref[...] = contribution(0)

    def step(j, carry):
        acc_ref[...] += contribution(j)
        return carry

    lax.fori_loop(1, N_FF_CHUNKS, step, 0, unroll=FF_UNROLL)
    y = ALPHA * x_ref[...] + acc_ref[...]
    o_ref[...] = _layer_norm(y, g_ref[...], b_ref[...])


def _ffn_ln(x, wg, wu, wd, g, b, layer, which, *, tm=1024):
    n = x.shape[0]
    return pl.pallas_call(
        functools.partial(_ffn_kernel, layer=layer, which=which),
        grid=(n // tm,),
        in_specs=[
            pl.BlockSpec((tm, D_MODEL), lambda i: (i, 0)),
            pl.BlockSpec(memory_space=pl.ANY),
            pl.BlockSpec(memory_space=pl.ANY),
            pl.BlockSpec(memory_space=pl.ANY),
            pl.BlockSpec((1, D_MODEL), lambda i: (0, 0)),
            pl.BlockSpec((1, D_MODEL), lambda i: (0, 0)),
        ],
        out_specs=pl.BlockSpec((tm, D_MODEL), lambda i: (i, 0)),
        out_shape=jax.ShapeDtypeStruct((n, D_MODEL), F32),
        scratch_shapes=[
            pltpu.VMEM((tm, D_MODEL), BF16),
            pltpu.VMEM((tm, D_MODEL), F32),
            pltpu.VMEM((N_FF_CHUNKS, D_MODEL, FF_CHUNK), BF16),
            pltpu.VMEM((N_FF_CHUNKS, D_MODEL, FF_CHUNK), BF16),
            pltpu.VMEM((N_FF_CHUNKS, FF_CHUNK, D_MODEL), BF16),
            pltpu.VMEM((2, D_MODEL, FF_CHUNK), F32),
            pltpu.VMEM((2, D_MODEL, FF_CHUNK), F32),
            pltpu.VMEM((2, FF_CHUNK, D_MODEL), F32),
            pltpu.SemaphoreType.DMA((3, 2)),
        ],
        compiler_params=pltpu.CompilerParams(
            dimension_semantics=("arbitrary",), vmem_limit_bytes=VMEM_LIMIT),
        name="ffn_ln",
    )(x, wg, wu, wd, g, b)


D_IN = 2344
CMP_COLS = (512, 768)


def _inproj_kernel(x_ref, w_hbm, o_ref, okv_ref, w_s, stage_ref, sem, *, layer):
    @pl.when(pl.program_id(0) == 0)
    def _():
        cp = pltpu.make_async_copy(w_hbm.at[layer], stage_ref, sem.at[0])
        cp.start()
        cp.wait()
        w_s[...] = jnp.zeros_like(w_s)
        for src, width, dst in _IN_LAYOUT.values():
            w_s[dst:dst + width, :] = stage_ref[src:src + width, :].astype(BF16)

    h = _dot_nt(x_ref[...].astype(BF16), w_s[...])
    o_ref[...] = h.astype(o_ref.dtype)
    okv_ref[...] = h[:, CMP_COLS[0]:CMP_COLS[1]]


def _in_proj(x, w_in_t, layer, *, tm=512):
    n = x.shape[0]
    n_cmp = CMP_COLS[1] - CMP_COLS[0]
    return pl.pallas_call(
        functools.partial(_inproj_kernel, layer=layer),
        grid=(n // tm,),
        in_specs=[
            pl.BlockSpec((tm, D_MODEL), lambda i: (i, 0)),
            pl.BlockSpec(memory_space=pl.ANY),
        ],
        out_specs=[pl.BlockSpec((tm, D_IN_PAD), lambda i: (i, 0)),
                   pl.BlockSpec((tm, n_cmp), lambda i: (i, 0))],
        out_shape=[jax.ShapeDtypeStruct((n, D_IN_PAD), BF16), jax.ShapeDtypeStruct((n, n_cmp), F32)],
        scratch_shapes=[
            pltpu.VMEM((D_IN_PAD, D_MODEL), BF16),
            pltpu.VMEM((D_IN, D_MODEL), F32),
            pltpu.SemaphoreType.DMA((1,)),
        ],
        compiler_params=pltpu.CompilerParams(
            dimension_semantics=("arbitrary",), vmem_limit_bytes=VMEM_LIMIT),
        name="in_proj",
    )(x, w_in_t)


N_CMP_PAD = 128


def _gelu_tanh(x):
    return 0.5 * x * (1.0 + jnp.tanh(np.sqrt(2.0 / np.pi) * (x + 0.044715 * (x * x * x))))


def _compress_kernel(zk_ref, zv_ref, pe_ref, w1_ref, w2_ref, ok_ref, ov_ref):
    def one(z_ref, which, o_ref):
        slabs = [z_ref[pl.ds(q, N_CMP_PAD, stride=CMP_STRIDE), :] for q in range(CMP_STRIDE)]
        cat = jnp.concatenate(slabs, axis=1)
        half = CMP_STRIDE * LANES
        top = _dot((cat + pe_ref[which, :, :half]).astype(BF16), w1_ref[which, :half, :])
        bot = _dot((cat + pe_ref[which, :, half:]).astype(BF16), w1_ref[which, half:, :])
        pre = top + pltpu.roll(bot, N_CMP_PAD - 1, 0)
        o_ref[...] = _dot(_gelu_tanh(pre).astype(BF16), w2_ref[which])

    one(zk_ref, 0, ok_ref)
    one(zv_ref, 1, ov_ref)


def _compress(h_kv, pe_rows, w1_bd, w2_bd, batch, seq):
    return pl.pallas_call(
        _compress_kernel,
        grid=(batch,),
        in_specs=[
            pl.BlockSpec((seq, LANES), lambda b: (b, 0)),
            pl.BlockSpec((seq, LANES), lambda b: (b, 1)),
            pl.BlockSpec((2, 1, CMP_BLOCK * LANES), lambda b: (0, 0, 0)),
            pl.BlockSpec((2, CMP_BLOCK * LANES, LANES), lambda b: (0, 0, 0)),
            pl.BlockSpec((2, LANES, LANES), lambda b: (0, 0, 0)),
        ],
        out_specs=[
            pl.BlockSpec((N_CMP_PAD, LANES), lambda b: (b, 0)),
            pl.BlockSpec((N_CMP_PAD, LANES), lambda b: (b, 0)),
        ],
        out_shape=[jax.ShapeDtypeStruct((batch * N_CMP_PAD, LANES), F32)] * 2,
        compiler_params=pltpu.CompilerParams(
            dimension_semantics=("parallel",), vmem_limit_bytes=VMEM_LIMIT),
        name="nsa_compress",
    )(h_kv, h_kv, pe_rows, w1_bd, w2_bd)


N_SLC = 32
WIN_KEYS = WINDOW + Q_BLOCK
KEY_CHUNK = 512
GQ = NSA_GROUP_SIZE * Q_BLOCK
PANEL = 256
SCORE_LOOKAHEAD = 9


VT_ROWS = NSA_HEAD_DIM + 16
LOG2E = 1.4426950408889634


def _softmax_cols(s):
    m = jnp.max(s, axis=0, keepdims=True)
    m = jnp.where(m > 0.5 * NEG, m, 0.0)
    e = jnp.exp2(s - m)
    d = jnp.sum(e, axis=0, keepdims=True)
    inv = 1.0 / jnp.where(d > 0.0, d, 1.0)
    return e, inv


def _tile4(x):
    return jnp.concatenate([x] * NSA_GROUP_SIZE, axis=1)


def _values_t(v_both):
    n = v_both.shape[0]
    v_t = v_both.astype(F32).T
    extra = (lax.broadcasted_iota(jnp.int32, (VT_ROWS - NSA_HEAD_DIM, n), 0) == 0).astype(F32)
    return [jnp.concatenate([v_t[g * NSA_HEAD_DIM:(g + 1) * NSA_HEAD_DIM, :], extra], axis=0).astype(BF16)
            for g in range(NSA_KV_GROUPS)]


def _nsa_kernel(q_ref, kc_ref, vc_ref, ksvs_ref, kwvw_ref, gl_ref, o_ref,
                ks_s, vst_s, kw_s, vwt_s, kc_s, vct_s, *, seq):
    qi = pl.program_id(1)
    start = qi * Q_BLOCK
    hd = NSA_HEAD_DIM
    n_chunks = seq // KEY_CHUNK
    n_qb = seq // Q_BLOCK

    @pl.when(qi == 0)
    def _():
        key_blk = lax.broadcasted_iota(jnp.int32, (seq, LANES), 0) // SLC_BLOCK
        blk = lax.broadcasted_iota(jnp.int32, (seq, LANES), 1)
        ks_s[:, :NSA_KV_DIM] = ksvs_ref[:, :NSA_KV_DIM].astype(BF16)
        ks_s[:, NSA_KV_DIM:] = (key_blk == blk).astype(BF16)
        kw_s[...] = kwvw_ref[:, :NSA_KV_DIM].astype(BF16)
        for c in range(n_chunks):
            for g, v_t in enumerate(_values_t(ksvs_ref[c * KEY_CHUNK:(c + 1) * KEY_CHUNK, NSA_KV_DIM:])):
                vst_s[c, g] = v_t
        for j in range(n_qb):
            for g, v_t in enumerate(_values_t(kwvw_ref[j * Q_BLOCK:(j + 1) * Q_BLOCK, NSA_KV_DIM:])):
                vwt_s[j, g] = v_t
        kc_s[...] = kc_ref[...].astype(BF16)
        vct_s[...] = vc_ref[...].T.astype(BF16)

    t_row = start + lax.broadcasted_iota(jnp.int32, (1, Q_BLOCK), 1)

    q_t = (q_ref[...].astype(F32) * (hd ** -0.5 * LOG2E)).T.astype(BF16)
    zeros_half = jnp.zeros((hd, GQ), BF16)
    q_ops = []
    for g in range(NSA_KV_GROUPS):
        top = jnp.concatenate([q_t[(g * NSA_GROUP_SIZE + r) * hd:(g * NSA_GROUP_SIZE + r + 1) * hd, :]
                               for r in range(NSA_GROUP_SIZE)], axis=1)
        q_ops.append(jnp.concatenate([top, zeros_half] if g == 0 else [zeros_half, top], axis=0))

    n_sub = lax.broadcasted_iota(jnp.int32, (N_CMP_PAD, 1), 0)
    bias_c = jnp.where(n_sub * CMP_STRIDE + (CMP_BLOCK - 1) <= t_row, 0.0, NEG)
    m_sub = lax.broadcasted_iota(jnp.int32, (N_SLC, N_CMP_PAD), 0)
    n_lane = lax.broadcasted_iota(jnp.int32, (N_SLC, N_CMP_PAD), 1)
    c0 = n_lane * CMP_STRIDE
    s0 = m_sub * SLC_BLOCK
    ov_t = ((c0 <= s0 + SLC_BLOCK - 1) & (c0 + CMP_BLOCK - 1 >= s0)
            & (n_lane < seq // CMP_STRIDE - 1)).astype(BF16)
    m_idx = lax.broadcasted_iota(jnp.int32, (N_SLC, Q_BLOCK), 0)
    m_idx_f = m_idx.astype(F32)
    cur = t_row // SLC_BLOCK
    forced = (m_idx == 0) | (m_idx == cur) | (m_idx == cur - 1)
    future = m_idx * SLC_BLOCK > t_row

    def compressed_scores(g):
        return _dot(kc_s[...], q_ops[g]) + _tile4(bias_c)

    def compressed_branch(g, s_c):
        e, inv = _softmax_cols(s_c)
        p_c = e * inv
        o_cg = _dot(vct_s[g * hd:(g + 1) * hd, :], p_c.astype(BF16))
        p_sum = (p_c[:, 0:Q_BLOCK] + p_c[:, Q_BLOCK:2 * Q_BLOCK]
                 + p_c[:, 2 * Q_BLOCK:3 * Q_BLOCK] + p_c[:, 3 * Q_BLOCK:])
        hi, mid, lo = _split3(p_sum)
        imp = _dot(ov_t, hi) + _dot(ov_t, mid) + _dot(ov_t, lo)
        imp = jnp.where(forced, -NEG, jnp.where(future, NEG, imp))
        chosen = jnp.zeros((N_SLC, Q_BLOCK), jnp.bool_)
        for _ in range(SLC_TOPK):
            top = jnp.max(imp, axis=0, keepdims=True)
            first = jnp.min(jnp.where(imp == top, m_idx_f, float(N_SLC)), axis=0, keepdims=True)
            hit = m_idx_f == first
            chosen = chosen | hit
            imp = jnp.where(hit, 2.0 * NEG, imp)
        sel_bias = _tile4(jnp.where(chosen & jnp.logical_not(future), 0.0, NEG)).astype(BF16)
        return o_cg, jnp.concatenate([q_ops[g], sel_bias, jnp.zeros((LANES - N_SLC, GQ), BF16)], axis=0)

    j0 = jnp.maximum(qi - WINDOW // Q_BLOCK, 0)
    w0 = pl.multiple_of(j0 * Q_BLOCK, Q_BLOCK)
    panels = [(g, hp) for g in range(NSA_KV_GROUPS) for hp in range(GQ // PANEL)]
    init = (jnp.full((1, PANEL), NEG, F32), jnp.zeros((VT_ROWS, PANEL), F32)) * len(panels)


    def normalised(state):
        outs = []
        for g in range(NSA_KV_GROUPS):
            accs = [state[2 * panels.index((g, hp)) + 1] for hp in range(GQ // PANEL)]
            outs.append(jnp.concatenate([a[:hd] * (1.0 / a[hd:hd + 1]) for a in accs], axis=1))
        return outs

    def run_items(items, state, hooks=None):
        state = list(state)
        ahead = [it[1]() for it in items[:SCORE_LOOKAHEAD]]
        pending = None

        def flush(pending):
            i, alpha, p, values = pending
            state[2 * i + 1] = alpha * state[2 * i + 1] + _dot(values(), p)

        for n, (i, _, values) in enumerate(items):
            s = ahead.pop(0)
            if n + SCORE_LOOKAHEAD < len(items):
                ahead.append(items[n + SCORE_LOOKAHEAD][1]())
            m_p = state[2 * i]
            m_n = jnp.maximum(m_p, jnp.max(s, axis=0, keepdims=True))
            state[2 * i] = m_n
            p = jnp.exp2(s - m_n).astype(BF16)
            if pending is not None:
                flush(pending)
            pending = (i, jnp.exp2(m_p - m_n), p, values)
            if hooks and n in hooks:
                hooks[n]()
        flush(pending)
        return state

    items = []
    for o, n in ((0, 2 * Q_BLOCK), (2 * Q_BLOCK, 2 * Q_BLOCK), (4 * Q_BLOCK, Q_BLOCK)):
        kp_w = w0 + o + lax.broadcasted_iota(jnp.int32, (n, 1), 0)
        bias_w = jnp.where((kp_w <= t_row) & (kp_w > t_row - WINDOW), 0.0, NEG)
        bias_w = jnp.concatenate([bias_w, bias_w], axis=1)
        for i, (g, hp) in enumerate(panels):
            def score(o=o, n=n, g=g, hp=hp, bias_w=bias_w):
                return _dot(kw_s[pl.ds(w0 + o, n), :], q_ops[g][:, hp * PANEL:(hp + 1) * PANEL]) + bias_w

            def values(o=o, n=n, g=g):
                return jnp.concatenate([vwt_s[j0 + o // Q_BLOCK + j, g] for j in range(n // Q_BLOCK)], axis=1)
            items.append((i, score, values))
    window_items = items

    def chunk_items(c, diagonal, sels):
        items = []
        for sub in range(KEY_CHUNK // PANEL):
            off = pl.multiple_of(c * KEY_CHUNK + sub * PANEL, PANEL)
            causal_bias = None
            if diagonal:
                kpos = off + lax.broadcasted_iota(jnp.int32, (PANEL, 1), 0)
                causal_bias = jnp.where(kpos <= t_row, 0.0, NEG)
                causal_bias = jnp.concatenate([causal_bias, causal_bias], axis=1)
            for i, (g, hp) in enumerate(panels):
                def score(off=off, g=g, hp=hp, causal_bias=causal_bias):
                    s = _dot(ks_s[pl.ds(off, PANEL), :], sels[g][:, hp * PANEL:(hp + 1) * PANEL])
                    return s if causal_bias is None else s + causal_bias

                def values(sub=sub, g=g):
                    return vst_s[c, g, :, sub * PANEL:(sub + 1) * PANEL]
                items.append((len(panels) + i, score, values))
        return items

    s_cs = [compressed_scores(g) for g in range(NSA_KV_GROUPS)]
    cmp_out = {}

    def hook(g):
        return lambda: cmp_out.__setitem__(g, compressed_branch(g, s_cs[g]))

    o_w = normalised(run_items(window_items, init, hooks={1 + 4 * g: hook(g) for g in range(NSA_KV_GROUPS)}))
    o_c, sels = zip(*[cmp_out[g] for g in range(NSA_KV_GROUPS)])

    def sequence(n_past):
        def run():
            past = [it for c in range(n_past) for it in chunk_items(c, False, sels)]
            return tuple(run_items(chunk_items(n_past, True, sels) + past, init + init)[len(init):])
        return run

    o_s = normalised(lax.switch(qi // (KEY_CHUNK // Q_BLOCK), [sequence(k) for k in range(n_chunks)]))

    gates_t = jax.nn.sigmoid(gl_ref[...].astype(F32)).T
    outs = []
    for h in range(NSA_HEADS):
        g, r = divmod(h, NSA_GROUP_SIZE)
        lanes = slice(r * Q_BLOCK, (r + 1) * Q_BLOCK)
        outs.append(gates_t[3 * h:3 * h + 1, :] * o_c[g][:, lanes]
                    + gates_t[3 * h + 1:3 * h + 2, :] * o_s[g][:, lanes]
                    + gates_t[3 * h + 2:3 * h + 3, :] * o_w[g][:, lanes])
    o_ref[...] = jnp.concatenate(outs, axis=0).T.astype(o_ref.dtype)


def _nsa(h, kc, vc, batch, seq):
    nqb = seq // Q_BLOCK
    return pl.pallas_call(
        functools.partial(_nsa_kernel, seq=seq),
        grid=(batch, nqb),
        in_specs=[
            pl.BlockSpec((Q_BLOCK, NSA_OUT), lambda b, i: (b * nqb + i, 0)),
            pl.BlockSpec((N_CMP_PAD, LANES), lambda b, i: (b, 0)),
            pl.BlockSpec((N_CMP_PAD, LANES), lambda b, i: (b, 0)),
            pl.BlockSpec((seq, 2 * NSA_KV_DIM), lambda b, i: (b, 768 // 256)),
            pl.BlockSpec((seq, 2 * NSA_KV_DIM), lambda b, i: (b, 1024 // 256)),
            pl.BlockSpec((Q_BLOCK, LANES), lambda b, i: (b * nqb + i, 2304 // LANES)),
        ],
        out_specs=pl.BlockSpec((Q_BLOCK, NSA_OUT), lambda b, i: (b * nqb + i, 0)),
        out_shape=jax.ShapeDtypeStruct((batch * seq, NSA_OUT), BF16),
        scratch_shapes=[
            pltpu.VMEM((seq, NSA_KV_DIM + LANES), BF16),
            pltpu.VMEM((seq // KEY_CHUNK, NSA_KV_GROUPS, VT_ROWS, KEY_CHUNK), BF16),
            pltpu.VMEM((seq, NSA_KV_DIM), BF16),
            pltpu.VMEM((nqb, NSA_KV_GROUPS, VT_ROWS, Q_BLOCK), BF16),
            pltpu.VMEM((N_CMP_PAD, NSA_KV_DIM), BF16),
            pltpu.VMEM((NSA_KV_DIM, N_CMP_PAD), BF16),
        ],
        compiler_params=pltpu.CompilerParams(
            dimension_semantics=("parallel", "arbitrary"), vmem_limit_bytes=VMEM_LIMIT),
        name="nsa_attn",
    )(h, kc, vc, h, h, h)


def _shift_rows(x, k, row):
    return jnp.where(row >= k, pltpu.roll(x, k, 0), 0.0)


def _pool_kernel(u_ref, w_ref, sc_ref, o_ref, *, seq):
    u = u_ref[...].astype(F32)
    row = lax.broadcasted_iota(jnp.int32, (seq, 1), 0)
    lane = lax.broadcasted_iota(jnp.int32, (1, POOL_DIM), 1)
    tp1 = (row + 1).astype(F32)
    acc = u
    mean = jnp.zeros_like(u)
    span = 1
    for gi, w in enumerate(POOL_WINDOWS):
        while span < w:
            acc = acc + _shift_rows(acc, span, row)
            span *= 2
        cnt = jnp.minimum(float(w), tp1)
        in_group = (lane >= gi * POOL_GROUP_DIM) & (lane < (gi + 1) * POOL_GROUP_DIM)
        mean = jnp.where(in_group, acc / cnt, mean)
    pooled = mean - u
    o_ref[...] = (_dot(pooled.astype(BF16), w_ref[...]) * sc_ref[...]).astype(o_ref.dtype)


def _pool(h, w_bd, scale, batch, seq):
    return pl.pallas_call(
        functools.partial(_pool_kernel, seq=seq),
        grid=(batch,),
        in_specs=[
            pl.BlockSpec((seq, POOL_DIM), lambda b: (b, 1280 // 256)),
            pl.BlockSpec((POOL_DIM, POOL_DIM), lambda b: (0, 0)),
            pl.BlockSpec((1, POOL_DIM), lambda b: (0, 0)),
        ],
        out_specs=pl.BlockSpec((seq, POOL_DIM), lambda b: (b, 0)),
        out_shape=jax.ShapeDtypeStruct((batch * seq, POOL_DIM), BF16),
        compiler_params=pltpu.CompilerParams(
            dimension_semantics=("parallel",), vmem_limit_bytes=VMEM_LIMIT),
        name="pool_mix",
    )(h, w_bd, scale)


def _gla_kernel(q_ref, k_ref, v_ref, a_ref, r_ref, wa2_ref, ba_ref, ng_ref, o_ref,
                qt_ref, kt_ref, kd_ref, dec_ref, oacc_ref, *, seq):
    C, H, DK, DV = GLA_CHUNK, GLA_HEADS, GLA_KEY_DIM, GLA_VAL_DIM
    nc = seq // C
    z = _dot(a_ref[...].astype(BF16), wa2_ref[...]) + ba_ref[...]
    log_a = (jnp.minimum(z, 0.0) - jnp.log(1.0 + jnp.exp(-jnp.abs(z)))) / GLA_TAU
    pos = lax.broadcasted_iota(jnp.int32, (seq, 1), 0) % C
    b = log_a
    step = 1
    while step < C:
        b = b + jnp.where(pos >= step, pltpu.roll(b, step, 0), 0.0)
        step *= 2
    b3 = b.reshape(nc, C, GLA_QK)
    b_last = b3[:, C - 1:C, :]
    k_all = k_ref[...].astype(F32)
    qt_ref[...] = q_ref[...].astype(F32) * (DK ** -0.5) * jnp.exp(b)
    kt_ref[...] = k_all * jnp.exp(-b)
    kd_ref[...] = (k_all.reshape(nc, C, GLA_QK) * jnp.exp(b_last - b3)).reshape(seq, GLA_QK)
    dec_ref[...] = jnp.exp(b_last)

    r_k = lax.broadcasted_iota(jnp.int32, (H * C, GLA_QK), 0) // C
    c_k = lax.broadcasted_iota(jnp.int32, (H * C, GLA_QK), 1) // DK
    mask_k = r_k == c_k
    r_v = lax.broadcasted_iota(jnp.int32, (H * C, GLA_OUT), 0) // C
    c_v = lax.broadcasted_iota(jnp.int32, (H * C, GLA_OUT), 1) // DV
    mask_v = r_v == c_v
    r_s = lax.broadcasted_iota(jnp.int32, (GLA_OUT, GLA_QK), 0) // DV
    c_s = lax.broadcasted_iota(jnp.int32, (GLA_OUT, GLA_QK), 1) // DK
    mask_s = r_s == c_s
    i_a = lax.broadcasted_iota(jnp.int32, (C, H * C), 0)
    j_a = lax.broadcasted_iota(jnp.int32, (C, H * C), 1) % C
    tril = j_a <= i_a

    def chunk(n, state_t):
        rows = pl.ds(pl.multiple_of(n * C, C), C)
        q_t = qt_ref[rows, :].astype(BF16)
        k_t = kt_ref[rows, :]
        k_d = kd_ref[rows, :].astype(BF16)
        v_c = v_ref[rows, :].astype(F32)
        k_bd = jnp.where(mask_k, jnp.concatenate([k_t] * H, axis=0), 0.0).astype(BF16)
        a_cat = jnp.where(tril, _dot_nt(q_t, k_bd), 0.0)
        v_bd = jnp.where(mask_v, jnp.concatenate([v_c] * H, axis=0), 0.0).astype(BF16)
        o_intra = _dot(a_cat.astype(BF16), v_bd)
        o_inter = _dot_nt(q_t, state_t.astype(BF16))
        oacc_ref[rows, :] = o_intra + o_inter
        d_state = jnp.where(mask_s, _dot(v_c.T.astype(BF16), k_d), 0.0)
        return state_t * dec_ref[n] + d_state

    lax.fori_loop(0, nc, chunk, jnp.zeros((GLA_OUT, GLA_QK), F32), unroll=8)

    o = oacc_ref[...]
    gr = lax.broadcasted_iota(jnp.int32, (GLA_OUT, GLA_OUT), 0) // DV
    gc = lax.broadcasted_iota(jnp.int32, (GLA_OUT, GLA_OUT), 1) // DV
    group_mean = jnp.where(gr == gc, 1.0 / DV, 0.0).astype(BF16)
    hi, mid, lo = _split3(o * o)
    ms = _dot(hi, group_mean) + _dot(mid, group_mean) + _dot(lo, group_mean)
    o = o * lax.rsqrt(ms + RMS_EPS) * ng_ref[...]
    r = r_ref[...].astype(F32)
    o_ref[...] = (o * (r * jax.nn.sigmoid(r))).astype(o_ref.dtype)


def _gla(h, wa2_pad, ba, norm_g, batch, seq):
    nc = seq // GLA_CHUNK
    return pl.pallas_call(
        functools.partial(_gla_kernel, seq=seq),
        grid=(batch,),
        in_specs=[
            pl.BlockSpec((seq, GLA_QK), lambda b: (b, 2048 // LANES)),
            pl.BlockSpec((seq, GLA_QK), lambda b: (b, 2176 // LANES)),
            pl.BlockSpec((seq, GLA_OUT), lambda b: (b, 1536 // 256)),
            pl.BlockSpec((seq, LANES), lambda b: (b, 2432 // LANES)),
            pl.BlockSpec((seq, GLA_OUT), lambda b: (b, 1792 // 256)),
            pl.BlockSpec((LANES, GLA_QK), lambda b: (0, 0)),
            pl.BlockSpec((1, GLA_QK), lambda b: (0, 0)),
            pl.BlockSpec((1, GLA_OUT), lambda b: (0, 0)),
        ],
        out_specs=pl.BlockSpec((seq, GLA_OUT), lambda b: (b, 0)),
        out_shape=jax.ShapeDtypeStruct((batch * seq, GLA_OUT), BF16),
        scratch_shapes=[
            pltpu.VMEM((seq, GLA_QK), F32),
            pltpu.VMEM((seq, GLA_QK), F32),
            pltpu.VMEM((seq, GLA_QK), F32),
            pltpu.VMEM((nc, 1, GLA_QK), F32),
            pltpu.VMEM((seq, GLA_OUT), F32),
        ],
        compiler_params=pltpu.CompilerParams(
            dimension_semantics=("parallel",), vmem_limit_bytes=VMEM_LIMIT),
        name="gla_mix",
    )(h, h, h, h, h, wa2_pad, ba, norm_g)


def _outproj_kernel(x_ref, oa_ref, ob_ref, oc_ref, wa_ref, wb_ref, wc_ref, g_ref, b_ref, o_ref):
    m = _dot(oa_ref[...], wa_ref[...]) + _dot(ob_ref[...], wb_ref[...]) + _dot(oc_ref[...], wc_ref[...])
    o_ref[...] = _layer_norm(ALPHA * x_ref[...] + m, g_ref[...], b_ref[...])


def _out_proj_ln(x, o_a, o_b, o_c, w_a, w_b, w_c, g, b, *, tm=1024):
    n = x.shape[0]
    row = lambda i: (i, 0)
    const = lambda i: (0, 0)
    return pl.pallas_call(
        _outproj_kernel,
        grid=(n // tm,),
        in_specs=[
            pl.BlockSpec((tm, D_MODEL), row),
            pl.BlockSpec((tm, NSA_OUT), row),
            pl.BlockSpec((tm, POOL_DIM), row),
            pl.BlockSpec((tm, GLA_OUT), row),
            pl.BlockSpec((NSA_OUT, D_MODEL), const),
            pl.BlockSpec((POOL_DIM, D_MODEL), const),
            pl.BlockSpec((GLA_OUT, D_MODEL), const),
            pl.BlockSpec((1, D_MODEL), const),
            pl.BlockSpec((1, D_MODEL), const),
        ],
        out_specs=pl.BlockSpec((tm, D_MODEL), row),
        out_shape=jax.ShapeDtypeStruct((n, D_MODEL), F32),
        compiler_params=pltpu.CompilerParams(
            dimension_semantics=("parallel",), vmem_limit_bytes=VMEM_LIMIT),
        name="out_proj_ln",
    )(x, o_a, o_b, o_c, w_a, w_b, w_c, g, b)


def _block_diag(blocks):
    n, r, c = blocks.shape[-3:]
    lead = [(0, 0)] * (blocks.ndim - 2)
    rows = [jnp.pad(blocks[..., g, :, :], lead + [(g * c, (n - 1 - g) * c)]) for g in range(n)]
    return jnp.concatenate(rows, axis=-2)


def _prep_compress(cmp_pe, cmp_w1, cmp_w2):
    G, HD = NSA_KV_GROUPS, NSA_HEAD_DIM
    pe = jnp.tile(cmp_pe[:, :, None, :], (1, 1, G, 1)).reshape(2, 1, CMP_BLOCK * G * HD)
    w1 = cmp_w1.reshape(2, CMP_BLOCK, 1, HD, HD).astype(BF16)
    w1_bd = _block_diag(jnp.tile(w1, (1, 1, G, 1, 1))).reshape(2, CMP_BLOCK * G * HD, G * HD)
    w2_bd = _block_diag(jnp.tile(cmp_w2[:, None].astype(BF16), (1, G, 1, 1)))
    return pe, w1_bd, w2_bd


def kernel(x, ln_g, ln_b, ffn_wg, ffn_wu, ffn_wd, w_in, w_out, cmp_pe, cmp_w1, cmp_w2,
           pool_w, pool_scale, gla_wa2, gla_ba, gla_norm_g):
    batch, seq, _ = x.shape
    xf = x.reshape(batch * seq, D_MODEL)
    w_in_t = jnp.swapaxes(w_in, 1, 2)
    for l in range(DEPTH):
        lg = lambda i: ln_g[l, i].reshape(1, D_MODEL)
        lb = lambda i: ln_b[l, i].reshape(1, D_MODEL)
        xf = _ffn_ln(xf, ffn_wg, ffn_wu, ffn_wd, lg(0), lb(0), l, 0)
        h, h_kv = _in_proj(xf, w_in_t, l)
        pe_rows, w1_bd, w2_bd = _prep_compress(cmp_pe[l], cmp_w1[l], cmp_w2[l])
        kc, vc = _compress(h_kv, pe_rows, w1_bd, w2_bd, batch, seq)
        o_a = _nsa(h, kc, vc, batch, seq)
        o_b = _pool(h, _block_diag(pool_w[l]).astype(BF16), pool_scale[l].reshape(1, POOL_DIM), batch, seq)
        wa2_pad = jnp.pad(gla_wa2[l], ((0, LANES - GLA_GATE_RANK), (0, 0))).astype(BF16)
        o_c = _gla(h, wa2_pad, gla_ba[l].reshape(1, GLA_QK), gla_norm_g[l].reshape(1, GLA_OUT), batch, seq)
        wo = w_out[l].astype(BF16)
        xf = _out_proj_ln(xf, o_a, o_b, o_c, wo[:NSA_OUT], wo[NSA_OUT:NSA_OUT + POOL_DIM],
                          wo[NSA_OUT + POOL_DIM:], lg(1), lb(1))
        xf = _ffn_ln(xf, ffn_wg, ffn_wu, ffn_wd, lg(2), lb(2), l, 1)
    return xf.reshape(batch, seq, D_MODEL)
```

```python
import functools

import numpy as np
import jax
import jax.numpy as jnp
from jax import lax
from jax.experimental import pallas as pl
from jax.experimental.pallas import tpu as pltpu

F32 = jnp.float32
BF16 = jnp.bfloat16

D_MODEL = 1024
DEPTH = 4
D_FF = 2816
NSA_HEADS = 8
NSA_KV_GROUPS = 2
NSA_HEAD_DIM = 64
NSA_GROUP_SIZE = NSA_HEADS // NSA_KV_GROUPS
NSA_OUT = NSA_HEADS * NSA_HEAD_DIM
NSA_KV_DIM = NSA_KV_GROUPS * NSA_HEAD_DIM
CMP_BLOCK = 32
CMP_STRIDE = 16
SLC_BLOCK = 64
SLC_TOPK = 8
WINDOW = 512
Q_BLOCK = 128
POOL_WINDOWS = (2, 4, 8, 16)
POOL_GROUPS = 4
POOL_GROUP_DIM = 64
POOL_DIM = POOL_GROUPS * POOL_GROUP_DIM
GLA_HEADS = 4
GLA_KEY_DIM = 32
GLA_VAL_DIM = 64
GLA_GATE_RANK = 16
GLA_TAU = 16.0
GLA_CHUNK = 64
GLA_QK = GLA_HEADS * GLA_KEY_DIM
GLA_OUT = GLA_HEADS * GLA_VAL_DIM
ALPHA = (2.0 * DEPTH) ** 0.25
LN_EPS = 1e-5
RMS_EPS = 1e-6

LANES = 128
NEG = -1e30
VMEM_LIMIT = 56 * 1024 * 1024

_IN_LAYOUT = {
    "qkv": (0, 1280, 0),
    "u": (1304, 256, 1280),
    "gv": (1816, 256, 1536),
    "gr": (2088, 256, 1792),
    "gq": (1560, 128, 2048),
    "gk": (1688, 128, 2176),
    "gl": (1280, 24, 2304),
    "ga": (2072, 16, 2432),
}
D_IN_PAD = 2560


def _dot(a, b):
    return jnp.dot(a, b, preferred_element_type=F32)


def _dot_nt(a, b):
    return lax.dot_general(a, b, (((1,), (1,)), ((), ())), preferred_element_type=F32)


def _layer_norm(y, g, b):
    mu = jnp.mean(y, axis=-1, keepdims=True)
    yc = y - mu
    var = jnp.mean(yc * yc, axis=-1, keepdims=True)
    return yc * lax.rsqrt(var + LN_EPS) * g + b


def _split3(x):
    hi = x.astype(BF16)
    r1 = x - hi.astype(F32)
    mid = r1.astype(BF16)
    lo = (r1 - mid.astype(F32)).astype(BF16)
    return hi, mid, lo


FF_CHUNK = 256


N_FF_CHUNKS = D_FF // FF_CHUNK
FF_UNROLL = 5


def _ffn_kernel(x_ref, wg_hbm, wu_hbm, wd_hbm, g_ref, b_ref, o_ref,
                xb_ref, acc_ref, wg_s, wu_s, wd_s, sg_ref, su_ref, sd_ref, sem, *, layer, which):
    def stage_copies(j, slot):
        cols = pl.ds(j * FF_CHUNK, FF_CHUNK)
        return (pltpu.make_async_copy(wg_hbm.at[layer, which, :, cols], sg_ref.at[slot], sem.at[0, slot]),
                pltpu.make_async_copy(wu_hbm.at[layer, which, :, cols], su_ref.at[slot], sem.at[1, slot]),
                pltpu.make_async_copy(wd_hbm.at[layer, which, cols, :], sd_ref.at[slot], sem.at[2, slot]))

    def start(j, slot):
        for cp in stage_copies(j, slot):
            cp.start()

    def land(j, slot):
        for cp in stage_copies(j, slot):
            cp.wait()
        wg_s[j] = sg_ref[slot].astype(BF16)
        wu_s[j] = su_ref[slot].astype(BF16)
        wd_s[j] = (0.5 * sd_ref[slot]).astype(BF16)

    def contribution(j):
        xb = xb_ref[...]
        gate = _dot(xb, wg_s[j])
        up = _dot(xb, wu_s[j])
        act = (gate * jax.nn.sigmoid(gate)) * up
        return _dot(act.astype(BF16), wd_s[j])

    xb_ref[...] = x_ref[...].astype(BF16)
    first = pl.program_id(0) == 0

    @pl.when(first)
    def _():
        start(0, 0)
        start(1, 1)
        land(0, 0)
        acc_ref[...] = contribution(0)

        def pair(p, carry):
            j = 1 + 2 * p
            start(j + 1, 0)
            land(j, 1)
            acc_ref[...] += contribution(j)
            start(j + 2, 1)
            land(j + 1, 0)
            acc_ref[...] += contribution(j + 1)
            return carry

        n_pairs = (N_FF_CHUNKS - 3) // 2
        lax.fori_loop(0, n_pairs, pair, 0)
        j = 1 + 2 * n_pairs
        start(j + 1, 0)
        land(j, 1)
        acc_ref[...] += contribution(j)
        land(j + 1, 0)
        acc_ref[...] += contribution(j + 1)

    @pl.when(jnp.logical_not(first))
    def _():
        acc_ref[...] = contribution(0)

        def step(j, carry):
            acc_ref[...] += contribution(j)
            return carry

        lax.fori_loop(1, N_FF_CHUNKS, step, 0, unroll=FF_UNROLL)

    y = ALPHA * x_ref[...] + acc_ref[...]
    o_ref[...] = _layer_norm(y, g_ref[...], b_ref[...])


def _ffn_ln(x, wg, wu, wd, g, b, layer, which, *, tm=1024):
    n = x.shape[0]
    return pl.pallas_call(
        functools.partial(_ffn_kernel, layer=layer, which=which),
        grid=(n // tm,),
        in_specs=[
            pl.BlockSpec((tm, D_MODEL), lambda i: (i, 0)),
            pl.BlockSpec(memory_space=pl.ANY),
            pl.BlockSpec(memory_space=pl.ANY),
            pl.BlockSpec(memory_space=pl.ANY),
            pl.BlockSpec((1, D_MODEL), lambda i: (0, 0)),
            pl.BlockSpec((1, D_MODEL), lambda i: (0, 0)),
        ],
        out_specs=pl.BlockSpec((tm, D_MODEL), lambda i: (i, 0)),
        out_shape=jax.ShapeDtypeStruct((n, D_MODEL), F32),
        scratch_shapes=[
            pltpu.VMEM((tm, D_MODEL), BF16),
            pltpu.VMEM((tm, D_MODEL), F32),
            pltpu.VMEM((N_FF_CHUNKS, D_MODEL, FF_CHUNK), BF16),
            pltpu.VMEM((N_FF_CHUNKS, D_MODEL, FF_CHUNK), BF16),
            pltpu.VMEM((N_FF_CHUNKS, FF_CHUNK, D_MODEL), BF16),
            pltpu.VMEM((2, D_MODEL, FF_CHUNK), F32),
            pltpu.VMEM((2, D_MODEL, FF_CHUNK), F32),
            pltpu.VMEM((2, FF_CHUNK, D_MODEL), F32),
            pltpu.SemaphoreType.DMA((3, 2)),
        ],
        compiler_params=pltpu.CompilerParams(
            dimension_semantics=("arbitrary",), vmem_limit_bytes=VMEM_LIMIT),
        name="ffn_ln",
    )(x, wg, wu, wd, g, b)


D_IN = 2344
CMP_COLS = (512, 768)


def _inproj_kernel(x_ref, w_hbm, o_ref, okv_ref, w_s, stage_ref, sem, *, layer):
    @pl.when(pl.program_id(0) == 0)
    def _():
        cp = pltpu.make_async_copy(w_hbm.at[layer], stage_ref, sem.at[0])
        cp.start()
        cp.wait()
        w_s[...] = jnp.zeros_like(w_s)
        for src, width, dst in _IN_LAYOUT.values():
            w_s[dst:dst + width, :] = stage_ref[src:src + width, :].astype(BF16)

    h = _dot_nt(x_ref[...].astype(BF16), w_s[...])
    o_ref[...] = h.astype(o_ref.dtype)
    okv_ref[...] = h[:, CMP_COLS[0]:CMP_COLS[1]]


def _in_proj(x, w_in_t, layer, *, tm=512):
    n = x.shape[0]
    n_cmp = CMP_COLS[1] - CMP_COLS[0]
    return pl.pallas_call(
        functools.partial(_inproj_kernel, layer=layer),
        grid=(n // tm,),
        in_specs=[
            pl.BlockSpec((tm, D_MODEL), lambda i: (i, 0)),
            pl.BlockSpec(memory_space=pl.ANY),
        ],
        out_specs=[pl.BlockSpec((tm, D_IN_PAD), lambda i: (i, 0)),
                   pl.BlockSpec((tm, n_cmp), lambda i: (i, 0))],
        out_shape=[jax.ShapeDtypeStruct((n, D_IN_PAD), BF16), jax.ShapeDtypeStruct((n, n_cmp), F32)],
        scratch_shapes=[
            pltpu.VMEM((D_IN_PAD, D_MODEL), BF16),
            pltpu.VMEM((D_IN, D_MODEL), F32),
            pltpu.SemaphoreType.DMA((1,)),
        ],
        compiler_params=pltpu.CompilerParams(
            dimension_semantics=("arbitrary",), vmem_limit_bytes=VMEM_LIMIT),
        name="in_proj",
    )(x, w_in_t)


N_CMP_PAD = 128


def _gelu_tanh(x):
    return 0.5 * x * (1.0 + jnp.tanh(np.sqrt(2.0 / np.pi) * (x + 0.044715 * (x * x * x))))


def _compress_kernel(zk_ref, zv_ref, pe_ref, w1_ref, w2_ref, ok_ref, ov_ref):
    def one(z_ref, which, o_ref):
        slabs = [z_ref[pl.ds(q, N_CMP_PAD, stride=CMP_STRIDE), :] for q in range(CMP_STRIDE)]
        cat = jnp.concatenate(slabs, axis=1)
        half = CMP_STRIDE * LANES
        top = _dot((cat + pe_ref[which, :, :half]).astype(BF16), w1_ref[which, :half, :])
        bot = _dot((cat + pe_ref[which, :, half:]).astype(BF16), w1_ref[which, half:, :])
        pre = top + pltpu.roll(bot, N_CMP_PAD - 1, 0)
        o_ref[...] = _dot(_gelu_tanh(pre).astype(BF16), w2_ref[which])

    one(zk_ref, 0, ok_ref)
    one(zv_ref, 1, ov_ref)


def _compress(h_kv, pe_rows, w1_bd, w2_bd, batch, seq):
    return pl.pallas_call(
        _compress_kernel,
        grid=(batch,),
        in_specs=[
            pl.BlockSpec((seq, LANES), lambda b: (b, 0)),
            pl.BlockSpec((seq, LANES), lambda b: (b, 1)),
            pl.BlockSpec((2, 1, CMP_BLOCK * LANES), lambda b: (0, 0, 0)),
            pl.BlockSpec((2, CMP_BLOCK * LANES, LANES), lambda b: (0, 0, 0)),
            pl.BlockSpec((2, LANES, LANES), lambda b: (0, 0, 0)),
        ],
        out_specs=[
            pl.BlockSpec((N_CMP_PAD, LANES), lambda b: (b, 0)),
            pl.BlockSpec((N_CMP_PAD, LANES), lambda b: (b, 0)),
        ],
        out_shape=[jax.ShapeDtypeStruct((batch * N_CMP_PAD, LANES), F32)] * 2,
        compiler_params=pltpu.CompilerParams(
            dimension_semantics=("parallel",), vmem_limit_bytes=VMEM_LIMIT),
        name="nsa_compress",
    )(h_kv, h_kv, pe_rows, w1_bd, w2_bd)


N_SLC = 32
WIN_KEYS = WINDOW + Q_BLOCK
KEY_CHUNK = 512
GQ = NSA_GROUP_SIZE * Q_BLOCK
PANEL = 256
SCORE_LOOKAHEAD = 9


VT_ROWS = NSA_HEAD_DIM + 16
LOG2E = 1.4426950408889634


def _softmax_cols(s):
    m = jnp.max(s, axis=0, keepdims=True)
    m = jnp.where(m > 0.5 * NEG, m, 0.0)
    e = jnp.exp2(s - m)
    d = jnp.sum(e, axis=0, keepdims=True)
    inv = 1.0 / jnp.where(d > 0.0, d, 1.0)
    return e, inv


def _tile4(x):
    return jnp.concatenate([x] * NSA_GROUP_SIZE, axis=1)


def _values_t(v_both):
    n = v_both.shape[0]
    v_t = v_both.astype(F32).T
    extra = (lax.broadcasted_iota(jnp.int32, (VT_ROWS - NSA_HEAD_DIM, n), 0) == 0).astype(F32)
    return [jnp.concatenate([v_t[g * NSA_HEAD_DIM:(g + 1) * NSA_HEAD_DIM, :], extra], axis=0).astype(BF16)
            for g in range(NSA_KV_GROUPS)]


def _nsa_kernel(q_ref, kc_ref, vc_ref, ksvs_ref, kwvw_ref, gl_ref, o_ref,
                ks_s, vst_s, kw_s, vwt_s, kc_s, vct_s, *, seq):
    qi = pl.program_id(1)
    start = qi * Q_BLOCK
    hd = NSA_HEAD_DIM
    n_chunks = seq // KEY_CHUNK
    n_qb = seq // Q_BLOCK

    @pl.when(qi == 0)
    def _():
        key_blk = lax.broadcasted_iota(jnp.int32, (seq, LANES), 0) // SLC_BLOCK
        blk = lax.broadcasted_iota(jnp.int32, (seq, LANES), 1)
        ks_s[:, :NSA_KV_DIM] = ksvs_ref[:, :NSA_KV_DIM].astype(BF16)
        ks_s[:, NSA_KV_DIM:] = (key_blk == blk).astype(BF16)
        kw_s[...] = kwvw_ref[:, :NSA_KV_DIM].astype(BF16)
        for c in range(n_chunks):
            for g, v_t in enumerate(_values_t(ksvs_ref[c * KEY_CHUNK:(c + 1) * KEY_CHUNK, NSA_KV_DIM:])):
                vst_s[c, g] = v_t
        for j in range(n_qb):
            for g, v_t in enumerate(_values_t(kwvw_ref[j * Q_BLOCK:(j + 1) * Q_BLOCK, NSA_KV_DIM:])):
                vwt_s[j, g] = v_t
        kc_s[...] = kc_ref[...].astype(BF16)
        vct_s[...] = vc_ref[...].T.astype(BF16)

    t_row = start + lax.broadcasted_iota(jnp.int32, (1, Q_BLOCK), 1)

    q_t = (q_ref[...].astype(F32) * (hd ** -0.5 * LOG2E)).T.astype(BF16)
    zeros_half = jnp.zeros((hd, GQ), BF16)
    q_ops = []
    for g in range(NSA_KV_GROUPS):
        top = jnp.concatenate([q_t[(g * NSA_GROUP_SIZE + r) * hd:(g * NSA_GROUP_SIZE + r + 1) * hd, :]
                               for r in range(NSA_GROUP_SIZE)], axis=1)
        q_ops.append(jnp.concatenate([top, zeros_half] if g == 0 else [zeros_half, top], axis=0))

    n_sub = lax.broadcasted_iota(jnp.int32, (N_CMP_PAD, 1), 0)
    bias_c = jnp.where(n_sub * CMP_STRIDE + (CMP_BLOCK - 1) <= t_row, 0.0, NEG)
    m_sub = lax.broadcasted_iota(jnp.int32, (N_SLC, N_CMP_PAD), 0)
    n_lane = lax.broadcasted_iota(jnp.int32, (N_SLC, N_CMP_PAD), 1)
    c0 = n_lane * CMP_STRIDE
    s0 = m_sub * SLC_BLOCK
    ov_t = ((c0 <= s0 + SLC_BLOCK - 1) & (c0 + CMP_BLOCK - 1 >= s0)
            & (n_lane < seq // CMP_STRIDE - 1)).astype(BF16)
    m_idx = lax.broadcasted_iota(jnp.int32, (N_SLC, Q_BLOCK), 0)
    m_idx_f = m_idx.astype(F32)
    cur = t_row // SLC_BLOCK
    forced = (m_idx == 0) | (m_idx == cur) | (m_idx == cur - 1)
    future = m_idx * SLC_BLOCK > t_row

    def compressed_scores(g):
        return _dot(kc_s[...], q_ops[g]) + _tile4(bias_c)

    def compressed_branch(g, s_c):
        e, inv = _softmax_cols(s_c)
        p_c = e * inv
        o_cg = _dot(vct_s[g * hd:(g + 1) * hd, :], p_c.astype(BF16))
        p_sum = (p_c[:, 0:Q_BLOCK] + p_c[:, Q_BLOCK:2 * Q_BLOCK]
                 + p_c[:, 2 * Q_BLOCK:3 * Q_BLOCK] + p_c[:, 3 * Q_BLOCK:])
        hi, mid, lo = _split3(p_sum)
        imp = _dot(ov_t, hi) + _dot(ov_t, mid) + _dot(ov_t, lo)
        imp = jnp.where(forced, -NEG, jnp.where(future, NEG, imp))
        chosen = jnp.zeros((N_SLC, Q_BLOCK), jnp.bool_)
        for _ in range(SLC_TOPK):
            top = jnp.max(imp, axis=0, keepdims=True)
            first = jnp.min(jnp.where(imp == top, m_idx_f, float(N_SLC)), axis=0, keepdims=True)
            hit = m_idx_f == first
            chosen = chosen | hit
            imp = jnp.where(hit, 2.0 * NEG, imp)
        sel_bias = _tile4(jnp.where(chosen & jnp.logical_not(future), 0.0, NEG)).astype(BF16)
        return o_cg, jnp.concatenate([q_ops[g], sel_bias, jnp.zeros((LANES - N_SLC, GQ), BF16)], axis=0)

    j0 = jnp.maximum(qi - WINDOW // Q_BLOCK, 0)
    w0 = pl.multiple_of(j0 * Q_BLOCK, Q_BLOCK)
    panels = [(g, hp) for g in range(NSA_KV_GROUPS) for hp in range(GQ // PANEL)]
    init = (jnp.full((1, PANEL), NEG, F32), jnp.zeros((VT_ROWS, PANEL), F32)) * len(panels)


    def normalised(state):
        outs = []
        for g in range(NSA_KV_GROUPS):
            accs = [state[2 * panels.index((g, hp)) + 1] for hp in range(GQ // PANEL)]
            outs.append(jnp.concatenate([a[:hd] * (1.0 / a[hd:hd + 1]) for a in accs], axis=1))
        return outs

    def run_items(items, state, hooks=None):
        state = list(state)
        ahead = [it[1]() for it in items[:SCORE_LOOKAHEAD]]
        pending = None

        def flush(pending):
            i, alpha, p, values = pending
            state[2 * i + 1] = alpha * state[2 * i + 1] + _dot(values(), p)

        for n, (i, _, values) in enumerate(items):
            s = ahead.pop(0)
            if n + SCORE_LOOKAHEAD < len(items):
                ahead.append(items[n + SCORE_LOOKAHEAD][1]())
            m_p = state[2 * i]
            m_n = jnp.maximum(m_p, jnp.max(s, axis=0, keepdims=True))
            state[2 * i] = m_n
            p = jnp.exp2(s - m_n).astype(BF16)
            if pending is not None:
                flush(pending)
            pending = (i, jnp.exp2(m_p - m_n), p, values)
            if hooks and n in hooks:
                hooks[n]()
        flush(pending)
        return state

    items = []
    for o, n in ((0, 2 * Q_BLOCK), (2 * Q_BLOCK, 2 * Q_BLOCK), (4 * Q_BLOCK, Q_BLOCK)):
        kp_w = w0 + o + lax.broadcasted_iota(jnp.int32, (n, 1), 0)
        bias_w = jnp.where((kp_w <= t_row) & (kp_w > t_row - WINDOW), 0.0, NEG)
        bias_w = jnp.concatenate([bias_w, bias_w], axis=1)
        for i, (g, hp) in enumerate(panels):
            def score(o=o, n=n, g=g, hp=hp, bias_w=bias_w):
                return _dot(kw_s[pl.ds(w0 + o, n), :], q_ops[g][:, hp * PANEL:(hp + 1) * PANEL]) + bias_w

            def values(o=o, n=n, g=g):
                return jnp.concatenate([vwt_s[j0 + o // Q_BLOCK + j, g] for j in range(n // Q_BLOCK)], axis=1)
            items.append((i, score, values))
    window_items = items

    def key_panel_items(kp, diagonal, sels):
        off = kp * PANEL
        c, sub = divmod(off, KEY_CHUNK)
        causal_bias = None
        if diagonal:
            kpos = off + lax.broadcasted_iota(jnp.int32, (PANEL, 1), 0)
            causal_bias = jnp.where(kpos <= t_row, 0.0, NEG)
            causal_bias = jnp.concatenate([causal_bias, causal_bias], axis=1)
        items = []
        for i, (g, hp) in enumerate(panels):
            def score(g=g, hp=hp):
                s = _dot(ks_s[off:off + PANEL, :], sels[g][:, hp * PANEL:(hp + 1) * PANEL])
                return s if causal_bias is None else s + causal_bias

            def values(g=g):
                return vst_s[c, g, :, sub:sub + PANEL]
            items.append((len(panels) + i, score, values))
        return items

    s_cs = [compressed_scores(g) for g in range(NSA_KV_GROUPS)]
    cmp_out = {}

    def hook(g):
        return lambda: cmp_out.__setitem__(g, compressed_branch(g, s_cs[g]))

    o_w = normalised(run_items(window_items, init, hooks={1 + 4 * g: hook(g) for g in range(NSA_KV_GROUPS)}))
    o_c, sels = zip(*[cmp_out[g] for g in range(NSA_KV_GROUPS)])

    def sequence(n_past):
        def run():
            past = [it for kp in range(n_past) for it in key_panel_items(kp, False, sels)]
            return tuple(run_items(key_panel_items(n_past, True, sels) + past, init + init)[len(init):])
        return run

    o_s = normalised(lax.switch(qi // (PANEL // Q_BLOCK), [sequence(k) for k in range(seq // PANEL)]))

    gates_t = jax.nn.sigmoid(gl_ref[...].astype(F32)).T
    outs = []
    for h in range(NSA_HEADS):
        g, r = divmod(h, NSA_GROUP_SIZE)
        lanes = slice(r * Q_BLOCK, (r + 1) * Q_BLOCK)
        outs.append(gates_t[3 * h:3 * h + 1, :] * o_c[g][:, lanes]
                    + gates_t[3 * h + 1:3 * h + 2, :] * o_s[g][:, lanes]
                    + gates_t[3 * h + 2:3 * h + 3, :] * o_w[g][:, lanes])
    o_ref[...] = jnp.concatenate(outs, axis=0).T.astype(o_ref.dtype)


def _nsa(h, kc, vc, batch, seq):
    nqb = seq // Q_BLOCK
    return pl.pallas_call(
        functools.partial(_nsa_kernel, seq=seq),
        grid=(batch, nqb),
        in_specs=[
            pl.BlockSpec((Q_BLOCK, NSA_OUT), lambda b, i: (b * nqb + i, 0)),
            pl.BlockSpec((N_CMP_PAD, LANES), lambda b, i: (b, 0)),
            pl.BlockSpec((N_CMP_PAD, LANES), lambda b, i: (b, 0)),
            pl.BlockSpec((seq, 2 * NSA_KV_DIM), lambda b, i: (b, 768 // 256)),
            pl.BlockSpec((seq, 2 * NSA_KV_DIM), lambda b, i: (b, 1024 // 256)),
            pl.BlockSpec((Q_BLOCK, LANES), lambda b, i: (b * nqb + i, 2304 // LANES)),
        ],
        out_specs=pl.BlockSpec((Q_BLOCK, NSA_OUT), lambda b, i: (b * nqb + i, 0)),
        out_shape=jax.ShapeDtypeStruct((batch * seq, NSA_OUT), BF16),
        scratch_shapes=[
            pltpu.VMEM((seq, NSA_KV_DIM + LANES), BF16),
            pltpu.VMEM((seq // KEY_CHUNK, NSA_KV_GROUPS, VT_ROWS, KEY_CHUNK), BF16),
            pltpu.VMEM((seq, NSA_KV_DIM), BF16),
            pltpu.VMEM((nqb, NSA_KV_GROUPS, VT_ROWS, Q_BLOCK), BF16),
            pltpu.VMEM((N_CMP_PAD, NSA_KV_DIM), BF16),
            pltpu.VMEM((NSA_KV_DIM, N_CMP_PAD), BF16),
        ],
        compiler_params=pltpu.CompilerParams(
            dimension_semantics=("parallel", "arbitrary"), vmem_limit_bytes=VMEM_LIMIT),
        name="nsa_attn",
    )(h, kc, vc, h, h, h)


def _shift_rows(x, k, row):
    return jnp.where(row >= k, pltpu.roll(x, k, 0), 0.0)


def _pool_kernel(u_ref, w_ref, sc_ref, o_ref, *, seq):
    u = u_ref[...].astype(F32)
    row = lax.broadcasted_iota(jnp.int32, (seq, 1), 0)
    lane = lax.broadcasted_iota(jnp.int32, (1, POOL_DIM), 1)
    tp1 = (row + 1).astype(F32)
    acc = u
    mean = jnp.zeros_like(u)
    span = 1
    for gi, w in enumerate(POOL_WINDOWS):
        while span < w:
            acc = acc + _shift_rows(acc, span, row)
            span *= 2
        cnt = jnp.minimum(float(w), tp1)
        in_group = (lane >= gi * POOL_GROUP_DIM) & (lane < (gi + 1) * POOL_GROUP_DIM)
        mean = jnp.where(in_group, acc / cnt, mean)
    pooled = mean - u
    o_ref[...] = (_dot(pooled.astype(BF16), w_ref[...]) * sc_ref[...]).astype(o_ref.dtype)


def _pool(h, w_bd, scale, batch, seq):
    return pl.pallas_call(
        functools.partial(_pool_kernel, seq=seq),
        grid=(batch,),
        in_specs=[
            pl.BlockSpec((seq, POOL_DIM), lambda b: (b, 1280 // 256)),
            pl.BlockSpec((POOL_DIM, POOL_DIM), lambda b: (0, 0)),
            pl.BlockSpec((1, POOL_DIM), lambda b: (0, 0)),
        ],
        out_specs=pl.BlockSpec((seq, POOL_DIM), lambda b: (b, 0)),
        out_shape=jax.ShapeDtypeStruct((batch * seq, POOL_DIM), BF16),
        compiler_params=pltpu.CompilerParams(
            dimension_semantics=("parallel",), vmem_limit_bytes=VMEM_LIMIT),
        name="pool_mix",
    )(h, w_bd, scale)


def _gla_kernel(q_ref, k_ref, v_ref, a_ref, r_ref, wa2_ref, ba_ref, ng_ref, o_ref,
                qt_ref, kt_ref, kd_ref, dec_ref, oacc_ref, *, seq):
    C, H, DK, DV = GLA_CHUNK, GLA_HEADS, GLA_KEY_DIM, GLA_VAL_DIM
    nc = seq // C
    z = _dot(a_ref[...].astype(BF16), wa2_ref[...]) + ba_ref[...]
    log_a = (jnp.minimum(z, 0.0) - jnp.log(1.0 + jnp.exp(-jnp.abs(z)))) / GLA_TAU
    pos = lax.broadcasted_iota(jnp.int32, (seq, 1), 0) % C
    b = log_a
    step = 1
    while step < C:
        b = b + jnp.where(pos >= step, pltpu.roll(b, step, 0), 0.0)
        step *= 2
    b3 = b.reshape(nc, C, GLA_QK)
    b_last = b3[:, C - 1:C, :]
    k_all = k_ref[...].astype(F32)
    qt_ref[...] = q_ref[...].astype(F32) * (DK ** -0.5) * jnp.exp(b)
    kt_ref[...] = k_all * jnp.exp(-b)
    kd_ref[...] = (k_all.reshape(nc, C, GLA_QK) * jnp.exp(b_last - b3)).reshape(seq, GLA_QK)
    dec_ref[...] = jnp.exp(b_last)

    r_k = lax.broadcasted_iota(jnp.int32, (H * C, GLA_QK), 0) // C
    c_k = lax.broadcasted_iota(jnp.int32, (H * C, GLA_QK), 1) // DK
    mask_k = r_k == c_k
    r_v = lax.broadcasted_iota(jnp.int32, (H * C, GLA_OUT), 0) // C
    c_v = lax.broadcasted_iota(jnp.int32, (H * C, GLA_OUT), 1) // DV
    mask_v = r_v == c_v
    r_s = lax.broadcasted_iota(jnp.int32, (GLA_OUT, GLA_QK), 0) // DV
    c_s = lax.broadcasted_iota(jnp.int32, (GLA_OUT, GLA_QK), 1) // DK
    mask_s = r_s == c_s
    i_a = lax.broadcasted_iota(jnp.int32, (C, H * C), 0)
    j_a = lax.broadcasted_iota(jnp.int32, (C, H * C), 1) % C
    tril = j_a <= i_a

    def chunk(n, state_t):
        rows = pl.ds(pl.multiple_of(n * C, C), C)
        q_t = qt_ref[rows, :].astype(BF16)
        k_t = kt_ref[rows, :]
        k_d = kd_ref[rows, :].astype(BF16)
        v_c = v_ref[rows, :].astype(F32)
        k_bd = jnp.where(mask_k, jnp.concatenate([k_t] * H, axis=0), 0.0).astype(BF16)
        a_cat = jnp.where(tril, _dot_nt(q_t, k_bd), 0.0)
        v_bd = jnp.where(mask_v, jnp.concatenate([v_c] * H, axis=0), 0.0).astype(BF16)
        o_intra = _dot(a_cat.astype(BF16), v_bd)
        o_inter = _dot_nt(q_t, state_t.astype(BF16))
        oacc_ref[rows, :] = o_intra + o_inter
        d_state = jnp.where(mask_s, _dot(v_c.T.astype(BF16), k_d), 0.0)
        return state_t * dec_ref[n] + d_state

    lax.fori_loop(0, nc, chunk, jnp.zeros((GLA_OUT, GLA_QK), F32), unroll=8)

    o = oacc_ref[...]
    gr = lax.broadcasted_iota(jnp.int32, (GLA_OUT, GLA_OUT), 0) // DV
    gc = lax.broadcasted_iota(jnp.int32, (GLA_OUT, GLA_OUT), 1) // DV
    group_mean = jnp.where(gr == gc, 1.0 / DV, 0.0).astype(BF16)
    hi, mid, lo = _split3(o * o)
    ms = _dot(hi, group_mean) + _dot(mid, group_mean) + _dot(lo, group_mean)
    o = o * lax.rsqrt(ms + RMS_EPS) * ng_ref[...]
    r = r_ref[...].astype(F32)
    o_ref[...] = (o * (r * jax.nn.sigmoid(r))).astype(o_ref.dtype)


def _gla(h, wa2_pad, ba, norm_g, batch, seq):
    nc = seq // GLA_CHUNK
    return pl.pallas_call(
        functools.partial(_gla_kernel, seq=seq),
        grid=(batch,),
        in_specs=[
            pl.BlockSpec((seq, GLA_QK), lambda b: (b, 2048 // LANES)),
            pl.BlockSpec((seq, GLA_QK), lambda b: (b, 2176 // LANES)),
            pl.BlockSpec((seq, GLA_OUT), lambda b: (b, 1536 // 256)),
            pl.BlockSpec((seq, LANES), lambda b: (b, 2432 // LANES)),
            pl.BlockSpec((seq, GLA_OUT), lambda b: (b, 1792 // 256)),
            pl.BlockSpec((LANES, GLA_QK), lambda b: (0, 0)),
            pl.BlockSpec((1, GLA_QK), lambda b: (0, 0)),
            pl.BlockSpec((1, GLA_OUT), lambda b: (0, 0)),
        ],
        out_specs=pl.BlockSpec((seq, GLA_OUT), lambda b: (b, 0)),
        out_shape=jax.ShapeDtypeStruct((batch * seq, GLA_OUT), BF16),
        scratch_shapes=[
            pltpu.VMEM((seq, GLA_QK), F32),
            pltpu.VMEM((seq, GLA_QK), F32),
            pltpu.VMEM((seq, GLA_QK), F32),
            pltpu.VMEM((nc, 1, GLA_QK), F32),
            pltpu.VMEM((seq, GLA_OUT), F32),
        ],
        compiler_params=pltpu.CompilerParams(
            dimension_semantics=("parallel",), vmem_limit_bytes=VMEM_LIMIT),
        name="gla_mix",
    )(h, h, h, h, h, wa2_pad, ba, norm_g)


def _outproj_kernel(x_ref, oa_ref, ob_ref, oc_ref, wa_ref, wb_ref, wc_ref, g_ref, b_ref, o_ref):
    m = _dot(oa_ref[...], wa_ref[...]) + _dot(ob_ref[...], wb_ref[...]) + _dot(oc_ref[...], wc_ref[...])
    o_ref[...] = _layer_norm(ALPHA * x_ref[...] + m, g_ref[...], b_ref[...])


def _out_proj_ln(x, o_a, o_b, o_c, w_a, w_b, w_c, g, b, *, tm=1024):
    n = x.shape[0]
    row = lambda i: (i, 0)
    const = lambda i: (0, 0)
    return pl.pallas_call(
        _outproj_kernel,
        grid=(n // tm,),
        in_specs=[
            pl.BlockSpec((tm, D_MODEL), row),
            pl.BlockSpec((tm, NSA_OUT), row),
            pl.BlockSpec((tm, POOL_DIM), row),
            pl.BlockSpec((tm, GLA_OUT), row),
            pl.BlockSpec((NSA_OUT, D_MODEL), const),
            pl.BlockSpec((POOL_DIM, D_MODEL), const),
            pl.BlockSpec((GLA_OUT, D_MODEL), const),
            pl.BlockSpec((1, D_MODEL), const),
            pl.BlockSpec((1, D_MODEL), const),
        ],
        out_specs=pl.BlockSpec((tm, D_MODEL), row),
        out_shape=jax.ShapeDtypeStruct((n, D_MODEL), F32),
        compiler_params=pltpu.CompilerParams(
            dimension_semantics=("parallel",), vmem_limit_bytes=VMEM_LIMIT),
        name="out_proj_ln",
    )(x, o_a, o_b, o_c, w_a, w_b, w_c, g, b)


def _block_diag(blocks):
    n, r, c = blocks.shape[-3:]
    lead = [(0, 0)] * (blocks.ndim - 2)
    rows = [jnp.pad(blocks[..., g, :, :], lead + [(g * c, (n - 1 - g) * c)]) for g in range(n)]
    return jnp.concatenate(rows, axis=-2)


def _prep_compress(cmp_pe, cmp_w1, cmp_w2):
    G, HD = NSA_KV_GROUPS, NSA_HEAD_DIM
    pe = jnp.tile(cmp_pe[:, :, None, :], (1, 1, G, 1)).reshape(2, 1, CMP_BLOCK * G * HD)
    w1 = cmp_w1.reshape(2, CMP_BLOCK, 1, HD, HD).astype(BF16)
    w1_bd = _block_diag(jnp.tile(w1, (1, 1, G, 1, 1))).reshape(2, CMP_BLOCK * G * HD, G * HD)
    w2_bd = _block_diag(jnp.tile(cmp_w2[:, None].astype(BF16), (1, G, 1, 1)))
    return pe, w1_bd, w2_bd


def kernel(x, ln_g, ln_b, ffn_wg, ffn_wu, ffn_wd, w_in, w_out, cmp_pe, cmp_w1, cmp_w2,
           pool_w, pool_scale, gla_wa2, gla_ba, gla_norm_g):
    batch, seq, _ = x.shape
    xf = x.reshape(batch * seq, D_MODEL)
    w_in_t = jnp.swapaxes(w_in, 1, 2)
    for l in range(DEPTH):
        lg = lambda i: ln_g[l, i].reshape(1, D_MODEL)
        lb = lambda i: ln_b[l, i].reshape(1, D_MODEL)
        xf = _ffn_ln(xf, ffn_wg, ffn_wu, ffn_wd, lg(0), lb(0), l, 0)
        h, h_kv = _in_proj(xf, w_in_t, l)
        pe_rows, w1_bd, w2_bd = _prep_compress(cmp_pe[l], cmp_w1[l], cmp_w2[l])
        kc, vc = _compress(h_kv, pe_rows, w1_bd, w2_bd, batch, seq)
        o_a = _nsa(h, kc, vc, batch, seq)
        o_b = _pool(h, _block_diag(pool_w[l]).astype(BF16), pool_scale[l].reshape(1, POOL_DIM), batch, seq)
        wa2_pad = jnp.pad(gla_wa2[l], ((0, LANES - GLA_GATE_RANK), (0, 0))).astype(BF16)
        o_c = _gla(h, wa2_pad, gla_ba[l].reshape(1, GLA_QK), gla_norm_g[l].reshape(1, GLA_OUT), batch, seq)
        wo = w_out[l].astype(BF16)
        xf = _out_proj_ln(xf, o_a, o_b, o_c, wo[:NSA_OUT], wo[NSA_OUT:NSA_OUT + POOL_DIM],
                          wo[NSA_OUT + POOL_DIM:], lg(1), lb(1))
        xf = _ffn_ln(xf, ffn_wg, ffn_wu, ffn_wd, lg(2), lb(2), l, 1)
    return xf.reshape(batch, seq, D_MODEL)
```

```python
import functools

import numpy as np
import jax
import jax.numpy as jnp
from jax import lax
from jax.experimental import pallas as pl
from jax.experimental.pallas import tpu as pltpu

F32 = jnp.float32
BF16 = jnp.bfloat16

D_MODEL = 1024
DEPTH = 4
D_FF = 2816
NSA_HEADS = 8
NSA_KV_GROUPS = 2
NSA_HEAD_DIM = 64
NSA_GROUP_SIZE = NSA_HEADS // NSA_KV_GROUPS
NSA_OUT = NSA_HEADS * NSA_HEAD_DIM
NSA_KV_DIM = NSA_KV_GROUPS * NSA_HEAD_DIM
CMP_BLOCK = 32
CMP_STRIDE = 16
SLC_BLOCK = 64
SLC_TOPK = 8
WINDOW = 512
Q_BLOCK = 128
POOL_WINDOWS = (2, 4, 8, 16)
POOL_GROUPS = 4
POOL_GROUP_DIM = 64
POOL_DIM = POOL_GROUPS * POOL_GROUP_DIM
GLA_HEADS = 4
GLA_KEY_DIM = 32
GLA_VAL_DIM = 64
GLA_GATE_RANK = 16
GLA_TAU = 16.0
GLA_CHUNK = 64
GLA_QK = GLA_HEADS * GLA_KEY_DIM
GLA_OUT = GLA_HEADS * GLA_VAL_DIM
GLA_GROUP = 8
ALPHA = (2.0 * DEPTH) ** 0.25
LN_EPS = 1e-5
RMS_EPS = 1e-6

LANES = 128
NEG = -1e30
VMEM_LIMIT = 56 * 1024 * 1024

_IN_LAYOUT = {
    "qkv": (0, 1280, 0),
    "u": (1304, 256, 1280),
    "gv": (1816, 256, 1536),
    "gr": (2088, 256, 1792),
    "gq": (1560, 128, 2048),
    "gk": (1688, 128, 2176),
    "gl": (1280, 24, 2304),
    "ga": (2072, 16, 2432),
}
D_IN_PAD = 2560


def _dot(a, b):
    return jnp.dot(a, b, preferred_element_type=F32)


def _dot_nt(a, b):
    return lax.dot_general(a, b, (((1,), (1,)), ((), ())), preferred_element_type=F32)


def _layer_norm(y, g, b):
    mu = jnp.mean(y, axis=-1, keepdims=True)
    yc = y - mu
    var = jnp.mean(yc * yc, axis=-1, keepdims=True)
    return yc * lax.rsqrt(var + LN_EPS) * g + b


def _split3(x):
    hi = x.astype(BF16)
    r1 = x - hi.astype(F32)
    mid = r1.astype(BF16)
    lo = (r1 - mid.astype(F32)).astype(BF16)
    return hi, mid, lo


FF_CHUNK = 256


N_FF_CHUNKS = D_FF // FF_CHUNK
FF_UNROLL = 5


def _ffn_kernel(x_ref, wg_hbm, wu_hbm, wd_hbm, g_ref, b_ref, o_ref,
                xb_ref, acc_ref, wg_s, wu_s, wd_s, sg_ref, su_ref, sd_ref, sem, *, layer, which):
    def stage_copies(j, slot):
        cols = pl.ds(j * FF_CHUNK, FF_CHUNK)
        return (pltpu.make_async_copy(wg_hbm.at[layer, which, :, cols], sg_ref.at[slot], sem.at[0, slot]),
                pltpu.make_async_copy(wu_hbm.at[layer, which, :, cols], su_ref.at[slot], sem.at[1, slot]),
                pltpu.make_async_copy(wd_hbm.at[layer, which, cols, :], sd_ref.at[slot], sem.at[2, slot]))

    def start(j, slot):
        for cp in stage_copies(j, slot):
            cp.start()

    def land(j, slot):
        for cp in stage_copies(j, slot):
            cp.wait()
        wg_s[j] = sg_ref[slot].astype(BF16)
        wu_s[j] = su_ref[slot].astype(BF16)
        wd_s[j] = (0.5 * sd_ref[slot]).astype(BF16)

    def contribution(j):
        xb = xb_ref[...]
        gate = _dot(xb, wg_s[j])
        up = _dot(xb, wu_s[j])
        act = (gate * jax.nn.sigmoid(gate)) * up
        return _dot(act.astype(BF16), wd_s[j])

    xb_ref[...] = x_ref[...].astype(BF16)
    first = pl.program_id(0) == 0

    @pl.when(first)
    def _():
        start(0, 0)
        start(1, 1)
        land(0, 0)
        acc_ref[...] = contribution(0)

        def pair(p, carry):
            j = 1 + 2 * p
            start(j + 1, 0)
            land(j, 1)
            acc_ref[...] += contribution(j)
            start(j + 2, 1)
            land(j + 1, 0)
            acc_ref[...] += contribution(j + 1)
            return carry

        n_pairs = (N_FF_CHUNKS - 3) // 2
        lax.fori_loop(0, n_pairs, pair, 0)
        j = 1 + 2 * n_pairs
        start(j + 1, 0)
        land(j, 1)
        acc_ref[...] += contribution(j)
        land(j + 1, 0)
        acc_ref[...] += contribution(j + 1)

    @pl.when(jnp.logical_not(first))
    def _():
        acc_ref[...] = contribution(0)

        def step(j, carry):
            acc_ref[...] += contribution(j)
            return carry

        lax.fori_loop(1, N_FF_CHUNKS, step, 0, unroll=FF_UNROLL)

    y = ALPHA * x_ref[...] + acc_ref[...]
    o_ref[...] = _layer_norm(y, g_ref[...], b_ref[...])


def _ffn_ln(x, wg, wu, wd, g, b, layer, which, *, tm=1024):
    n = x.shape[0]
    return pl.pallas_call(
        functools.partial(_ffn_kernel, layer=layer, which=which),
        grid=(n // tm,),
        in_specs=[
            pl.BlockSpec((tm, D_MODEL), lambda i: (i, 0)),
            pl.BlockSpec(memory_space=pl.ANY),
            pl.BlockSpec(memory_space=pl.ANY),
            pl.BlockSpec(memory_space=pl.ANY),
            pl.BlockSpec((1, D_MODEL), lambda i: (0, 0)),
            pl.BlockSpec((1, D_MODEL), lambda i: (0, 0)),
        ],
        out_specs=pl.BlockSpec((tm, D_MODEL), lambda i: (i, 0)),
        out_shape=jax.ShapeDtypeStruct((n, D_MODEL), F32),
        scratch_shapes=[
            pltpu.VMEM((tm, D_MODEL), BF16),
            pltpu.VMEM((tm, D_MODEL), F32),
            pltpu.VMEM((N_FF_CHUNKS, D_MODEL, FF_CHUNK), BF16),
            pltpu.VMEM((N_FF_CHUNKS, D_MODEL, FF_CHUNK), BF16),
            pltpu.VMEM((N_FF_CHUNKS, FF_CHUNK, D_MODEL), BF16),
            pltpu.VMEM((2, D_MODEL, FF_CHUNK), F32),
            pltpu.VMEM((2, D_MODEL, FF_CHUNK), F32),
            pltpu.VMEM((2, FF_CHUNK, D_MODEL), F32),
            pltpu.SemaphoreType.DMA((3, 2)),
        ],
        compiler_params=pltpu.CompilerParams(
            dimension_semantics=("arbitrary",), vmem_limit_bytes=VMEM_LIMIT),
        name="ffn_ln",
    )(x, wg, wu, wd, g, b)


D_IN = 2344
CMP_COLS = (512, 768)


def _inproj_kernel(x_ref, w_hbm, o_ref, okv_ref, w_s, stage_ref, sem, *, layer):
    @pl.when(pl.program_id(0) == 0)
    def _():
        cp = pltpu.make_async_copy(w_hbm.at[layer], stage_ref, sem.at[0])
        cp.start()
        cp.wait()
        w_s[...] = jnp.zeros_like(w_s)
        for src, width, dst in _IN_LAYOUT.values():
            w_s[dst:dst + width, :] = stage_ref[src:src + width, :].astype(BF16)

    h = _dot_nt(x_ref[...].astype(BF16), w_s[...])
    o_ref[...] = h.astype(o_ref.dtype)
    okv_ref[...] = h[:, CMP_COLS[0]:CMP_COLS[1]]


def _in_proj(x, w_in_t, layer, *, tm=1024):
    n = x.shape[0]
    n_cmp = CMP_COLS[1] - CMP_COLS[0]
    return pl.pallas_call(
        functools.partial(_inproj_kernel, layer=layer),
        grid=(n // tm,),
        in_specs=[
            pl.BlockSpec((tm, D_MODEL), lambda i: (i, 0)),
            pl.BlockSpec(memory_space=pl.ANY),
        ],
        out_specs=[pl.BlockSpec((tm, D_IN_PAD), lambda i: (i, 0)),
                   pl.BlockSpec((tm, n_cmp), lambda i: (i, 0))],
        out_shape=[jax.ShapeDtypeStruct((n, D_IN_PAD), BF16), jax.ShapeDtypeStruct((n, n_cmp), F32)],
        scratch_shapes=[
            pltpu.VMEM((D_IN_PAD, D_MODEL), BF16),
            pltpu.VMEM((D_IN, D_MODEL), F32),
            pltpu.SemaphoreType.DMA((1,)),
        ],
        compiler_params=pltpu.CompilerParams(
            dimension_semantics=("arbitrary",), vmem_limit_bytes=VMEM_LIMIT),
        name="in_proj",
    )(x, w_in_t)


N_CMP_PAD = 128


def _gelu_tanh(x):
    return 0.5 * x * (1.0 + jnp.tanh(np.sqrt(2.0 / np.pi) * (x + 0.044715 * (x * x * x))))


def _compress_kernel(zk_ref, zv_ref, pe_ref, w1_ref, w2_ref, ok_ref, ov_ref):
    def one(z_ref, which, o_ref):
        slabs = [z_ref[pl.ds(q, N_CMP_PAD, stride=CMP_STRIDE), :] for q in range(CMP_STRIDE)]
        cat = jnp.concatenate(slabs, axis=1)
        half = CMP_STRIDE * LANES
        top = _dot((cat + pe_ref[which, :, :half]).astype(BF16), w1_ref[which, :half, :])
        bot = _dot((cat + pe_ref[which, :, half:]).astype(BF16), w1_ref[which, half:, :])
        pre = top + pltpu.roll(bot, N_CMP_PAD - 1, 0)
        o_ref[...] = _dot(_gelu_tanh(pre).astype(BF16), w2_ref[which])

    one(zk_ref, 0, ok_ref)
    one(zv_ref, 1, ov_ref)


def _compress(h_kv, pe_rows, w1_bd, w2_bd, batch, seq):
    return pl.pallas_call(
        _compress_kernel,
        grid=(batch,),
        in_specs=[
            pl.BlockSpec((seq, LANES), lambda b: (b, 0)),
            pl.BlockSpec((seq, LANES), lambda b: (b, 1)),
            pl.BlockSpec((2, 1, CMP_BLOCK * LANES), lambda b: (0, 0, 0)),
            pl.BlockSpec((2, CMP_BLOCK * LANES, LANES), lambda b: (0, 0, 0)),
            pl.BlockSpec((2, LANES, LANES), lambda b: (0, 0, 0)),
        ],
        out_specs=[
            pl.BlockSpec((N_CMP_PAD, LANES), lambda b: (b, 0)),
            pl.BlockSpec((N_CMP_PAD, LANES), lambda b: (b, 0)),
        ],
        out_shape=[jax.ShapeDtypeStruct((batch * N_CMP_PAD, LANES), F32)] * 2,
        compiler_params=pltpu.CompilerParams(
            dimension_semantics=("parallel",), vmem_limit_bytes=VMEM_LIMIT),
        name="nsa_compress",
    )(h_kv, h_kv, pe_rows, w1_bd, w2_bd)


N_SLC = 32
WIN_KEYS = WINDOW + Q_BLOCK
KEY_CHUNK = 512
GQ = NSA_GROUP_SIZE * Q_BLOCK
PANEL = 256
SCORE_LOOKAHEAD = 6


VT_ROWS = NSA_HEAD_DIM + 16
LOG2E = 1.4426950408889634


def _softmax_cols(s):
    m = jnp.max(s, axis=0, keepdims=True)
    m = jnp.where(m > 0.5 * NEG, m, 0.0)
    e = jnp.exp2(s - m)
    d = jnp.sum(e, axis=0, keepdims=True)
    inv = 1.0 / jnp.where(d > 0.0, d, 1.0)
    return e, inv


def _tile4(x):
    return jnp.concatenate([x] * NSA_GROUP_SIZE, axis=1)


def _values_t(v_both):
    n = v_both.shape[0]
    v_t = v_both.astype(F32).T
    extra = (lax.broadcasted_iota(jnp.int32, (VT_ROWS - NSA_HEAD_DIM, n), 0) == 0).astype(F32)
    return [jnp.concatenate([v_t[g * NSA_HEAD_DIM:(g + 1) * NSA_HEAD_DIM, :], extra], axis=0).astype(BF16)
            for g in range(NSA_KV_GROUPS)]


def _nsa_kernel(q_ref, kc_ref, vc_ref, ksvs_ref, kwvw_ref, gl_ref, o_ref,
                ks_s, vst_s, kw_s, vwt_s, kc_s, vct_s, *, seq):
    qi = pl.program_id(1)
    start = qi * Q_BLOCK
    hd = NSA_HEAD_DIM
    n_chunks = seq // KEY_CHUNK
    n_qb = seq // Q_BLOCK

    @pl.when(qi == 0)
    def _():
        key_blk = lax.broadcasted_iota(jnp.int32, (seq, LANES), 0) // SLC_BLOCK
        blk = lax.broadcasted_iota(jnp.int32, (seq, LANES), 1)
        ks_s[:, :NSA_KV_DIM] = ksvs_ref[:, :NSA_KV_DIM].astype(BF16)
        ks_s[:, NSA_KV_DIM:] = (key_blk == blk).astype(BF16)
        kw_s[...] = kwvw_ref[:, :NSA_KV_DIM].astype(BF16)
        for c in range(n_chunks):
            for g, v_t in enumerate(_values_t(ksvs_ref[c * KEY_CHUNK:(c + 1) * KEY_CHUNK, NSA_KV_DIM:])):
                vst_s[c, g] = v_t
        for j in range(n_qb):
            for g, v_t in enumerate(_values_t(kwvw_ref[j * Q_BLOCK:(j + 1) * Q_BLOCK, NSA_KV_DIM:])):
                vwt_s[j, g] = v_t
        kc_s[...] = kc_ref[...].astype(BF16)
        vct_s[...] = vc_ref[...].T.astype(BF16)

    t_row = start + lax.broadcasted_iota(jnp.int32, (1, Q_BLOCK), 1)

    q_t = (q_ref[...].astype(F32) * (hd ** -0.5 * LOG2E)).T.astype(BF16)
    zeros_half = jnp.zeros((hd, GQ), BF16)
    q_ops = []
    for g in range(NSA_KV_GROUPS):
        top = jnp.concatenate([q_t[(g * NSA_GROUP_SIZE + r) * hd:(g * NSA_GROUP_SIZE + r + 1) * hd, :]
                               for r in range(NSA_GROUP_SIZE)], axis=1)
        q_ops.append(jnp.concatenate([top, zeros_half] if g == 0 else [zeros_half, top], axis=0))

    n_sub = lax.broadcasted_iota(jnp.int32, (N_CMP_PAD, 1), 0)
    bias_c = jnp.where(n_sub * CMP_STRIDE + (CMP_BLOCK - 1) <= t_row, 0.0, NEG)
    m_sub = lax.broadcasted_iota(jnp.int32, (N_SLC, N_CMP_PAD), 0)
    n_lane = lax.broadcasted_iota(jnp.int32, (N_SLC, N_CMP_PAD), 1)
    c0 = n_lane * CMP_STRIDE
    s0 = m_sub * SLC_BLOCK
    ov_t = ((c0 <= s0 + SLC_BLOCK - 1) & (c0 + CMP_BLOCK - 1 >= s0)
            & (n_lane < seq // CMP_STRIDE - 1)).astype(BF16)
    m_idx = lax.broadcasted_iota(jnp.int32, (N_SLC, Q_BLOCK), 0)
    m_idx_f = m_idx.astype(F32)
    cur = t_row // SLC_BLOCK
    forced = (m_idx == 0) | (m_idx == cur) | (m_idx == cur - 1)
    future = m_idx * SLC_BLOCK > t_row

    def compressed_scores(g):
        return _dot(kc_s[...], q_ops[g]) + _tile4(bias_c)

    def compressed_branch(g, s_c):
        e, inv = _softmax_cols(s_c)
        p_c = e * inv
        o_cg = _dot(vct_s[g * hd:(g + 1) * hd, :], p_c.astype(BF16))
        p_sum = (p_c[:, 0:Q_BLOCK] + p_c[:, Q_BLOCK:2 * Q_BLOCK]
                 + p_c[:, 2 * Q_BLOCK:3 * Q_BLOCK] + p_c[:, 3 * Q_BLOCK:])
        hi, mid, lo = _split3(p_sum)
        imp = _dot(ov_t, hi) + _dot(ov_t, mid) + _dot(ov_t, lo)
        imp = jnp.where(forced, -NEG, jnp.where(future, NEG, imp))
        chosen = jnp.zeros((N_SLC, Q_BLOCK), jnp.bool_)
        for _ in range(SLC_TOPK):
            top = jnp.max(imp, axis=0, keepdims=True)
            first = jnp.min(jnp.where(imp == top, m_idx_f, float(N_SLC)), axis=0, keepdims=True)
            hit = m_idx_f == first
            chosen = chosen | hit
            imp = jnp.where(hit, 2.0 * NEG, imp)
        sel_bias = _tile4(jnp.where(chosen & jnp.logical_not(future), 0.0, NEG)).astype(BF16)
        return o_cg, jnp.concatenate([q_ops[g], sel_bias, jnp.zeros((LANES - N_SLC, GQ), BF16)], axis=0)

    j0 = jnp.maximum(qi - WINDOW // Q_BLOCK, 0)
    w0 = pl.multiple_of(j0 * Q_BLOCK, Q_BLOCK)
    panels = [(g, hp) for g in range(NSA_KV_GROUPS) for hp in range(GQ // PANEL)]
    init = (jnp.full((1, PANEL), NEG, F32), jnp.zeros((VT_ROWS, PANEL), F32)) * len(panels)


    def normalised(state):
        outs = []
        for g in range(NSA_KV_GROUPS):
            accs = [state[2 * panels.index((g, hp)) + 1] for hp in range(GQ // PANEL)]
            outs.append(jnp.concatenate([a[:hd] * (1.0 / a[hd:hd + 1]) for a in accs], axis=1))
        return outs

    def run_items(items, state, hooks=None):
        state = list(state)
        ahead = [it[1]() for it in items[:SCORE_LOOKAHEAD]]
        pending = None

        def flush(pending):
            i, alpha, p, values = pending
            state[2 * i + 1] = alpha * state[2 * i + 1] + _dot(values(), p)

        for n, (i, _, values) in enumerate(items):
            s = ahead.pop(0)
            if n + SCORE_LOOKAHEAD < len(items):
                ahead.append(items[n + SCORE_LOOKAHEAD][1]())
            m_p = state[2 * i]
            m_n = jnp.maximum(m_p, jnp.max(s, axis=0, keepdims=True))
            state[2 * i] = m_n
            p = jnp.exp2(s - m_n).astype(BF16)
            if pending is not None:
                flush(pending)
            pending = (i, jnp.exp2(m_p - m_n), p, values)
            if hooks and n in hooks:
                hooks[n]()
        flush(pending)
        return state

    items = []
    for o, n in ((0, 2 * Q_BLOCK), (2 * Q_BLOCK, 2 * Q_BLOCK), (4 * Q_BLOCK, Q_BLOCK)):
        kp_w = w0 + o + lax.broadcasted_iota(jnp.int32, (n, 1), 0)
        bias_w = jnp.where((kp_w <= t_row) & (kp_w > t_row - WINDOW), 0.0, NEG)
        bias_w = jnp.concatenate([bias_w, bias_w], axis=1)
        for i, (g, hp) in enumerate(panels):
            def score(o=o, n=n, g=g, hp=hp, bias_w=bias_w):
                return _dot(kw_s[pl.ds(w0 + o, n), :], q_ops[g][:, hp * PANEL:(hp + 1) * PANEL]) + bias_w

            def values(o=o, n=n, g=g):
                return jnp.concatenate([vwt_s[j0 + o // Q_BLOCK + j, g] for j in range(n // Q_BLOCK)], axis=1)
            items.append((i, score, values))
    window_items = items

    def key_panel_items(kp, diagonal, sels):
        off = kp * PANEL
        c, sub = divmod(off, KEY_CHUNK)
        causal_bias = None
        if diagonal:
            kpos = off + lax.broadcasted_iota(jnp.int32, (PANEL, 1), 0)
            causal_bias = jnp.where(kpos <= t_row, 0.0, NEG)
            causal_bias = jnp.concatenate([causal_bias, causal_bias], axis=1)
        items = []
        for i, (g, hp) in enumerate(panels):
            def score(g=g, hp=hp):
                s = _dot(ks_s[off:off + PANEL, :], sels[g][:, hp * PANEL:(hp + 1) * PANEL])
                return s if causal_bias is None else s + causal_bias

            def values(g=g):
                return vst_s[c, g, :, sub:sub + PANEL]
            items.append((len(panels) + i, score, values))
        return items

    s_cs = [compressed_scores(g) for g in range(NSA_KV_GROUPS)]
    cmp_out = {}

    def hook(g):
        return lambda: cmp_out.__setitem__(g, compressed_branch(g, s_cs[g]))

    o_w = normalised(run_items(window_items, init, hooks={1 + 4 * g: hook(g) for g in range(NSA_KV_GROUPS)}))
    o_c, sels = zip(*[cmp_out[g] for g in range(NSA_KV_GROUPS)])

    def sequence(n_past):
        def run():
            past = [it for kp in range(n_past) for it in key_panel_items(kp, False, sels)]
            return tuple(run_items(key_panel_items(n_past, True, sels) + past, init + init)[len(init):])
        return run

    o_s = normalised(lax.switch(qi // (PANEL // Q_BLOCK), [sequence(k) for k in range(seq // PANEL)]))

    gates_t = jax.nn.sigmoid(gl_ref[...].astype(F32)).T
    outs = []
    for h in range(NSA_HEADS):
        g, r = divmod(h, NSA_GROUP_SIZE)
        lanes = slice(r * Q_BLOCK, (r + 1) * Q_BLOCK)
        outs.append(gates_t[3 * h:3 * h + 1, :] * o_c[g][:, lanes]
                    + gates_t[3 * h + 1:3 * h + 2, :] * o_s[g][:, lanes]
                    + gates_t[3 * h + 2:3 * h + 3, :] * o_w[g][:, lanes])
    o_ref[...] = jnp.concatenate(outs, axis=0).T.astype(o_ref.dtype)


def _nsa(h, kc, vc, batch, seq):
    nqb = seq // Q_BLOCK
    return pl.pallas_call(
        functools.partial(_nsa_kernel, seq=seq),
        grid=(batch, nqb),
        in_specs=[
            pl.BlockSpec((Q_BLOCK, NSA_OUT), lambda b, i: (b * nqb + i, 0)),
            pl.BlockSpec((N_CMP_PAD, LANES), lambda b, i: (b, 0)),
            pl.BlockSpec((N_CMP_PAD, LANES), lambda b, i: (b, 0)),
            pl.BlockSpec((seq, 2 * NSA_KV_DIM), lambda b, i: (b, 768 // 256)),
            pl.BlockSpec((seq, 2 * NSA_KV_DIM), lambda b, i: (b, 1024 // 256)),
            pl.BlockSpec((Q_BLOCK, LANES), lambda b, i: (b * nqb + i, 2304 // LANES)),
        ],
        out_specs=pl.BlockSpec((Q_BLOCK, NSA_OUT), lambda b, i: (b * nqb + i, 0)),
        out_shape=jax.ShapeDtypeStruct((batch * seq, NSA_OUT), BF16),
        scratch_shapes=[
            pltpu.VMEM((seq, NSA_KV_DIM + LANES), BF16),
            pltpu.VMEM((seq // KEY_CHUNK, NSA_KV_GROUPS, VT_ROWS, KEY_CHUNK), BF16),
            pltpu.VMEM((seq, NSA_KV_DIM), BF16),
            pltpu.VMEM((nqb, NSA_KV_GROUPS, VT_ROWS, Q_BLOCK), BF16),
            pltpu.VMEM((N_CMP_PAD, NSA_KV_DIM), BF16),
            pltpu.VMEM((NSA_KV_DIM, N_CMP_PAD), BF16),
        ],
        compiler_params=pltpu.CompilerParams(
            dimension_semantics=("parallel", "arbitrary"), vmem_limit_bytes=VMEM_LIMIT),
        name="nsa_attn",
    )(h, kc, vc, h, h, h)


def _shift_rows(x, k, row):
    return jnp.where(row >= k, pltpu.roll(x, k, 0), 0.0)


def _pool_kernel(u_ref, w_ref, sc_ref, o_ref, *, seq):
    u = u_ref[...].astype(F32)
    row = lax.broadcasted_iota(jnp.int32, (seq, 1), 0)
    lane = lax.broadcasted_iota(jnp.int32, (1, POOL_DIM), 1)
    tp1 = (row + 1).astype(F32)
    acc = u
    mean = jnp.zeros_like(u)
    span = 1
    for gi, w in enumerate(POOL_WINDOWS):
        while span < w:
            acc = acc + _shift_rows(acc, span, row)
            span *= 2
        cnt = jnp.minimum(float(w), tp1)
        in_group = (lane >= gi * POOL_GROUP_DIM) & (lane < (gi + 1) * POOL_GROUP_DIM)
        mean = jnp.where(in_group, acc / cnt, mean)
    pooled = mean - u
    o_ref[...] = (_dot(pooled.astype(BF16), w_ref[...]) * sc_ref[...]).astype(o_ref.dtype)


def _pool(h, w_bd, scale, batch, seq):
    return pl.pallas_call(
        functools.partial(_pool_kernel, seq=seq),
        grid=(batch,),
        in_specs=[
            pl.BlockSpec((seq, POOL_DIM), lambda b: (b, 1280 // 256)),
            pl.BlockSpec((POOL_DIM, POOL_DIM), lambda b: (0, 0)),
            pl.BlockSpec((1, POOL_DIM), lambda b: (0, 0)),
        ],
        out_specs=pl.BlockSpec((seq, POOL_DIM), lambda b: (b, 0)),
        out_shape=jax.ShapeDtypeStruct((batch * seq, POOL_DIM), BF16),
        compiler_params=pltpu.CompilerParams(
            dimension_semantics=("parallel",), vmem_limit_bytes=VMEM_LIMIT),
        name="pool_mix",
    )(h, w_bd, scale)


def _gla_kernel(q_ref, k_ref, v_ref, a_ref, r_ref, wa2_ref, ba_ref, ng_ref, o_ref,
                qt_ref, kt_ref, kd_ref, dec_ref, oacc_ref, *, seq):
    C, H, DK, DV = GLA_CHUNK, GLA_HEADS, GLA_KEY_DIM, GLA_VAL_DIM
    nc = seq // C
    z = _dot(a_ref[...].astype(BF16), wa2_ref[...]) + ba_ref[...]
    log_a = (jnp.minimum(z, 0.0) - jnp.log(1.0 + jnp.exp(-jnp.abs(z)))) / GLA_TAU
    pos = lax.broadcasted_iota(jnp.int32, (seq, 1), 0) % C
    b = log_a
    step = 1
    while step < C:
        b = b + jnp.where(pos >= step, pltpu.roll(b, step, 0), 0.0)
        step *= 2
    b3 = b.reshape(nc, C, GLA_QK)
    b_last = b3[:, C - 1:C, :]
    k_all = k_ref[...].astype(F32)
    qt_ref[...] = q_ref[...].astype(F32) * (DK ** -0.5) * jnp.exp(b)
    kt_ref[...] = k_all * jnp.exp(-b)
    kd_ref[...] = (k_all.reshape(nc, C, GLA_QK) * jnp.exp(b_last - b3)).reshape(seq, GLA_QK)
    dec_ref[...] = jnp.exp(b_last)

    r_k = lax.broadcasted_iota(jnp.int32, (H * C, GLA_QK), 0) // C
    c_k = lax.broadcasted_iota(jnp.int32, (H * C, GLA_QK), 1) // DK
    mask_k = r_k == c_k
    r_v = lax.broadcasted_iota(jnp.int32, (H * C, GLA_OUT), 0) // C
    c_v = lax.broadcasted_iota(jnp.int32, (H * C, GLA_OUT), 1) // DV
    mask_v = r_v == c_v
    r_s = lax.broadcasted_iota(jnp.int32, (GLA_OUT, GLA_QK), 0) // DV
    c_s = lax.broadcasted_iota(jnp.int32, (GLA_OUT, GLA_QK), 1) // DK
    mask_s = r_s == c_s
    i_a = lax.broadcasted_iota(jnp.int32, (C, H * C), 0)
    j_a = lax.broadcasted_iota(jnp.int32, (C, H * C), 1) % C
    tril = j_a <= i_a

    def chunk_group(gi, state_t):
        ns = [gi * GLA_GROUP + k for k in range(GLA_GROUP)]
        rows = [pl.ds(pl.multiple_of(n * C, C), C) for n in ns]
        q_ts = [qt_ref[r, :].astype(BF16) for r in rows]
        v_cs = [v_ref[r, :].astype(F32) for r in rows]
        a_cats, d_states = [], []
        for r, q_t in zip(rows, q_ts):
            k_bd = jnp.where(mask_k, jnp.concatenate([kt_ref[r, :]] * H, axis=0), 0.0).astype(BF16)
            a_cats.append(jnp.where(tril, _dot_nt(q_t, k_bd), 0.0))
        for r, v_c in zip(rows, v_cs):
            d_states.append(jnp.where(mask_s, _dot(v_c.T.astype(BF16), kd_ref[r, :].astype(BF16)), 0.0))
        o_inters = []
        for n, q_t, d_state in zip(ns, q_ts, d_states):
            o_inters.append(_dot_nt(q_t, state_t.astype(BF16)))
            state_t = state_t * dec_ref[n] + d_state
        for r, a_cat, v_c, o_inter in zip(rows, a_cats, v_cs, o_inters):
            v_bd = jnp.where(mask_v, jnp.concatenate([v_c] * H, axis=0), 0.0).astype(BF16)
            oacc_ref[r, :] = _dot(a_cat.astype(BF16), v_bd) + o_inter
        return state_t

    lax.fori_loop(0, nc // GLA_GROUP, chunk_group, jnp.zeros((GLA_OUT, GLA_QK), F32))

    o = oacc_ref[...]
    gr = lax.broadcasted_iota(jnp.int32, (GLA_OUT, GLA_OUT), 0) // DV
    gc = lax.broadcasted_iota(jnp.int32, (GLA_OUT, GLA_OUT), 1) // DV
    group_mean = jnp.where(gr == gc, 1.0 / DV, 0.0).astype(BF16)
    hi, mid, lo = _split3(o * o)
    ms = _dot(hi, group_mean) + _dot(mid, group_mean) + _dot(lo, group_mean)
    o = o * lax.rsqrt(ms + RMS_EPS) * ng_ref[...]
    r = r_ref[...].astype(F32)
    o_ref[...] = (o * (r * jax.nn.sigmoid(r))).astype(o_ref.dtype)


def _gla(h, wa2_pad, ba, norm_g, batch, seq):
    nc = seq // GLA_CHUNK
    return pl.pallas_call(
        functools.partial(_gla_kernel, seq=seq),
        grid=(batch,),
        in_specs=[
            pl.BlockSpec((seq, GLA_QK), lambda b: (b, 2048 // LANES)),
            pl.BlockSpec((seq, GLA_QK), lambda b: (b, 2176 // LANES)),
            pl.BlockSpec((seq, GLA_OUT), lambda b: (b, 1536 // 256)),
            pl.BlockSpec((seq, LANES), lambda b: (b, 2432 // LANES)),
            pl.BlockSpec((seq, GLA_OUT), lambda b: (b, 1792 // 256)),
            pl.BlockSpec((LANES, GLA_QK), lambda b: (0, 0)),
            pl.BlockSpec((1, GLA_QK), lambda b: (0, 0)),
            pl.BlockSpec((1, GLA_OUT), lambda b: (0, 0)),
        ],
        out_specs=pl.BlockSpec((seq, GLA_OUT), lambda b: (b, 0)),
        out_shape=jax.ShapeDtypeStruct((batch * seq, GLA_OUT), BF16),
        scratch_shapes=[
            pltpu.VMEM((seq, GLA_QK), F32),
            pltpu.VMEM((seq, GLA_QK), F32),
            pltpu.VMEM((seq, GLA_QK), F32),
            pltpu.VMEM((nc, 1, GLA_QK), F32),
            pltpu.VMEM((seq, GLA_OUT), F32),
        ],
        compiler_params=pltpu.CompilerParams(
            dimension_semantics=("parallel",), vmem_limit_bytes=VMEM_LIMIT),
        name="gla_mix",
    )(h, h, h, h, h, wa2_pad, ba, norm_g)


def _outproj_kernel(x_ref, oa_ref, ob_ref, oc_ref, wa_ref, wb_ref, wc_ref, g_ref, b_ref, o_ref):
    m = _dot(oa_ref[...], wa_ref[...]) + _dot(ob_ref[...], wb_ref[...]) + _dot(oc_ref[...], wc_ref[...])
    o_ref[...] = _layer_norm(ALPHA * x_ref[...] + m, g_ref[...], b_ref[...])


def _out_proj_ln(x, o_a, o_b, o_c, w_a, w_b, w_c, g, b, *, tm=1024):
    n = x.shape[0]
    row = lambda i: (i, 0)
    const = lambda i: (0, 0)
    return pl.pallas_call(
        _outproj_kernel,
        grid=(n // tm,),
        in_specs=[
            pl.BlockSpec((tm, D_MODEL), row),
            pl.BlockSpec((tm, NSA_OUT), row),
            pl.BlockSpec((tm, POOL_DIM), row),
            pl.BlockSpec((tm, GLA_OUT), row),
            pl.BlockSpec((NSA_OUT, D_MODEL), const),
            pl.BlockSpec((POOL_DIM, D_MODEL), const),
            pl.BlockSpec((GLA_OUT, D_MODEL), const),
            pl.BlockSpec((1, D_MODEL), const),
            pl.BlockSpec((1, D_MODEL), const),
        ],
        out_specs=pl.BlockSpec((tm, D_MODEL), row),
        out_shape=jax.ShapeDtypeStruct((n, D_MODEL), F32),
        compiler_params=pltpu.CompilerParams(
            dimension_semantics=("parallel",), vmem_limit_bytes=VMEM_LIMIT),
        name="out_proj_ln",
    )(x, o_a, o_b, o_c, w_a, w_b, w_c, g, b)


def _block_diag(blocks):
    n, r, c = blocks.shape[-3:]
    lead = [(0, 0)] * (blocks.ndim - 2)
    rows = [jnp.pad(blocks[..., g, :, :], lead + [(g * c, (n - 1 - g) * c)]) for g in range(n)]
    return jnp.concatenate(rows, axis=-2)


def _prep_compress(cmp_pe, cmp_w1, cmp_w2):
    G, HD = NSA_KV_GROUPS, NSA_HEAD_DIM
    pe = jnp.tile(cmp_pe[:, :, None, :], (1, 1, G, 1)).reshape(2, 1, CMP_BLOCK * G * HD)
    w1 = cmp_w1.reshape(2, CMP_BLOCK, 1, HD, HD).astype(BF16)
    w1_bd = _block_diag(jnp.tile(w1, (1, 1, G, 1, 1))).reshape(2, CMP_BLOCK * G * HD, G * HD)
    w2_bd = _block_diag(jnp.tile(cmp_w2[:, None].astype(BF16), (1, G, 1, 1)))
    return pe, w1_bd, w2_bd


def kernel(x, ln_g, ln_b, ffn_wg, ffn_wu, ffn_wd, w_in, w_out, cmp_pe, cmp_w1, cmp_w2,
           pool_w, pool_scale, gla_wa2, gla_ba, gla_norm_g):
    batch, seq, _ = x.shape
    xf = x.reshape(batch * seq, D_MODEL)
    w_in_t = jnp.swapaxes(w_in, 1, 2)
    for l in range(DEPTH):
        lg = lambda i: ln_g[l, i].reshape(1, D_MODEL)
        lb = lambda i: ln_b[l, i].reshape(1, D_MODEL)
        xf = _ffn_ln(xf, ffn_wg, ffn_wu, ffn_wd, lg(0), lb(0), l, 0)
        h, h_kv = _in_proj(xf, w_in_t, l)
        pe_rows, w1_bd, w2_bd = _prep_compress(cmp_pe[l], cmp_w1[l], cmp_w2[l])
        kc, vc = _compress(h_kv, pe_rows, w1_bd, w2_bd, batch, seq)
        o_a = _nsa(h, kc, vc, batch, seq)
        o_b = _pool(h, _block_diag(pool_w[l]).astype(BF16), pool_scale[l].reshape(1, POOL_DIM), batch, seq)
        wa2_pad = jnp.pad(gla_wa2[l], ((0, LANES - GLA_GATE_RANK), (0, 0))).astype(BF16)
        o_c = _gla(h, wa2_pad, gla_ba[l].reshape(1, GLA_QK), gla_norm_g[l].reshape(1, GLA_OUT), batch, seq)
        wo = w_out[l].astype(BF16)
        xf = _out_proj_ln(xf, o_a, o_b, o_c, wo[:NSA_OUT], wo[NSA_OUT:NSA_OUT + POOL_DIM],
                          wo[NSA_OUT + POOL_DIM:], lg(1), lb(1))
        xf = _ffn_ln(xf, ffn_wg, ffn_wu, ffn_wd, lg(2), lb(2), l, 1)
    return xf.reshape(batch, seq, D_MODEL)
```

```python
import functools

import numpy as np
import jax
import jax.numpy as jnp
from jax import lax
from jax.experimental import pallas as pl
from jax.experimental.pallas import tpu as pltpu

F32 = jnp.float32
BF16 = jnp.bfloat16

D_MODEL = 1024
DEPTH = 4
D_FF = 2816
NSA_HEADS = 8
NSA_KV_GROUPS = 2
NSA_HEAD_DIM = 64
NSA_GROUP_SIZE = NSA_HEADS // NSA_KV_GROUPS
NSA_OUT = NSA_HEADS * NSA_HEAD_DIM
NSA_KV_DIM = NSA_KV_GROUPS * NSA_HEAD_DIM
CMP_BLOCK = 32
CMP_STRIDE = 16
SLC_BLOCK = 64
SLC_TOPK = 8
WINDOW = 512
Q_BLOCK = 128
POOL_WINDOWS = (2, 4, 8, 16)
POOL_GROUPS = 4
POOL_GROUP_DIM = 64
POOL_DIM = POOL_GROUPS * POOL_GROUP_DIM
GLA_HEADS = 4
GLA_KEY_DIM = 32
GLA_VAL_DIM = 64
GLA_GATE_RANK = 16
GLA_TAU = 16.0
GLA_CHUNK = 64
GLA_QK = GLA_HEADS * GLA_KEY_DIM
GLA_OUT = GLA_HEADS * GLA_VAL_DIM
GLA_GROUP = 16
ALPHA = (2.0 * DEPTH) ** 0.25
LN_EPS = 1e-5
RMS_EPS = 1e-6

LANES = 128
NEG = -1e30
VMEM_LIMIT = 56 * 1024 * 1024

_IN_LAYOUT = {
    "qkv": (0, 1280, 0),
    "u": (1304, 256, 1280),
    "gv": (1816, 256, 1536),
    "gr": (2088, 256, 1792),
    "gq": (1560, 128, 2048),
    "gk": (1688, 128, 2176),
    "gl": (1280, 24, 2304),
    "ga": (2072, 16, 2432),
}
D_IN_PAD = 2560


def _dot(a, b):
    return jnp.dot(a, b, preferred_element_type=F32)


def _dot_nt(a, b):
    return lax.dot_general(a, b, (((1,), (1,)), ((), ())), preferred_element_type=F32)


def _layer_norm(y, g, b):
    mu = jnp.mean(y, axis=-1, keepdims=True)
    yc = y - mu
    var = jnp.mean(yc * yc, axis=-1, keepdims=True)
    return yc * lax.rsqrt(var + LN_EPS) * g + b


def _split3(x):
    hi = x.astype(BF16)
    r1 = x - hi.astype(F32)
    mid = r1.astype(BF16)
    lo = (r1 - mid.astype(F32)).astype(BF16)
    return hi, mid, lo


FF_CHUNK = 256


N_FF_CHUNKS = D_FF // FF_CHUNK
FF_UNROLL = 5


def _ffn_kernel(x_ref, wg_hbm, wu_hbm, wd_hbm, g_ref, b_ref, o_ref,
                xb_ref, acc_ref, wg_s, wu_s, wd_s, sg_ref, su_ref, sd_ref, sem, *, layer, which):
    def stage_copies(j, slot):
        cols = pl.ds(j * FF_CHUNK, FF_CHUNK)
        return (pltpu.make_async_copy(wg_hbm.at[layer, which, :, cols], sg_ref.at[slot], sem.at[0, slot]),
                pltpu.make_async_copy(wu_hbm.at[layer, which, :, cols], su_ref.at[slot], sem.at[1, slot]),
                pltpu.make_async_copy(wd_hbm.at[layer, which, cols, :], sd_ref.at[slot], sem.at[2, slot]))

    def start(j, slot):
        for cp in stage_copies(j, slot):
            cp.start()

    def land(j, slot):
        for cp in stage_copies(j, slot):
            cp.wait()
        wg_s[j] = sg_ref[slot].astype(BF16)
        wu_s[j] = su_ref[slot].astype(BF16)
        wd_s[j] = (0.5 * sd_ref[slot]).astype(BF16)

    def contribution(j):
        xb = xb_ref[...]
        gate = _dot(xb, wg_s[j])
        up = _dot(xb, wu_s[j])
        act = (gate * jax.nn.sigmoid(gate)) * up
        return _dot(act.astype(BF16), wd_s[j])

    xb_ref[...] = x_ref[...].astype(BF16)
    first = pl.program_id(0) == 0

    @pl.when(first)
    def _():
        start(0, 0)
        start(1, 1)
        land(0, 0)
        acc_ref[...] = contribution(0)

        def pair(p, carry):
            j = 1 + 2 * p
            start(j + 1, 0)
            land(j, 1)
            acc_ref[...] += contribution(j)
            start(j + 2, 1)
            land(j + 1, 0)
            acc_ref[...] += contribution(j + 1)
            return carry

        n_pairs = (N_FF_CHUNKS - 3) // 2
        lax.fori_loop(0, n_pairs, pair, 0)
        j = 1 + 2 * n_pairs
        start(j + 1, 0)
        land(j, 1)
        acc_ref[...] += contribution(j)
        land(j + 1, 0)
        acc_ref[...] += contribution(j + 1)

    @pl.when(jnp.logical_not(first))
    def _():
        acc_ref[...] = contribution(0)

        def step(j, carry):
            acc_ref[...] += contribution(j)
            return carry

        lax.fori_loop(1, N_FF_CHUNKS, step, 0, unroll=FF_UNROLL)

    y = ALPHA * x_ref[...] + acc_ref[...]
    o_ref[...] = _layer_norm(y, g_ref[...], b_ref[...])


def _ffn_ln(x, wg, wu, wd, g, b, layer, which, *, tm=1024):
    n = x.shape[0]
    return pl.pallas_call(
        functools.partial(_ffn_kernel, layer=layer, which=which),
        grid=(n // tm,),
        in_specs=[
            pl.BlockSpec((tm, D_MODEL), lambda i: (i, 0)),
            pl.BlockSpec(memory_space=pl.ANY),
            pl.BlockSpec(memory_space=pl.ANY),
            pl.BlockSpec(memory_space=pl.ANY),
            pl.BlockSpec((1, D_MODEL), lambda i: (0, 0)),
            pl.BlockSpec((1, D_MODEL), lambda i: (0, 0)),
        ],
        out_specs=pl.BlockSpec((tm, D_MODEL), lambda i: (i, 0)),
        out_shape=jax.ShapeDtypeStruct((n, D_MODEL), F32),
        scratch_shapes=[
            pltpu.VMEM((tm, D_MODEL), BF16),
            pltpu.VMEM((tm, D_MODEL), F32),
            pltpu.VMEM((N_FF_CHUNKS, D_MODEL, FF_CHUNK), BF16),
            pltpu.VMEM((N_FF_CHUNKS, D_MODEL, FF_CHUNK), BF16),
            pltpu.VMEM((N_FF_CHUNKS, FF_CHUNK, D_MODEL), BF16),
            pltpu.VMEM((2, D_MODEL, FF_CHUNK), F32),
            pltpu.VMEM((2, D_MODEL, FF_CHUNK), F32),
            pltpu.VMEM((2, FF_CHUNK, D_MODEL), F32),
            pltpu.SemaphoreType.DMA((3, 2)),
        ],
        compiler_params=pltpu.CompilerParams(
            dimension_semantics=("arbitrary",), vmem_limit_bytes=VMEM_LIMIT),
        name="ffn_ln",
    )(x, wg, wu, wd, g, b)


D_IN = 2344
CMP_COLS = (512, 768)


def _inproj_kernel(x_ref, w_hbm, o_ref, okv_ref, w_s, stage_ref, sem, *, layer):
    @pl.when(pl.program_id(0) == 0)
    def _():
        cp = pltpu.make_async_copy(w_hbm.at[layer], stage_ref, sem.at[0])
        cp.start()
        cp.wait()
        w_s[...] = jnp.zeros_like(w_s)
        for src, width, dst in _IN_LAYOUT.values():
            w_s[dst:dst + width, :] = stage_ref[src:src + width, :].astype(BF16)

    h = _dot_nt(x_ref[...].astype(BF16), w_s[...])
    o_ref[...] = h.astype(o_ref.dtype)
    okv_ref[...] = h[:, CMP_COLS[0]:CMP_COLS[1]]


def _in_proj(x, w_in_t, layer, *, tm=1024):
    n = x.shape[0]
    n_cmp = CMP_COLS[1] - CMP_COLS[0]
    return pl.pallas_call(
        functools.partial(_inproj_kernel, layer=layer),
        grid=(n // tm,),
        in_specs=[
            pl.BlockSpec((tm, D_MODEL), lambda i: (i, 0)),
            pl.BlockSpec(memory_space=pl.ANY),
        ],
        out_specs=[pl.BlockSpec((tm, D_IN_PAD), lambda i: (i, 0)),
                   pl.BlockSpec((tm, n_cmp), lambda i: (i, 0))],
        out_shape=[jax.ShapeDtypeStruct((n, D_IN_PAD), BF16), jax.ShapeDtypeStruct((n, n_cmp), F32)],
        scratch_shapes=[
            pltpu.VMEM((D_IN_PAD, D_MODEL), BF16),
            pltpu.VMEM((D_IN, D_MODEL), F32),
            pltpu.SemaphoreType.DMA((1,)),
        ],
        compiler_params=pltpu.CompilerParams(
            dimension_semantics=("arbitrary",), vmem_limit_bytes=VMEM_LIMIT),
        name="in_proj",
    )(x, w_in_t)


N_CMP_PAD = 128


def _gelu_tanh(x):
    return 0.5 * x * (1.0 + jnp.tanh(np.sqrt(2.0 / np.pi) * (x + 0.044715 * (x * x * x))))


def _compress_kernel(zk_ref, zv_ref, pe_ref, w1_ref, w2_ref, ok_ref, ov_ref):
    def one(z_ref, which, o_ref):
        slabs = [z_ref[pl.ds(q, N_CMP_PAD, stride=CMP_STRIDE), :] for q in range(CMP_STRIDE)]
        cat = jnp.concatenate(slabs, axis=1)
        half = CMP_STRIDE * LANES
        top = _dot((cat + pe_ref[which, :, :half]).astype(BF16), w1_ref[which, :half, :])
        bot = _dot((cat + pe_ref[which, :, half:]).astype(BF16), w1_ref[which, half:, :])
        pre = top + pltpu.roll(bot, N_CMP_PAD - 1, 0)
        o_ref[...] = _dot(_gelu_tanh(pre).astype(BF16), w2_ref[which])

    one(zk_ref, 0, ok_ref)
    one(zv_ref, 1, ov_ref)


def _compress(h_kv, pe_rows, w1_bd, w2_bd, batch, seq):
    return pl.pallas_call(
        _compress_kernel,
        grid=(batch,),
        in_specs=[
            pl.BlockSpec((seq, LANES), lambda b: (b, 0)),
            pl.BlockSpec((seq, LANES), lambda b: (b, 1)),
            pl.BlockSpec((2, 1, CMP_BLOCK * LANES), lambda b: (0, 0, 0)),
            pl.BlockSpec((2, CMP_BLOCK * LANES, LANES), lambda b: (0, 0, 0)),
            pl.BlockSpec((2, LANES, LANES), lambda b: (0, 0, 0)),
        ],
        out_specs=[
            pl.BlockSpec((N_CMP_PAD, LANES), lambda b: (b, 0)),
            pl.BlockSpec((N_CMP_PAD, LANES), lambda b: (b, 0)),
        ],
        out_shape=[jax.ShapeDtypeStruct((batch * N_CMP_PAD, LANES), F32)] * 2,
        compiler_params=pltpu.CompilerParams(
            dimension_semantics=("parallel",), vmem_limit_bytes=VMEM_LIMIT),
        name="nsa_compress",
    )(h_kv, h_kv, pe_rows, w1_bd, w2_bd)


N_SLC = 32
WIN_KEYS = WINDOW + Q_BLOCK
KEY_CHUNK = 512
GQ = NSA_GROUP_SIZE * Q_BLOCK
PANEL = 256
SCORE_LOOKAHEAD = 9


VT_ROWS = NSA_HEAD_DIM + 16
LOG2E = 1.4426950408889634


def _softmax_cols(s):
    m = jnp.max(s, axis=0, keepdims=True)
    m = jnp.where(m > 0.5 * NEG, m, 0.0)
    e = jnp.exp2(s - m)
    d = jnp.sum(e, axis=0, keepdims=True)
    inv = 1.0 / jnp.where(d > 0.0, d, 1.0)
    return e, inv


def _tile4(x):
    return jnp.concatenate([x] * NSA_GROUP_SIZE, axis=1)


def _values_t(v_both):
    n = v_both.shape[0]
    v_t = v_both.astype(F32).T
    extra = (lax.broadcasted_iota(jnp.int32, (VT_ROWS - NSA_HEAD_DIM, n), 0) == 0).astype(F32)
    return [jnp.concatenate([v_t[g * NSA_HEAD_DIM:(g + 1) * NSA_HEAD_DIM, :], extra], axis=0).astype(BF16)
            for g in range(NSA_KV_GROUPS)]


def _nsa_kernel(q_ref, kc_ref, vc_ref, ksvs_ref, kwvw_ref, gl_ref, o_ref,
                ks_s, vst_s, kw_s, vwt_s, kc_s, vct_s, *, seq):
    qi = pl.program_id(1)
    start = qi * Q_BLOCK
    hd = NSA_HEAD_DIM
    n_chunks = seq // KEY_CHUNK
    n_qb = seq // Q_BLOCK

    @pl.when(qi == 0)
    def _():
        key_blk = lax.broadcasted_iota(jnp.int32, (seq, LANES), 0) // SLC_BLOCK
        blk = lax.broadcasted_iota(jnp.int32, (seq, LANES), 1)
        ks_s[:, :NSA_KV_DIM] = ksvs_ref[:, :NSA_KV_DIM].astype(BF16)
        ks_s[:, NSA_KV_DIM:] = (key_blk == blk).astype(BF16)
        kw_s[...] = kwvw_ref[:, :NSA_KV_DIM].astype(BF16)
        for c in range(n_chunks):
            for g, v_t in enumerate(_values_t(ksvs_ref[c * KEY_CHUNK:(c + 1) * KEY_CHUNK, NSA_KV_DIM:])):
                vst_s[c, g] = v_t
        for j in range(n_qb):
            for g, v_t in enumerate(_values_t(kwvw_ref[j * Q_BLOCK:(j + 1) * Q_BLOCK, NSA_KV_DIM:])):
                vwt_s[j, g] = v_t
        kc_s[...] = kc_ref[...].astype(BF16)
        vct_s[...] = vc_ref[...].T.astype(BF16)

    t_row = start + lax.broadcasted_iota(jnp.int32, (1, Q_BLOCK), 1)

    q_t = (q_ref[...].astype(F32) * (hd ** -0.5 * LOG2E)).T.astype(BF16)
    zeros_half = jnp.zeros((hd, GQ), BF16)
    q_ops = []
    for g in range(NSA_KV_GROUPS):
        top = jnp.concatenate([q_t[(g * NSA_GROUP_SIZE + r) * hd:(g * NSA_GROUP_SIZE + r + 1) * hd, :]
                               for r in range(NSA_GROUP_SIZE)], axis=1)
        q_ops.append(jnp.concatenate([top, zeros_half] if g == 0 else [zeros_half, top], axis=0))

    n_sub = lax.broadcasted_iota(jnp.int32, (N_CMP_PAD, 1), 0)
    bias_c = jnp.where(n_sub * CMP_STRIDE + (CMP_BLOCK - 1) <= t_row, 0.0, NEG)
    m_sub = lax.broadcasted_iota(jnp.int32, (N_SLC, N_CMP_PAD), 0)
    n_lane = lax.broadcasted_iota(jnp.int32, (N_SLC, N_CMP_PAD), 1)
    c0 = n_lane * CMP_STRIDE
    s0 = m_sub * SLC_BLOCK
    ov_t = ((c0 <= s0 + SLC_BLOCK - 1) & (c0 + CMP_BLOCK - 1 >= s0)
            & (n_lane < seq // CMP_STRIDE - 1)).astype(BF16)
    m_idx = lax.broadcasted_iota(jnp.int32, (N_SLC, Q_BLOCK), 0)
    m_idx_f = m_idx.astype(F32)
    cur = t_row // SLC_BLOCK
    forced = (m_idx == 0) | (m_idx == cur) | (m_idx == cur - 1)
    future = m_idx * SLC_BLOCK > t_row

    def compressed_scores(g):
        return _dot(kc_s[...], q_ops[g]) + _tile4(bias_c)

    def compressed_branch(g, s_c):
        e, inv = _softmax_cols(s_c)
        p_c = e * inv
        o_cg = _dot(vct_s[g * hd:(g + 1) * hd, :], p_c.astype(BF16))
        p_sum = (p_c[:, 0:Q_BLOCK] + p_c[:, Q_BLOCK:2 * Q_BLOCK]
                 + p_c[:, 2 * Q_BLOCK:3 * Q_BLOCK] + p_c[:, 3 * Q_BLOCK:])
        hi, mid, lo = _split3(p_sum)
        imp = _dot(ov_t, hi) + _dot(ov_t, mid) + _dot(ov_t, lo)
        imp = jnp.where(forced, -NEG, jnp.where(future, NEG, imp))
        chosen = jnp.zeros((N_SLC, Q_BLOCK), jnp.bool_)
        for _ in range(SLC_TOPK):
            top = jnp.max(imp, axis=0, keepdims=True)
            first = jnp.min(jnp.where(imp == top, m_idx_f, float(N_SLC)), axis=0, keepdims=True)
            hit = m_idx_f == first
            chosen = chosen | hit
            imp = jnp.where(hit, 2.0 * NEG, imp)
        sel_bias = _tile4(jnp.where(chosen & jnp.logical_not(future), 0.0, NEG)).astype(BF16)
        return o_cg, jnp.concatenate([q_ops[g], sel_bias, jnp.zeros((LANES - N_SLC, GQ), BF16)], axis=0)

    j0 = jnp.maximum(qi - WINDOW // Q_BLOCK, 0)
    w0 = pl.multiple_of(j0 * Q_BLOCK, Q_BLOCK)
    panels = [(g, hp) for g in range(NSA_KV_GROUPS) for hp in range(GQ // PANEL)]
    init = (jnp.full((1, PANEL), NEG, F32), jnp.zeros((VT_ROWS, PANEL), F32)) * len(panels)


    def normalised(state):
        outs = []
        for g in range(NSA_KV_GROUPS):
            accs = [state[2 * panels.index((g, hp)) + 1] for hp in range(GQ // PANEL)]
            outs.append(jnp.concatenate([a[:hd] * (1.0 / a[hd:hd + 1]) for a in accs], axis=1))
        return outs

    def run_items(items, state, hooks=None):
        state = list(state)
        ahead = [it[1]() for it in items[:SCORE_LOOKAHEAD]]
        pending = None

        def flush(pending):
            i, alpha, p, values = pending
            state[2 * i + 1] = alpha * state[2 * i + 1] + _dot(values(), p)

        for n, (i, _, values) in enumerate(items):
            s = ahead.pop(0)
            if n + SCORE_LOOKAHEAD < len(items):
                ahead.append(items[n + SCORE_LOOKAHEAD][1]())
            m_p = state[2 * i]
            m_n = jnp.maximum(m_p, jnp.max(s, axis=0, keepdims=True))
            state[2 * i] = m_n
            p = jnp.exp2(s - m_n).astype(BF16)
            if pending is not None:
                flush(pending)
            pending = (i, jnp.exp2(m_p - m_n), p, values)
            if hooks and n in hooks:
                hooks[n]()
        flush(pending)
        return state

    items = []
    for o, n in ((0, 2 * Q_BLOCK), (2 * Q_BLOCK, 2 * Q_BLOCK), (4 * Q_BLOCK, Q_BLOCK)):
        kp_w = w0 + o + lax.broadcasted_iota(jnp.int32, (n, 1), 0)
        bias_w = jnp.where((kp_w <= t_row) & (kp_w > t_row - WINDOW), 0.0, NEG)
        bias_w = jnp.concatenate([bias_w, bias_w], axis=1)
        for i, (g, hp) in enumerate(panels):
            def score(o=o, n=n, g=g, hp=hp, bias_w=bias_w):
                return _dot(kw_s[pl.ds(w0 + o, n), :], q_ops[g][:, hp * PANEL:(hp + 1) * PANEL]) + bias_w

            def values(o=o, n=n, g=g):
                return jnp.concatenate([vwt_s[j0 + o // Q_BLOCK + j, g] for j in range(n // Q_BLOCK)], axis=1)
            items.append((i, score, values))
    window_items = items

    def key_panel_items(kp, diagonal, sels):
        off = kp * PANEL
        c, sub = divmod(off, KEY_CHUNK)
        causal_bias = None
        if diagonal:
            kpos = off + lax.broadcasted_iota(jnp.int32, (PANEL, 1), 0)
            causal_bias = jnp.where(kpos <= t_row, 0.0, NEG)
            causal_bias = jnp.concatenate([causal_bias, causal_bias], axis=1)
        items = []
        for i, (g, hp) in enumerate(panels):
            def score(g=g, hp=hp):
                s = _dot(ks_s[off:off + PANEL, :], sels[g][:, hp * PANEL:(hp + 1) * PANEL])
                return s if causal_bias is None else s + causal_bias

            def values(g=g):
                return vst_s[c, g, :, sub:sub + PANEL]
            items.append((len(panels) + i, score, values))
        return items

    s_cs = [compressed_scores(g) for g in range(NSA_KV_GROUPS)]
    cmp_out = {}

    def hook(g):
        return lambda: cmp_out.__setitem__(g, compressed_branch(g, s_cs[g]))

    o_w = normalised(run_items(window_items, init, hooks={1 + 4 * g: hook(g) for g in range(NSA_KV_GROUPS)}))
    o_c, sels = zip(*[cmp_out[g] for g in range(NSA_KV_GROUPS)])

    def sequence(n_past):
        def run():
            past = [it for kp in range(n_past) for it in key_panel_items(kp, False, sels)]
            return tuple(run_items(key_panel_items(n_past, True, sels) + past, init + init)[len(init):])
        return run

    o_s = normalised(lax.switch(qi // (PANEL // Q_BLOCK), [sequence(k) for k in range(seq // PANEL)]))

    gates_t = jax.nn.sigmoid(gl_ref[...].astype(F32)).T
    outs = []
    for h in range(NSA_HEADS):
        g, r = divmod(h, NSA_GROUP_SIZE)
        lanes = slice(r * Q_BLOCK, (r + 1) * Q_BLOCK)
        outs.append(gates_t[3 * h:3 * h + 1, :] * o_c[g][:, lanes]
                    + gates_t[3 * h + 1:3 * h + 2, :] * o_s[g][:, lanes]
                    + gates_t[3 * h + 2:3 * h + 3, :] * o_w[g][:, lanes])
    o_ref[...] = jnp.concatenate(outs, axis=0).T.astype(o_ref.dtype)


def _nsa(h, kc, vc, batch, seq):
    nqb = seq // Q_BLOCK
    return pl.pallas_call(
        functools.partial(_nsa_kernel, seq=seq),
        grid=(batch, nqb),
        in_specs=[
            pl.BlockSpec((Q_BLOCK, NSA_OUT), lambda b, i: (b * nqb + i, 0)),
            pl.BlockSpec((N_CMP_PAD, LANES), lambda b, i: (b, 0)),
            pl.BlockSpec((N_CMP_PAD, LANES), lambda b, i: (b, 0)),
            pl.BlockSpec((seq, 2 * NSA_KV_DIM), lambda b, i: (b, 768 // 256)),
            pl.BlockSpec((seq, 2 * NSA_KV_DIM), lambda b, i: (b, 1024 // 256)),
            pl.BlockSpec((Q_BLOCK, LANES), lambda b, i: (b * nqb + i, 2304 // LANES)),
        ],
        out_specs=pl.BlockSpec((Q_BLOCK, NSA_OUT), lambda b, i: (b * nqb + i, 0)),
        out_shape=jax.ShapeDtypeStruct((batch * seq, NSA_OUT), BF16),
        scratch_shapes=[
            pltpu.VMEM((seq, NSA_KV_DIM + LANES), BF16),
            pltpu.VMEM((seq // KEY_CHUNK, NSA_KV_GROUPS, VT_ROWS, KEY_CHUNK), BF16),
            pltpu.VMEM((seq, NSA_KV_DIM), BF16),
            pltpu.VMEM((nqb, NSA_KV_GROUPS, VT_ROWS, Q_BLOCK), BF16),
            pltpu.VMEM((N_CMP_PAD, NSA_KV_DIM), BF16),
            pltpu.VMEM((NSA_KV_DIM, N_CMP_PAD), BF16),
        ],
        compiler_params=pltpu.CompilerParams(
            dimension_semantics=("parallel", "arbitrary"), vmem_limit_bytes=VMEM_LIMIT),
        name="nsa_attn",
    )(h, kc, vc, h, h, h)


def _shift_rows(x, k, row):
    return jnp.where(row >= k, pltpu.roll(x, k, 0), 0.0)


def _pool_kernel(u_ref, w_ref, sc_ref, o_ref, *, seq):
    u = u_ref[...].astype(F32)
    row = lax.broadcasted_iota(jnp.int32, (seq, 1), 0)
    lane = lax.broadcasted_iota(jnp.int32, (1, POOL_DIM), 1)
    tp1 = (row + 1).astype(F32)
    acc = u
    mean = jnp.zeros_like(u)
    span = 1
    for gi, w in enumerate(POOL_WINDOWS):
        while span < w:
            acc = acc + _shift_rows(acc, span, row)
            span *= 2
        cnt = jnp.minimum(float(w), tp1)
        in_group = (lane >= gi * POOL_GROUP_DIM) & (lane < (gi + 1) * POOL_GROUP_DIM)
        mean = jnp.where(in_group, acc / cnt, mean)
    pooled = mean - u
    o_ref[...] = (_dot(pooled.astype(BF16), w_ref[...]) * sc_ref[...]).astype(o_ref.dtype)


def _pool(h, w_bd, scale, batch, seq):
    return pl.pallas_call(
        functools.partial(_pool_kernel, seq=seq),
        grid=(batch,),
        in_specs=[
            pl.BlockSpec((seq, POOL_DIM), lambda b: (b, 1280 // 256)),
            pl.BlockSpec((POOL_DIM, POOL_DIM), lambda b: (0, 0)),
            pl.BlockSpec((1, POOL_DIM), lambda b: (0, 0)),
        ],
        out_specs=pl.BlockSpec((seq, POOL_DIM), lambda b: (b, 0)),
        out_shape=jax.ShapeDtypeStruct((batch * seq, POOL_DIM), BF16),
        compiler_params=pltpu.CompilerParams(
            dimension_semantics=("parallel",), vmem_limit_bytes=VMEM_LIMIT),
        name="pool_mix",
    )(h, w_bd, scale)


def _gla_kernel(q_ref, k_ref, v_ref, a_ref, r_ref, wa2_ref, ba_ref, ng_ref, o_ref,
                qt_ref, kt_ref, kd_ref, dec_ref, oacc_ref, *, seq):
    C, H, DK, DV = GLA_CHUNK, GLA_HEADS, GLA_KEY_DIM, GLA_VAL_DIM
    nc = seq // C
    z = _dot(a_ref[...].astype(BF16), wa2_ref[...]) + ba_ref[...]
    log_a = (jnp.minimum(z, 0.0) - jnp.log(1.0 + jnp.exp(-jnp.abs(z)))) / GLA_TAU
    pos = lax.broadcasted_iota(jnp.int32, (seq, 1), 0) % C
    b = log_a
    step = 1
    while step < C:
        b = b + jnp.where(pos >= step, pltpu.roll(b, step, 0), 0.0)
        step *= 2
    b3 = b.reshape(nc, C, GLA_QK)
    b_last = b3[:, C - 1:C, :]
    k_all = k_ref[...].astype(F32)
    qt_ref[...] = q_ref[...].astype(F32) * (DK ** -0.5) * jnp.exp(b)
    kt_ref[...] = k_all * jnp.exp(-b)
    kd_ref[...] = (k_all.reshape(nc, C, GLA_QK) * jnp.exp(b_last - b3)).reshape(seq, GLA_QK)
    dec_ref[...] = jnp.exp(b_last)

    r_k = lax.broadcasted_iota(jnp.int32, (H * C, GLA_QK), 0) // C
    c_k = lax.broadcasted_iota(jnp.int32, (H * C, GLA_QK), 1) // DK
    mask_k = r_k == c_k
    r_v = lax.broadcasted_iota(jnp.int32, (H * C, GLA_OUT), 0) // C
    c_v = lax.broadcasted_iota(jnp.int32, (H * C, GLA_OUT), 1) // DV
    mask_v = r_v == c_v
    r_s = lax.broadcasted_iota(jnp.int32, (GLA_OUT, GLA_QK), 0) // DV
    c_s = lax.broadcasted_iota(jnp.int32, (GLA_OUT, GLA_QK), 1) // DK
    mask_s = r_s == c_s
    i_a = lax.broadcasted_iota(jnp.int32, (C, H * C), 0)
    j_a = lax.broadcasted_iota(jnp.int32, (C, H * C), 1) % C
    tril = j_a <= i_a

    def chunk_group(gi, state_t):
        ns = [gi * GLA_GROUP + k for k in range(GLA_GROUP)]
        rows = [pl.ds(pl.multiple_of(n * C, C), C) for n in ns]
        q_ts = [qt_ref[r, :].astype(BF16) for r in rows]
        v_cs = [v_ref[r, :].astype(F32) for r in rows]
        a_cats, d_states = [], []
        for r, q_t in zip(rows, q_ts):
            k_bd = jnp.where(mask_k, jnp.concatenate([kt_ref[r, :]] * H, axis=0), 0.0).astype(BF16)
            a_cats.append(jnp.where(tril, _dot_nt(q_t, k_bd), 0.0))
        for r, v_c in zip(rows, v_cs):
            d_states.append(jnp.where(mask_s, _dot(v_c.T.astype(BF16), kd_ref[r, :].astype(BF16)), 0.0))
        o_inters = []
        for n, q_t, d_state in zip(ns, q_ts, d_states):
            o_inters.append(_dot_nt(q_t, state_t.astype(BF16)))
            state_t = state_t * dec_ref[n] + d_state
        for r, a_cat, v_c, o_inter in zip(rows, a_cats, v_cs, o_inters):
            v_bd = jnp.where(mask_v, jnp.concatenate([v_c] * H, axis=0), 0.0).astype(BF16)
            oacc_ref[r, :] = _dot(a_cat.astype(BF16), v_bd) + o_inter
        return state_t

    lax.fori_loop(0, nc // GLA_GROUP, chunk_group, jnp.zeros((GLA_OUT, GLA_QK), F32))

    o = oacc_ref[...]
    gr = lax.broadcasted_iota(jnp.int32, (GLA_OUT, GLA_OUT), 0) // DV
    gc = lax.broadcasted_iota(jnp.int32, (GLA_OUT, GLA_OUT), 1) // DV
    group_mean = jnp.where(gr == gc, 1.0 / DV, 0.0).astype(BF16)
    hi, mid, lo = _split3(o * o)
    ms = _dot(hi, group_mean) + _dot(mid, group_mean) + _dot(lo, group_mean)
    o = o * lax.rsqrt(ms + RMS_EPS) * ng_ref[...]
    r = r_ref[...].astype(F32)
    o_ref[...] = (o * (r * jax.nn.sigmoid(r))).astype(o_ref.dtype)


def _gla(h, wa2_pad, ba, norm_g, batch, seq):
    nc = seq // GLA_CHUNK
    return pl.pallas_call(
        functools.partial(_gla_kernel, seq=seq),
        grid=(batch,),
        in_specs=[
            pl.BlockSpec((seq, GLA_QK), lambda b: (b, 2048 // LANES)),
            pl.BlockSpec((seq, GLA_QK), lambda b: (b, 2176 // LANES)),
            pl.BlockSpec((seq, GLA_OUT), lambda b: (b, 1536 // 256)),
            pl.BlockSpec((seq, LANES), lambda b: (b, 2432 // LANES)),
            pl.BlockSpec((seq, GLA_OUT), lambda b: (b, 1792 // 256)),
            pl.BlockSpec((LANES, GLA_QK), lambda b: (0, 0)),
            pl.BlockSpec((1, GLA_QK), lambda b: (0, 0)),
            pl.BlockSpec((1, GLA_OUT), lambda b: (0, 0)),
        ],
        out_specs=pl.BlockSpec((seq, GLA_OUT), lambda b: (b, 0)),
        out_shape=jax.ShapeDtypeStruct((batch * seq, GLA_OUT), BF16),
        scratch_shapes=[
            pltpu.VMEM((seq, GLA_QK), F32),
            pltpu.VMEM((seq, GLA_QK), F32),
            pltpu.VMEM((seq, GLA_QK), F32),
            pltpu.VMEM((nc, 1, GLA_QK), F32),
            pltpu.VMEM((seq, GLA_OUT), F32),
        ],
        compiler_params=pltpu.CompilerParams(
            dimension_semantics=("parallel",), vmem_limit_bytes=VMEM_LIMIT),
        name="gla_mix",
    )(h, h, h, h, h, wa2_pad, ba, norm_g)


OUT_SUBTILES = 4


def _outproj_kernel(x_ref, oa_ref, ob_ref, oc_ref, wa_ref, wb_ref, wc_ref, g_ref, b_ref, o_ref):
    ts = x_ref.shape[0] // OUT_SUBTILES

    def mixed(s):
        rows = pl.ds(s * ts, ts)
        return (_dot(oa_ref[rows, :], wa_ref[...]) + _dot(ob_ref[rows, :], wb_ref[...])
                + _dot(oc_ref[rows, :], wc_ref[...]))

    m_next = mixed(0)
    for s in range(OUT_SUBTILES):
        m = m_next
        if s + 1 < OUT_SUBTILES:
            m_next = mixed(s + 1)
        rows = pl.ds(s * ts, ts)
        o_ref[rows, :] = _layer_norm(ALPHA * x_ref[rows, :] + m, g_ref[...], b_ref[...])


def _out_proj_ln(x, o_a, o_b, o_c, w_a, w_b, w_c, g, b, *, tm=1024):
    n = x.shape[0]
    row = lambda i: (i, 0)
    const = lambda i: (0, 0)
    return pl.pallas_call(
        _outproj_kernel,
        grid=(n // tm,),
        in_specs=[
            pl.BlockSpec((tm, D_MODEL), row),
            pl.BlockSpec((tm, NSA_OUT), row),
            pl.BlockSpec((tm, POOL_DIM), row),
            pl.BlockSpec((tm, GLA_OUT), row),
            pl.BlockSpec((NSA_OUT, D_MODEL), const),
            pl.BlockSpec((POOL_DIM, D_MODEL), const),
            pl.BlockSpec((GLA_OUT, D_MODEL), const),
            pl.BlockSpec((1, D_MODEL), const),
            pl.BlockSpec((1, D_MODEL), const),
        ],
        out_specs=pl.BlockSpec((tm, D_MODEL), row),
        out_shape=jax.ShapeDtypeStruct((n, D_MODEL), F32),
        compiler_params=pltpu.CompilerParams(
            dimension_semantics=("parallel",), vmem_limit_bytes=VMEM_LIMIT),
        name="out_proj_ln",
    )(x, o_a, o_b, o_c, w_a, w_b, w_c, g, b)


def _block_diag(blocks):
    n, r, c = blocks.shape[-3:]
    lead = [(0, 0)] * (blocks.ndim - 2)
    rows = [jnp.pad(blocks[..., g, :, :], lead + [(g * c, (n - 1 - g) * c)]) for g in range(n)]
    return jnp.concatenate(rows, axis=-2)


def _prep_compress(cmp_pe, cmp_w1, cmp_w2):
    G, HD = NSA_KV_GROUPS, NSA_HEAD_DIM
    pe = jnp.tile(cmp_pe[:, :, None, :], (1, 1, G, 1)).reshape(2, 1, CMP_BLOCK * G * HD)
    w1 = cmp_w1.reshape(2, CMP_BLOCK, 1, HD, HD).astype(BF16)
    w1_bd = _block_diag(jnp.tile(w1, (1, 1, G, 1, 1))).reshape(2, CMP_BLOCK * G * HD, G * HD)
    w2_bd = _block_diag(jnp.tile(cmp_w2[:, None].astype(BF16), (1, G, 1, 1)))
    return pe, w1_bd, w2_bd


def kernel(x, ln_g, ln_b, ffn_wg, ffn_wu, ffn_wd, w_in, w_out, cmp_pe, cmp_w1, cmp_w2,
           pool_w, pool_scale, gla_wa2, gla_ba, gla_norm_g):
    batch, seq, _ = x.shape
    xf = x.reshape(batch * seq, D_MODEL)
    w_in_t = jnp.swapaxes(w_in, 1, 2)
    for l in range(DEPTH):
        lg = lambda i: ln_g[l, i].reshape(1, D_MODEL)
        lb = lambda i: ln_b[l, i].reshape(1, D_MODEL)
        xf = _ffn_ln(xf, ffn_wg, ffn_wu, ffn_wd, lg(0), lb(0), l, 0)
        h, h_kv = _in_proj(xf, w_in_t, l)
        pe_rows, w1_bd, w2_bd = _prep_compress(cmp_pe[l], cmp_w1[l], cmp_w2[l])
        kc, vc = _compress(h_kv, pe_rows, w1_bd, w2_bd, batch, seq)
        o_a = _nsa(h, kc, vc, batch, seq)
        o_b = _pool(h, _block_diag(pool_w[l]).astype(BF16), pool_scale[l].reshape(1, POOL_DIM), batch, seq)
        wa2_pad = jnp.pad(gla_wa2[l], ((0, LANES - GLA_GATE_RANK), (0, 0))).astype(BF16)
        o_c = _gla(h, wa2_pad, gla_ba[l].reshape(1, GLA_QK), gla_norm_g[l].reshape(1, GLA_OUT), batch, seq)
        wo = w_out[l].astype(BF16)
        xf = _out_proj_ln(xf, o_a, o_b, o_c, wo[:NSA_OUT], wo[NSA_OUT:NSA_OUT + POOL_DIM],
                          wo[NSA_OUT + POOL_DIM:], lg(1), lb(1))
        xf = _ffn_ln(xf, ffn_wg, ffn_wu, ffn_wd, lg(2), lb(2), l, 1)
    return xf.reshape(batch, seq, D_MODEL)
```

```python
import functools

import numpy as np
import jax
import jax.numpy as jnp
from jax import lax
from jax.experimental import pallas as pl
from jax.experimental.pallas import tpu as pltpu

F32 = jnp.float32
BF16 = jnp.bfloat16

D_MODEL = 1024
DEPTH = 4
D_FF = 2816
NSA_HEADS = 8
NSA_KV_GROUPS = 2
NSA_HEAD_DIM = 64
NSA_GROUP_SIZE = NSA_HEADS // NSA_KV_GROUPS
NSA_OUT = NSA_HEADS * NSA_HEAD_DIM
NSA_KV_DIM = NSA_KV_GROUPS * NSA_HEAD_DIM
CMP_BLOCK = 32
CMP_STRIDE = 16
SLC_BLOCK = 64
SLC_TOPK = 8
WINDOW = 512
Q_BLOCK = 128
POOL_WINDOWS = (2, 4, 8, 16)
POOL_GROUPS = 4
POOL_GROUP_DIM = 64
POOL_DIM = POOL_GROUPS * POOL_GROUP_DIM
GLA_HEADS = 4
GLA_KEY_DIM = 32
GLA_VAL_DIM = 64
GLA_GATE_RANK = 16
GLA_TAU = 16.0
GLA_CHUNK = 64
GLA_QK = GLA_HEADS * GLA_KEY_DIM
GLA_OUT = GLA_HEADS * GLA_VAL_DIM
GLA_GROUP = 16
ALPHA = (2.0 * DEPTH) ** 0.25
LN_EPS = 1e-5
RMS_EPS = 1e-6

LANES = 128
NEG = -1e30
VMEM_LIMIT = 56 * 1024 * 1024

_IN_LAYOUT = {
    "qkv": (0, 1280, 0),
    "u": (1304, 256, 1280),
    "gv": (1816, 256, 1536),
    "gr": (2088, 256, 1792),
    "gq": (1560, 128, 2048),
    "gk": (1688, 128, 2176),
    "gl": (1280, 24, 2304),
    "ga": (2072, 16, 2432),
}
D_IN_PAD = 2560


def _dot(a, b):
    return jnp.dot(a, b, preferred_element_type=F32)


def _dot_nt(a, b):
    return lax.dot_general(a, b, (((1,), (1,)), ((), ())), preferred_element_type=F32)


def _layer_norm(y, g, b):
    mu = jnp.mean(y, axis=-1, keepdims=True)
    yc = y - mu
    var = jnp.mean(yc * yc, axis=-1, keepdims=True)
    return yc * lax.rsqrt(var + LN_EPS) * g + b


def _split3(x):
    hi = x.astype(BF16)
    r1 = x - hi.astype(F32)
    mid = r1.astype(BF16)
    lo = (r1 - mid.astype(F32)).astype(BF16)
    return hi, mid, lo


FF_CHUNK = 256


N_FF_CHUNKS = D_FF // FF_CHUNK
FF_UNROLL = 3
FF_TAIL_SUBTILES = 4


def _ffn_kernel(x_ref, wg_hbm, wu_hbm, wd_hbm, g_ref, b_ref, o_ref,
                xb_ref, acc_ref, wg_s, wu_s, wd_s, sg_ref, su_ref, sd_ref, sem, *, layer, which):
    def stage_copies(j, slot):
        cols = pl.ds(j * FF_CHUNK, FF_CHUNK)
        return (pltpu.make_async_copy(wg_hbm.at[layer, which, :, cols], sg_ref.at[slot], sem.at[0, slot]),
                pltpu.make_async_copy(wu_hbm.at[layer, which, :, cols], su_ref.at[slot], sem.at[1, slot]),
                pltpu.make_async_copy(wd_hbm.at[layer, which, cols, :], sd_ref.at[slot], sem.at[2, slot]))

    def start(j, slot):
        for cp in stage_copies(j, slot):
            cp.start()

    def land(j, slot):
        for cp in stage_copies(j, slot):
            cp.wait()
        wg_s[j] = sg_ref[slot].astype(BF16)
        wu_s[j] = su_ref[slot].astype(BF16)
        wd_s[j] = (0.5 * sd_ref[slot]).astype(BF16)

    def contribution(j):
        xb = xb_ref[...]
        gate = _dot(xb, wg_s[j])
        up = _dot(xb, wu_s[j])
        act = (gate * jax.nn.sigmoid(gate)) * up
        return _dot(act.astype(BF16), wd_s[j])

    xb_ref[...] = x_ref[...].astype(BF16)
    first = pl.program_id(0) == 0

    @pl.when(first)
    def _():
        start(0, 0)
        start(1, 1)
        land(0, 0)
        acc_ref[...] = contribution(0)

        def pair(p, carry):
            j = 1 + 2 * p
            start(j + 1, 0)
            land(j, 1)
            acc_ref[...] += contribution(j)
            start(j + 2, 1)
            land(j + 1, 0)
            acc_ref[...] += contribution(j + 1)
            return carry

        n_pairs = (N_FF_CHUNKS - 3) // 2
        lax.fori_loop(0, n_pairs, pair, 0)
        j = 1 + 2 * n_pairs
        start(j + 1, 0)
        land(j, 1)
        acc_ref[...] += contribution(j)
        land(j + 1, 0)

    @pl.when(jnp.logical_not(first))
    def _():
        acc_ref[...] = contribution(0)

        def step(j, carry):
            acc_ref[...] += contribution(j)
            return carry

        lax.fori_loop(1, N_FF_CHUNKS - 1, step, 0, unroll=FF_UNROLL)

    ts = x_ref.shape[0] // FF_TAIL_SUBTILES
    last = N_FF_CHUNKS - 1

    def tail(s):
        rows = pl.ds(s * ts, ts)
        xb = xb_ref[rows, :]
        gate = _dot(xb, wg_s[last])
        up = _dot(xb, wu_s[last])
        act = (gate * jax.nn.sigmoid(gate)) * up
        return acc_ref[rows, :] + _dot(act.astype(BF16), wd_s[last])

    total_next = tail(0)
    for s in range(FF_TAIL_SUBTILES):
        total = total_next
        if s + 1 < FF_TAIL_SUBTILES:
            total_next = tail(s + 1)
        rows = pl.ds(s * ts, ts)
        o_ref[rows, :] = _layer_norm(ALPHA * x_ref[rows, :] + total, g_ref[...], b_ref[...])


def _ffn_ln(x, wg, wu, wd, g, b, layer, which, *, tm=1024):
    n = x.shape[0]
    return pl.pallas_call(
        functools.partial(_ffn_kernel, layer=layer, which=which),
        grid=(n // tm,),
        in_specs=[
            pl.BlockSpec((tm, D_MODEL), lambda i: (i, 0)),
            pl.BlockSpec(memory_space=pl.ANY),
            pl.BlockSpec(memory_space=pl.ANY),
            pl.BlockSpec(memory_space=pl.ANY),
            pl.BlockSpec((1, D_MODEL), lambda i: (0, 0)),
            pl.BlockSpec((1, D_MODEL), lambda i: (0, 0)),
        ],
        out_specs=pl.BlockSpec((tm, D_MODEL), lambda i: (i, 0)),
        out_shape=jax.ShapeDtypeStruct((n, D_MODEL), F32),
        scratch_shapes=[
            pltpu.VMEM((tm, D_MODEL), BF16),
            pltpu.VMEM((tm, D_MODEL), F32),
            pltpu.VMEM((N_FF_CHUNKS, D_MODEL, FF_CHUNK), BF16),
            pltpu.VMEM((N_FF_CHUNKS, D_MODEL, FF_CHUNK), BF16),
            pltpu.VMEM((N_FF_CHUNKS, FF_CHUNK, D_MODEL), BF16),
            pltpu.VMEM((2, D_MODEL, FF_CHUNK), F32),
            pltpu.VMEM((2, D_MODEL, FF_CHUNK), F32),
            pltpu.VMEM((2, FF_CHUNK, D_MODEL), F32),
            pltpu.SemaphoreType.DMA((3, 2)),
        ],
        compiler_params=pltpu.CompilerParams(
            dimension_semantics=("arbitrary",), vmem_limit_bytes=VMEM_LIMIT),
        name="ffn_ln",
    )(x, wg, wu, wd, g, b)


D_IN = 2344
CMP_COLS = (512, 768)


def _inproj_kernel(x_ref, w_hbm, o_ref, okv_ref, w_s, stage_ref, sem, *, layer):
    @pl.when(pl.program_id(0) == 0)
    def _():
        cp = pltpu.make_async_copy(w_hbm.at[layer], stage_ref, sem.at[0])
        cp.start()
        cp.wait()
        w_s[...] = jnp.zeros_like(w_s)
        for src, width, dst in _IN_LAYOUT.values():
            w_s[dst:dst + width, :] = stage_ref[src:src + width, :].astype(BF16)

    h = _dot_nt(x_ref[...].astype(BF16), w_s[...])
    o_ref[...] = h.astype(o_ref.dtype)
    okv_ref[...] = h[:, CMP_COLS[0]:CMP_COLS[1]]


def _in_proj(x, w_in_t, layer, *, tm=1024):
    n = x.shape[0]
    n_cmp = CMP_COLS[1] - CMP_COLS[0]
    return pl.pallas_call(
        functools.partial(_inproj_kernel, layer=layer),
        grid=(n // tm,),
        in_specs=[
            pl.BlockSpec((tm, D_MODEL), lambda i: (i, 0)),
            pl.BlockSpec(memory_space=pl.ANY),
        ],
        out_specs=[pl.BlockSpec((tm, D_IN_PAD), lambda i: (i, 0)),
                   pl.BlockSpec((tm, n_cmp), lambda i: (i, 0))],
        out_shape=[jax.ShapeDtypeStruct((n, D_IN_PAD), BF16), jax.ShapeDtypeStruct((n, n_cmp), F32)],
        scratch_shapes=[
            pltpu.VMEM((D_IN_PAD, D_MODEL), BF16),
            pltpu.VMEM((D_IN, D_MODEL), F32),
            pltpu.SemaphoreType.DMA((1,)),
        ],
        compiler_params=pltpu.CompilerParams(
            dimension_semantics=("arbitrary",), vmem_limit_bytes=VMEM_LIMIT),
        name="in_proj",
    )(x, w_in_t)


N_CMP_PAD = 128


def _gelu_tanh(x):
    return 0.5 * x * (1.0 + jnp.tanh(np.sqrt(2.0 / np.pi) * (x + 0.044715 * (x * x * x))))


def _compress_kernel(zk_ref, zv_ref, pe_ref, w1_ref, w2_ref, ok_ref, ov_ref):
    def one(z_ref, which, o_ref):
        slabs = [z_ref[pl.ds(q, N_CMP_PAD, stride=CMP_STRIDE), :] for q in range(CMP_STRIDE)]
        cat = jnp.concatenate(slabs, axis=1)
        half = CMP_STRIDE * LANES
        top = _dot((cat + pe_ref[which, :, :half]).astype(BF16), w1_ref[which, :half, :])
        bot = _dot((cat + pe_ref[which, :, half:]).astype(BF16), w1_ref[which, half:, :])
        pre = top + pltpu.roll(bot, N_CMP_PAD - 1, 0)
        o_ref[...] = _dot(_gelu_tanh(pre).astype(BF16), w2_ref[which])

    one(zk_ref, 0, ok_ref)
    one(zv_ref, 1, ov_ref)


def _compress(h_kv, pe_rows, w1_bd, w2_bd, batch, seq):
    return pl.pallas_call(
        _compress_kernel,
        grid=(batch,),
        in_specs=[
            pl.BlockSpec((seq, LANES), lambda b: (b, 0)),
            pl.BlockSpec((seq, LANES), lambda b: (b, 1)),
            pl.BlockSpec((2, 1, CMP_BLOCK * LANES), lambda b: (0, 0, 0)),
            pl.BlockSpec((2, CMP_BLOCK * LANES, LANES), lambda b: (0, 0, 0)),
            pl.BlockSpec((2, LANES, LANES), lambda b: (0, 0, 0)),
        ],
        out_specs=[
            pl.BlockSpec((N_CMP_PAD, LANES), lambda b: (b, 0)),
            pl.BlockSpec((N_CMP_PAD, LANES), lambda b: (b, 0)),
        ],
        out_shape=[jax.ShapeDtypeStruct((batch * N_CMP_PAD, LANES), F32)] * 2,
        compiler_params=pltpu.CompilerParams(
            dimension_semantics=("parallel",), vmem_limit_bytes=VMEM_LIMIT),
        name="nsa_compress",
    )(h_kv, h_kv, pe_rows, w1_bd, w2_bd)


N_SLC = 32
WIN_KEYS = WINDOW + Q_BLOCK
KEY_CHUNK = 512
GQ = NSA_GROUP_SIZE * Q_BLOCK
PANEL = 256
SCORE_LOOKAHEAD = 9


VT_ROWS = NSA_HEAD_DIM + 16
LOG2E = 1.4426950408889634


def _softmax_cols(s):
    m = jnp.max(s, axis=0, keepdims=True)
    m = jnp.where(m > 0.5 * NEG, m, 0.0)
    e = jnp.exp2(s - m)
    d = jnp.sum(e, axis=0, keepdims=True)
    inv = 1.0 / jnp.where(d > 0.0, d, 1.0)
    return e, inv


def _tile4(x):
    return jnp.concatenate([x] * NSA_GROUP_SIZE, axis=1)


def _values_t(v_both):
    n = v_both.shape[0]
    v_t = v_both.astype(F32).T
    extra = (lax.broadcasted_iota(jnp.int32, (VT_ROWS - NSA_HEAD_DIM, n), 0) == 0).astype(F32)
    return [jnp.concatenate([v_t[g * NSA_HEAD_DIM:(g + 1) * NSA_HEAD_DIM, :], extra], axis=0).astype(BF16)
            for g in range(NSA_KV_GROUPS)]


def _nsa_kernel(q_ref, kc_ref, vc_ref, ksvs_ref, kwvw_ref, gl_ref, o_ref,
                ks_s, vst_s, kw_s, vwt_s, kc_s, vct_s, *, seq):
    qi = pl.program_id(1)
    start = qi * Q_BLOCK
    hd = NSA_HEAD_DIM
    n_chunks = seq // KEY_CHUNK
    n_qb = seq // Q_BLOCK

    @pl.when(qi == 0)
    def _():
        key_blk = lax.broadcasted_iota(jnp.int32, (seq, LANES), 0) // SLC_BLOCK
        blk = lax.broadcasted_iota(jnp.int32, (seq, LANES), 1)
        ks_s[:, :NSA_KV_DIM] = ksvs_ref[:, :NSA_KV_DIM].astype(BF16)
        ks_s[:, NSA_KV_DIM:] = (key_blk == blk).astype(BF16)
        kw_s[...] = kwvw_ref[:, :NSA_KV_DIM].astype(BF16)
        for c in range(n_chunks):
            for g, v_t in enumerate(_values_t(ksvs_ref[c * KEY_CHUNK:(c + 1) * KEY_CHUNK, NSA_KV_DIM:])):
                vst_s[c, g] = v_t
        for j in range(n_qb):
            for g, v_t in enumerate(_values_t(kwvw_ref[j * Q_BLOCK:(j + 1) * Q_BLOCK, NSA_KV_DIM:])):
                vwt_s[j, g] = v_t
        kc_s[...] = kc_ref[...].astype(BF16)
        vct_s[...] = vc_ref[...].T.astype(BF16)

    t_row = start + lax.broadcasted_iota(jnp.int32, (1, Q_BLOCK), 1)

    q_t = (q_ref[...].astype(F32) * (hd ** -0.5 * LOG2E)).T.astype(BF16)
    zeros_half = jnp.zeros((hd, GQ), BF16)
    q_ops = []
    for g in range(NSA_KV_GROUPS):
        top = jnp.concatenate([q_t[(g * NSA_GROUP_SIZE + r) * hd:(g * NSA_GROUP_SIZE + r + 1) * hd, :]
                               for r in range(NSA_GROUP_SIZE)], axis=1)
        q_ops.append(jnp.concatenate([top, zeros_half] if g == 0 else [zeros_half, top], axis=0))

    n_sub = lax.broadcasted_iota(jnp.int32, (N_CMP_PAD, 1), 0)
    bias_c = jnp.where(n_sub * CMP_STRIDE + (CMP_BLOCK - 1) <= t_row, 0.0, NEG)
    m_sub = lax.broadcasted_iota(jnp.int32, (N_SLC, N_CMP_PAD), 0)
    n_lane = lax.broadcasted_iota(jnp.int32, (N_SLC, N_CMP_PAD), 1)
    c0 = n_lane * CMP_STRIDE
    s0 = m_sub * SLC_BLOCK
    ov_t = ((c0 <= s0 + SLC_BLOCK - 1) & (c0 + CMP_BLOCK - 1 >= s0)
            & (n_lane < seq // CMP_STRIDE - 1)).astype(BF16)
    m_idx = lax.broadcasted_iota(jnp.int32, (N_SLC, Q_BLOCK), 0)
    m_idx_f = m_idx.astype(F32)
    cur = t_row // SLC_BLOCK
    forced = (m_idx == 0) | (m_idx == cur) | (m_idx == cur - 1)
    future = m_idx * SLC_BLOCK > t_row

    def compressed_scores(g):
        return _dot(kc_s[...], q_ops[g]) + _tile4(bias_c)

    def compressed_branch(g, s_c):
        e, inv = _softmax_cols(s_c)
        p_c = e * inv
        o_cg = _dot(vct_s[g * hd:(g + 1) * hd, :], p_c.astype(BF16))
        p_sum = (p_c[:, 0:Q_BLOCK] + p_c[:, Q_BLOCK:2 * Q_BLOCK]
                 + p_c[:, 2 * Q_BLOCK:3 * Q_BLOCK] + p_c[:, 3 * Q_BLOCK:])
        hi, mid, lo = _split3(p_sum)
        imp = _dot(ov_t, hi) + _dot(ov_t, mid) + _dot(ov_t, lo)
        imp = jnp.where(forced, -NEG, jnp.where(future, NEG, imp))
        chosen = jnp.zeros((N_SLC, Q_BLOCK), jnp.bool_)
        for _ in range(SLC_TOPK):
            top = jnp.max(imp, axis=0, keepdims=True)
            first = jnp.min(jnp.where(imp == top, m_idx_f, float(N_SLC)), axis=0, keepdims=True)
            hit = m_idx_f == first
            chosen = chosen | hit
            imp = jnp.where(hit, 2.0 * NEG, imp)
        sel_bias = _tile4(jnp.where(chosen & jnp.logical_not(future), 0.0, NEG)).astype(BF16)
        return o_cg, jnp.concatenate([q_ops[g], sel_bias, jnp.zeros((LANES - N_SLC, GQ), BF16)], axis=0)

    j0 = jnp.maximum(qi - WINDOW // Q_BLOCK, 0)
    w0 = pl.multiple_of(j0 * Q_BLOCK, Q_BLOCK)
    panels = [(g, hp) for g in range(NSA_KV_GROUPS) for hp in range(GQ // PANEL)]
    init = (jnp.full((1, PANEL), NEG, F32), jnp.zeros((VT_ROWS, PANEL), F32)) * len(panels)


    def normalised(state):
        outs = []
        for g in range(NSA_KV_GROUPS):
            accs = [state[2 * panels.index((g, hp)) + 1] for hp in range(GQ // PANEL)]
            outs.append(jnp.concatenate([a[:hd] * (1.0 / a[hd:hd + 1]) for a in accs], axis=1))
        return outs

    def run_items(items, state, hooks=None):
        state = list(state)
        ahead = [it[1]() for it in items[:SCORE_LOOKAHEAD]]
        pending = None

        def flush(pending):
            i, alpha, p, values = pending
            state[2 * i + 1] = alpha * state[2 * i + 1] + _dot(values(), p)

        for n, (i, _, values) in enumerate(items):
            s = ahead.pop(0)
            if n + SCORE_LOOKAHEAD < len(items):
                ahead.append(items[n + SCORE_LOOKAHEAD][1]())
            m_p = state[2 * i]
            m_n = jnp.maximum(m_p, jnp.max(s, axis=0, keepdims=True))
            state[2 * i] = m_n
            p = jnp.exp2(s - m_n).astype(BF16)
            if pending is not None:
                flush(pending)
            pending = (i, jnp.exp2(m_p - m_n), p, values)
            if hooks and n in hooks:
                hooks[n]()
        flush(pending)
        return state

    items = []
    for o, n in ((0, 2 * Q_BLOCK), (2 * Q_BLOCK, 2 * Q_BLOCK), (4 * Q_BLOCK, Q_BLOCK)):
        kp_w = w0 + o + lax.broadcasted_iota(jnp.int32, (n, 1), 0)
        bias_w = jnp.where((kp_w <= t_row) & (kp_w > t_row - WINDOW), 0.0, NEG)
        bias_w = jnp.concatenate([bias_w, bias_w], axis=1)
        for i, (g, hp) in enumerate(panels):
            def score(o=o, n=n, g=g, hp=hp, bias_w=bias_w):
                return _dot(kw_s[pl.ds(w0 + o, n), :], q_ops[g][:, hp * PANEL:(hp + 1) * PANEL]) + bias_w

            def values(o=o, n=n, g=g):
                return jnp.concatenate([vwt_s[j0 + o // Q_BLOCK + j, g] for j in range(n // Q_BLOCK)], axis=1)
            items.append((i, score, values))
    window_items = items

    def key_panel_items(kp, diagonal, sels):
        off = kp * PANEL
        c, sub = divmod(off, KEY_CHUNK)
        causal_bias = None
        if diagonal:
            kpos = off + lax.broadcasted_iota(jnp.int32, (PANEL, 1), 0)
            causal_bias = jnp.where(kpos <= t_row, 0.0, NEG)
            causal_bias = jnp.concatenate([causal_bias, causal_bias], axis=1)
        items = []
        for i, (g, hp) in enumerate(panels):
            def score(g=g, hp=hp):
                s = _dot(ks_s[off:off + PANEL, :], sels[g][:, hp * PANEL:(hp + 1) * PANEL])
                return s if causal_bias is None else s + causal_bias

            def values(g=g):
                return vst_s[c, g, :, sub:sub + PANEL]
            items.append((len(panels) + i, score, values))
        return items

    s_cs = [compressed_scores(g) for g in range(NSA_KV_GROUPS)]
    cmp_out = {}

    def hook(g):
        return lambda: cmp_out.__setitem__(g, compressed_branch(g, s_cs[g]))

    o_w = normalised(run_items(window_items, init, hooks={1 + 4 * g: hook(g) for g in range(NSA_KV_GROUPS)}))
    o_c, sels = zip(*[cmp_out[g] for g in range(NSA_KV_GROUPS)])

    def sequence(n_past):
        def run():
            past = [it for kp in range(n_past) for it in key_panel_items(kp, False, sels)]
            return tuple(run_items(key_panel_items(n_past, True, sels) + past, init + init)[len(init):])
        return run

    o_s = normalised(lax.switch(qi // (PANEL // Q_BLOCK), [sequence(k) for k in range(seq // PANEL)]))

    gates_t = jax.nn.sigmoid(gl_ref[...].astype(F32)).T
    outs = []
    for h in range(NSA_HEADS):
        g, r = divmod(h, NSA_GROUP_SIZE)
        lanes = slice(r * Q_BLOCK, (r + 1) * Q_BLOCK)
        outs.append(gates_t[3 * h:3 * h + 1, :] * o_c[g][:, lanes]
                    + gates_t[3 * h + 1:3 * h + 2, :] * o_s[g][:, lanes]
                    + gates_t[3 * h + 2:3 * h + 3, :] * o_w[g][:, lanes])
    o_ref[...] = jnp.concatenate(outs, axis=0).T.astype(o_ref.dtype)


def _nsa(h, kc, vc, batch, seq):
    nqb = seq // Q_BLOCK
    return pl.pallas_call(
        functools.partial(_nsa_kernel, seq=seq),
        grid=(batch, nqb),
        in_specs=[
            pl.BlockSpec((Q_BLOCK, NSA_OUT), lambda b, i: (b * nqb + i, 0)),
            pl.BlockSpec((N_CMP_PAD, LANES), lambda b, i: (b, 0)),
            pl.BlockSpec((N_CMP_PAD, LANES), lambda b, i: (b, 0)),
            pl.BlockSpec((seq, 2 * NSA_KV_DIM), lambda b, i: (b, 768 // 256)),
            pl.BlockSpec((seq, 2 * NSA_KV_DIM), lambda b, i: (b, 1024 // 256)),
            pl.BlockSpec((Q_BLOCK, LANES), lambda b, i: (b * nqb + i, 2304 // LANES)),
        ],
        out_specs=pl.BlockSpec((Q_BLOCK, NSA_OUT), lambda b, i: (b * nqb + i, 0)),
        out_shape=jax.ShapeDtypeStruct((batch * seq, NSA_OUT), BF16),
        scratch_shapes=[
            pltpu.VMEM((seq, NSA_KV_DIM + LANES), BF16),
            pltpu.VMEM((seq // KEY_CHUNK, NSA_KV_GROUPS, VT_ROWS, KEY_CHUNK), BF16),
            pltpu.VMEM((seq, NSA_KV_DIM), BF16),
            pltpu.VMEM((nqb, NSA_KV_GROUPS, VT_ROWS, Q_BLOCK), BF16),
            pltpu.VMEM((N_CMP_PAD, NSA_KV_DIM), BF16),
            pltpu.VMEM((NSA_KV_DIM, N_CMP_PAD), BF16),
        ],
        compiler_params=pltpu.CompilerParams(
            dimension_semantics=("parallel", "arbitrary"), vmem_limit_bytes=VMEM_LIMIT),
        name="nsa_attn",
    )(h, kc, vc, h, h, h)


def _shift_rows(x, k, row):
    return jnp.where(row >= k, pltpu.roll(x, k, 0), 0.0)


def _pool_kernel(u_ref, w_ref, sc_ref, o_ref, *, seq):
    u = u_ref[...].astype(F32)
    row = lax.broadcasted_iota(jnp.int32, (seq, 1), 0)
    lane = lax.broadcasted_iota(jnp.int32, (1, POOL_DIM), 1)
    tp1 = (row + 1).astype(F32)
    acc = u
    mean = jnp.zeros_like(u)
    span = 1
    for gi, w in enumerate(POOL_WINDOWS):
        while span < w:
            acc = acc + _shift_rows(acc, span, row)
            span *= 2
        cnt = jnp.minimum(float(w), tp1)
        in_group = (lane >= gi * POOL_GROUP_DIM) & (lane < (gi + 1) * POOL_GROUP_DIM)
        mean = jnp.where(in_group, acc / cnt, mean)
    pooled = mean - u
    o_ref[...] = (_dot(pooled.astype(BF16), w_ref[...]) * sc_ref[...]).astype(o_ref.dtype)


def _pool(h, w_bd, scale, batch, seq):
    return pl.pallas_call(
        functools.partial(_pool_kernel, seq=seq),
        grid=(batch,),
        in_specs=[
            pl.BlockSpec((seq, POOL_DIM), lambda b: (b, 1280 // 256)),
            pl.BlockSpec((POOL_DIM, POOL_DIM), lambda b: (0, 0)),
            pl.BlockSpec((1, POOL_DIM), lambda b: (0, 0)),
        ],
        out_specs=pl.BlockSpec((seq, POOL_DIM), lambda b: (b, 0)),
        out_shape=jax.ShapeDtypeStruct((batch * seq, POOL_DIM), BF16),
        compiler_params=pltpu.CompilerParams(
            dimension_semantics=("parallel",), vmem_limit_bytes=VMEM_LIMIT),
        name="pool_mix",
    )(h, w_bd, scale)


def _gla_kernel(q_ref, k_ref, v_ref, a_ref, r_ref, wa2_ref, ba_ref, ng_ref, o_ref,
                qt_ref, kt_ref, kd_ref, dec_ref, oacc_ref, *, seq):
    C, H, DK, DV = GLA_CHUNK, GLA_HEADS, GLA_KEY_DIM, GLA_VAL_DIM
    nc = seq // C
    z = _dot(a_ref[...].astype(BF16), wa2_ref[...]) + ba_ref[...]
    log_a = (jnp.minimum(z, 0.0) - jnp.log(1.0 + jnp.exp(-jnp.abs(z)))) / GLA_TAU
    pos = lax.broadcasted_iota(jnp.int32, (seq, 1), 0) % C
    b = log_a
    step = 1
    while step < C:
        b = b + jnp.where(pos >= step, pltpu.roll(b, step, 0), 0.0)
        step *= 2
    b3 = b.reshape(nc, C, GLA_QK)
    b_last = b3[:, C - 1:C, :]
    k_all = k_ref[...].astype(F32)
    qt_ref[...] = q_ref[...].astype(F32) * (DK ** -0.5) * jnp.exp(b)
    kt_ref[...] = k_all * jnp.exp(-b)
    kd_ref[...] = (k_all.reshape(nc, C, GLA_QK) * jnp.exp(b_last - b3)).reshape(seq, GLA_QK)
    dec_ref[...] = jnp.exp(b_last)

    r_k = lax.broadcasted_iota(jnp.int32, (H * C, GLA_QK), 0) // C
    c_k = lax.broadcasted_iota(jnp.int32, (H * C, GLA_QK), 1) // DK
    mask_k = r_k == c_k
    r_v = lax.broadcasted_iota(jnp.int32, (H * C, GLA_OUT), 0) // C
    c_v = lax.broadcasted_iota(jnp.int32, (H * C, GLA_OUT), 1) // DV
    mask_v = r_v == c_v
    r_s = lax.broadcasted_iota(jnp.int32, (GLA_OUT, GLA_QK), 0) // DV
    c_s = lax.broadcasted_iota(jnp.int32, (GLA_OUT, GLA_QK), 1) // DK
    mask_s = r_s == c_s
    i_a = lax.broadcasted_iota(jnp.int32, (C, H * C), 0)
    j_a = lax.broadcasted_iota(jnp.int32, (C, H * C), 1) % C
    tril = j_a <= i_a

    def chunk_group(gi, state_t):
        ns = [gi * GLA_GROUP + k for k in range(GLA_GROUP)]
        rows = [pl.ds(pl.multiple_of(n * C, C), C) for n in ns]
        q_ts = [qt_ref[r, :].astype(BF16) for r in rows]
        v_cs = [v_ref[r, :].astype(F32) for r in rows]
        a_cats, d_states = [], []
        for r, q_t in zip(rows, q_ts):
            k_bd = jnp.where(mask_k, jnp.concatenate([kt_ref[r, :]] * H, axis=0), 0.0).astype(BF16)
            a_cats.append(jnp.where(tril, _dot_nt(q_t, k_bd), 0.0))
        for r, v_c in zip(rows, v_cs):
            d_states.append(jnp.where(mask_s, _dot(v_c.T.astype(BF16), kd_ref[r, :].astype(BF16)), 0.0))
        o_inters = []
        for n, q_t, d_state in zip(ns, q_ts, d_states):
            o_inters.append(_dot_nt(q_t, state_t.astype(BF16)))
            state_t = state_t * dec_ref[n] + d_state
        for r, a_cat, v_c, o_inter in zip(rows, a_cats, v_cs, o_inters):
            v_bd = jnp.where(mask_v, jnp.concatenate([v_c] * H, axis=0), 0.0).astype(BF16)
            oacc_ref[r, :] = _dot(a_cat.astype(BF16), v_bd) + o_inter
        return state_t

    lax.fori_loop(0, nc // GLA_GROUP, chunk_group, jnp.zeros((GLA_OUT, GLA_QK), F32))

    o = oacc_ref[...]
    gr = lax.broadcasted_iota(jnp.int32, (GLA_OUT, GLA_OUT), 0) // DV
    gc = lax.broadcasted_iota(jnp.int32, (GLA_OUT, GLA_OUT), 1) // DV
    group_mean = jnp.where(gr == gc, 1.0 / DV, 0.0).astype(BF16)
    hi, mid, lo = _split3(o * o)
    ms = _dot(hi, group_mean) + _dot(mid, group_mean) + _dot(lo, group_mean)
    o = o * lax.rsqrt(ms + RMS_EPS) * ng_ref[...]
    r = r_ref[...].astype(F32)
    o_ref[...] = (o * (r * jax.nn.sigmoid(r))).astype(o_ref.dtype)


def _gla(h, wa2_pad, ba, norm_g, batch, seq):
    nc = seq // GLA_CHUNK
    return pl.pallas_call(
        functools.partial(_gla_kernel, seq=seq),
        grid=(batch,),
        in_specs=[
            pl.BlockSpec((seq, GLA_QK), lambda b: (b, 2048 // LANES)),
            pl.BlockSpec((seq, GLA_QK), lambda b: (b, 2176 // LANES)),
            pl.BlockSpec((seq, GLA_OUT), lambda b: (b, 1536 // 256)),
            pl.BlockSpec((seq, LANES), lambda b: (b, 2432 // LANES)),
            pl.BlockSpec((seq, GLA_OUT), lambda b: (b, 1792 // 256)),
            pl.BlockSpec((LANES, GLA_QK), lambda b: (0, 0)),
            pl.BlockSpec((1, GLA_QK), lambda b: (0, 0)),
            pl.BlockSpec((1, GLA_OUT), lambda b: (0, 0)),
        ],
        out_specs=pl.BlockSpec((seq, GLA_OUT), lambda b: (b, 0)),
        out_shape=jax.ShapeDtypeStruct((batch * seq, GLA_OUT), BF16),
        scratch_shapes=[
            pltpu.VMEM((seq, GLA_QK), F32),
            pltpu.VMEM((seq, GLA_QK), F32),
            pltpu.VMEM((seq, GLA_QK), F32),
            pltpu.VMEM((nc, 1, GLA_QK), F32),
            pltpu.VMEM((seq, GLA_OUT), F32),
        ],
        compiler_params=pltpu.CompilerParams(
            dimension_semantics=("parallel",), vmem_limit_bytes=VMEM_LIMIT),
        name="gla_mix",
    )(h, h, h, h, h, wa2_pad, ba, norm_g)


OUT_SUBTILES = 4


def _outproj_kernel(x_ref, oa_ref, ob_ref, oc_ref, wa_ref, wb_ref, wc_ref, g_ref, b_ref, o_ref):
    ts = x_ref.shape[0] // OUT_SUBTILES

    def mixed(s):
        rows = pl.ds(s * ts, ts)
        return (_dot(oa_ref[rows, :], wa_ref[...]) + _dot(ob_ref[rows, :], wb_ref[...])
                + _dot(oc_ref[rows, :], wc_ref[...]))

    m_next = mixed(0)
    for s in range(OUT_SUBTILES):
        m = m_next
        if s + 1 < OUT_SUBTILES:
            m_next = mixed(s + 1)
        rows = pl.ds(s * ts, ts)
        o_ref[rows, :] = _layer_norm(ALPHA * x_ref[rows, :] + m, g_ref[...], b_ref[...])


def _out_proj_ln(x, o_a, o_b, o_c, w_a, w_b, w_c, g, b, *, tm=1024):
    n = x.shape[0]
    row = lambda i: (i, 0)
    const = lambda i: (0, 0)
    return pl.pallas_call(
        _outproj_kernel,
        grid=(n // tm,),
        in_specs=[
            pl.BlockSpec((tm, D_MODEL), row),
            pl.BlockSpec((tm, NSA_OUT), row),
            pl.BlockSpec((tm, POOL_DIM), row),
            pl.BlockSpec((tm, GLA_OUT), row),
            pl.BlockSpec((NSA_OUT, D_MODEL), const),
            pl.BlockSpec((POOL_DIM, D_MODEL), const),
            pl.BlockSpec((GLA_OUT, D_MODEL), const),
            pl.BlockSpec((1, D_MODEL), const),
            pl.BlockSpec((1, D_MODEL), const),
        ],
        out_specs=pl.BlockSpec((tm, D_MODEL), row),
        out_shape=jax.ShapeDtypeStruct((n, D_MODEL), F32),
        compiler_params=pltpu.CompilerParams(
            dimension_semantics=("parallel",), vmem_limit_bytes=VMEM_LIMIT),
        name="out_proj_ln",
    )(x, o_a, o_b, o_c, w_a, w_b, w_c, g, b)


def _block_diag(blocks):
    n, r, c = blocks.shape[-3:]
    lead = [(0, 0)] * (blocks.ndim - 2)
    rows = [jnp.pad(blocks[..., g, :, :], lead + [(g * c, (n - 1 - g) * c)]) for g in range(n)]
    return jnp.concatenate(rows, axis=-2)


def _prep_compress(cmp_pe, cmp_w1, cmp_w2):
    G, HD = NSA_KV_GROUPS, NSA_HEAD_DIM
    pe = jnp.tile(cmp_pe[:, :, None, :], (1, 1, G, 1)).reshape(2, 1, CMP_BLOCK * G * HD)
    w1 = cmp_w1.reshape(2, CMP_BLOCK, 1, HD, HD).astype(BF16)
    w1_bd = _block_diag(jnp.tile(w1, (1, 1, G, 1, 1))).reshape(2, CMP_BLOCK * G * HD, G * HD)
    w2_bd = _block_diag(jnp.tile(cmp_w2[:, None].astype(BF16), (1, G, 1, 1)))
    return pe, w1_bd, w2_bd


def kernel(x, ln_g, ln_b, ffn_wg, ffn_wu, ffn_wd, w_in, w_out, cmp_pe, cmp_w1, cmp_w2,
           pool_w, pool_scale, gla_wa2, gla_ba, gla_norm_g):
    batch, seq, _ = x.shape
    xf = x.reshape(batch * seq, D_MODEL)
    w_in_t = jnp.swapaxes(w_in, 1, 2)
    for l in range(DEPTH):
        lg = lambda i: ln_g[l, i].reshape(1, D_MODEL)
        lb = lambda i: ln_b[l, i].reshape(1, D_MODEL)
        xf = _ffn_ln(xf, ffn_wg, ffn_wu, ffn_wd, lg(0), lb(0), l, 0)
        h, h_kv = _in_proj(xf, w_in_t, l)
        pe_rows, w1_bd, w2_bd = _prep_compress(cmp_pe[l], cmp_w1[l], cmp_w2[l])
        kc, vc = _compress(h_kv, pe_rows, w1_bd, w2_bd, batch, seq)
        o_a = _nsa(h, kc, vc, batch, seq)
        o_b = _pool(h, _block_diag(pool_w[l]).astype(BF16), pool_scale[l].reshape(1, POOL_DIM), batch, seq)
        wa2_pad = jnp.pad(gla_wa2[l], ((0, LANES - GLA_GATE_RANK), (0, 0))).astype(BF16)
        o_c = _gla(h, wa2_pad, gla_ba[l].reshape(1, GLA_QK), gla_norm_g[l].reshape(1, GLA_OUT), batch, seq)
        wo = w_out[l].astype(BF16)
        xf = _out_proj_ln(xf, o_a, o_b, o_c, wo[:NSA_OUT], wo[NSA_OUT:NSA_OUT + POOL_DIM],
                          wo[NSA_OUT + POOL_DIM:], lg(1), lb(1))
        xf = _ffn_ln(xf, ffn_wg, ffn_wu, ffn_wd, lg(2), lb(2), l, 1)
    return xf.reshape(batch, seq, D_MODEL)
```

```python
import functools

import numpy as np
import jax
import jax.numpy as jnp
from jax import lax
from jax.experimental import pallas as pl
from jax.experimental.pallas import tpu as pltpu

F32 = jnp.float32
BF16 = jnp.bfloat16

D_MODEL = 1024
DEPTH = 4
D_FF = 2816
NSA_HEADS = 8
NSA_KV_GROUPS = 2
NSA_HEAD_DIM = 64
NSA_GROUP_SIZE = NSA_HEADS // NSA_KV_GROUPS
NSA_OUT = NSA_HEADS * NSA_HEAD_DIM
NSA_KV_DIM = NSA_KV_GROUPS * NSA_HEAD_DIM
CMP_BLOCK = 32
CMP_STRIDE = 16
SLC_BLOCK = 64
SLC_TOPK = 8
WINDOW = 512
Q_BLOCK = 128
POOL_WINDOWS = (2, 4, 8, 16)
POOL_GROUPS = 4
POOL_GROUP_DIM = 64
POOL_DIM = POOL_GROUPS * POOL_GROUP_DIM
GLA_HEADS = 4
GLA_KEY_DIM = 32
GLA_VAL_DIM = 64
GLA_GATE_RANK = 16
GLA_TAU = 16.0
GLA_CHUNK = 64
GLA_QK = GLA_HEADS * GLA_KEY_DIM
GLA_OUT = GLA_HEADS * GLA_VAL_DIM
GLA_GROUP = 16
ALPHA = (2.0 * DEPTH) ** 0.25
LN_EPS = 1e-5
RMS_EPS = 1e-6

LANES = 128
NEG = -1e30
VMEM_LIMIT = 56 * 1024 * 1024

_IN_LAYOUT = {
    "qkv": (0, 1280, 0),
    "u": (1304, 256, 1280),
    "gv": (1816, 256, 1536),
    "gr": (2088, 256, 1792),
    "gq": (1560, 128, 2048),
    "gk": (1688, 128, 2176),
    "gl": (1280, 24, 2304),
    "ga": (2072, 16, 2432),
}
D_IN_PAD = 2560


def _dot(a, b):
    return jnp.dot(a, b, preferred_element_type=F32)


def _dot_nt(a, b):
    return lax.dot_general(a, b, (((1,), (1,)), ((), ())), preferred_element_type=F32)


def _layer_norm(y, g, b):
    mu = jnp.mean(y, axis=-1, keepdims=True)
    yc = y - mu
    var = jnp.mean(yc * yc, axis=-1, keepdims=True)
    return yc * lax.rsqrt(var + LN_EPS) * g + b


def _split3(x):
    hi = x.astype(BF16)
    r1 = x - hi.astype(F32)
    mid = r1.astype(BF16)
    lo = (r1 - mid.astype(F32)).astype(BF16)
    return hi, mid, lo


FF_CHUNK = 256


N_FF_CHUNKS = D_FF // FF_CHUNK
FF_UNROLL = 5


def _ffn_kernel(x_ref, wg_hbm, wu_hbm, wd_hbm, g_ref, b_ref, o_ref,
                xb_ref, acc_ref, wg_s, wu_s, wd_s, sg_ref, su_ref, sd_ref, sem, *, layer, which):
    def stage_copies(j, slot):
        cols = pl.ds(j * FF_CHUNK, FF_CHUNK)
        return (pltpu.make_async_copy(wg_hbm.at[layer, which, :, cols], sg_ref.at[slot], sem.at[0, slot]),
                pltpu.make_async_copy(wu_hbm.at[layer, which, :, cols], su_ref.at[slot], sem.at[1, slot]),
                pltpu.make_async_copy(wd_hbm.at[layer, which, cols, :], sd_ref.at[slot], sem.at[2, slot]))

    def start(j, slot):
        for cp in stage_copies(j, slot):
            cp.start()

    def land(j, slot):
        for cp in stage_copies(j, slot):
            cp.wait()
        wg_s[j] = sg_ref[slot].astype(BF16)
        wu_s[j] = su_ref[slot].astype(BF16)
        wd_s[j] = (0.5 * sd_ref[slot]).astype(BF16)

    def contribution(j):
        xb = xb_ref[...]
        gate = _dot(xb, wg_s[j])
        up = _dot(xb, wu_s[j])
        act = (gate * jax.nn.sigmoid(gate)) * up
        return _dot(act.astype(BF16), wd_s[j])

    xb_ref[...] = x_ref[...].astype(BF16)
    first = pl.program_id(0) == 0

    @pl.when(first)
    def _():
        start(0, 0)
        start(1, 1)
        land(0, 0)
        acc_ref[...] = contribution(0)

        def pair(p, carry):
            j = 1 + 2 * p
            start(j + 1, 0)
            land(j, 1)
            acc_ref[...] += contribution(j)
            start(j + 2, 1)
            land(j + 1, 0)
            acc_ref[...] += contribution(j + 1)
            return carry

        n_pairs = (N_FF_CHUNKS - 3) // 2
        lax.fori_loop(0, n_pairs, pair, 0)
        j = 1 + 2 * n_pairs
        start(j + 1, 0)
        land(j, 1)
        acc_ref[...] += contribution(j)
        land(j + 1, 0)
        acc_ref[...] += contribution(j + 1)

    @pl.when(jnp.logical_not(first))
    def _():
        acc_ref[...] = contribution(0)

        def step(j, carry):
            acc_ref[...] += contribution(j)
            return carry

        lax.fori_loop(1, N_FF_CHUNKS, step, 0, unroll=FF_UNROLL)

    y = ALPHA * x_ref[...] + acc_ref[...]
    o_ref[...] = _layer_norm(y, g_ref[...], b_ref[...])


def _ffn_ln(x, wg, wu, wd, g, b, layer, which, *, tm=1024):
    n = x.shape[0]
    return pl.pallas_call(
        functools.partial(_ffn_kernel, layer=layer, which=which),
        grid=(n // tm,),
        in_specs=[
            pl.BlockSpec((tm, D_MODEL), lambda i: (i, 0)),
            pl.BlockSpec(memory_space=pl.ANY),
            pl.BlockSpec(memory_space=pl.ANY),
            pl.BlockSpec(memory_space=pl.ANY),
            pl.BlockSpec((1, D_MODEL), lambda i: (0, 0)),
            pl.BlockSpec((1, D_MODEL), lambda i: (0, 0)),
        ],
        out_specs=pl.BlockSpec((tm, D_MODEL), lambda i: (i, 0)),
        out_shape=jax.ShapeDtypeStruct((n, D_MODEL), F32),
        scratch_shapes=[
            pltpu.VMEM((tm, D_MODEL), BF16),
            pltpu.VMEM((tm, D_MODEL), F32),
            pltpu.VMEM((N_FF_CHUNKS, D_MODEL, FF_CHUNK), BF16),
            pltpu.VMEM((N_FF_CHUNKS, D_MODEL, FF_CHUNK), BF16),
            pltpu.VMEM((N_FF_CHUNKS, FF_CHUNK, D_MODEL), BF16),
            pltpu.VMEM((2, D_MODEL, FF_CHUNK), F32),
            pltpu.VMEM((2, D_MODEL, FF_CHUNK), F32),
            pltpu.VMEM((2, FF_CHUNK, D_MODEL), F32),
            pltpu.SemaphoreType.DMA((3, 2)),
        ],
        compiler_params=pltpu.CompilerParams(
            dimension_semantics=("arbitrary",), vmem_limit_bytes=VMEM_LIMIT),
        name="ffn_ln",
    )(x, wg, wu, wd, g, b)


D_IN = 2344
CMP_COLS = (512, 768)


def _inproj_kernel(x_ref, w_hbm, o_ref, okv_ref, w_s, stage_ref, sem, *, layer):
    @pl.when(pl.program_id(0) == 0)
    def _():
        cp = pltpu.make_async_copy(w_hbm.at[layer], stage_ref, sem.at[0])
        cp.start()
        cp.wait()
        w_s[...] = jnp.zeros_like(w_s)
        for src, width, dst in _IN_LAYOUT.values():
            w_s[dst:dst + width, :] = stage_ref[src:src + width, :].astype(BF16)

    h = _dot_nt(x_ref[...].astype(BF16), w_s[...])
    o_ref[...] = h.astype(o_ref.dtype)
    okv_ref[...] = h[:, CMP_COLS[0]:CMP_COLS[1]]


def _in_proj(x, w_in_t, layer, *, tm=1024):
    n = x.shape[0]
    n_cmp = CMP_COLS[1] - CMP_COLS[0]
    return pl.pallas_call(
        functools.partial(_inproj_kernel, layer=layer),
        grid=(n // tm,),
        in_specs=[
            pl.BlockSpec((tm, D_MODEL), lambda i: (i, 0)),
            pl.BlockSpec(memory_space=pl.ANY),
        ],
        out_specs=[pl.BlockSpec((tm, D_IN_PAD), lambda i: (i, 0)),
                   pl.BlockSpec((tm, n_cmp), lambda i: (i, 0))],
        out_shape=[jax.ShapeDtypeStruct((n, D_IN_PAD), BF16), jax.ShapeDtypeStruct((n, n_cmp), F32)],
        scratch_shapes=[
            pltpu.VMEM((D_IN_PAD, D_MODEL), BF16),
            pltpu.VMEM((D_IN, D_MODEL), F32),
            pltpu.SemaphoreType.DMA((1,)),
        ],
        compiler_params=pltpu.CompilerParams(
            dimension_semantics=("arbitrary",), vmem_limit_bytes=VMEM_LIMIT),
        name="in_proj",
    )(x, w_in_t)


N_CMP_PAD = 128


def _gelu_tanh(x):
    return 0.5 * x * (1.0 + jnp.tanh(np.sqrt(2.0 / np.pi) * (x + 0.044715 * (x * x * x))))


def _compress_kernel(zk_ref, zv_ref, pe_ref, w1_ref, w2_ref, ok_ref, ov_ref):
    def one(z_ref, which, o_ref):
        slabs = [z_ref[pl.ds(q, N_CMP_PAD, stride=CMP_STRIDE), :] for q in range(CMP_STRIDE)]
        cat = jnp.concatenate(slabs, axis=1)
        half = CMP_STRIDE * LANES
        top = _dot((cat + pe_ref[which, :, :half]).astype(BF16), w1_ref[which, :half, :])
        bot = _dot((cat + pe_ref[which, :, half:]).astype(BF16), w1_ref[which, half:, :])
        pre = top + pltpu.roll(bot, N_CMP_PAD - 1, 0)
        o_ref[...] = _dot(_gelu_tanh(pre).astype(BF16), w2_ref[which])

    one(zk_ref, 0, ok_ref)
    one(zv_ref, 1, ov_ref)


def _compress(h_kv, pe_rows, w1_bd, w2_bd, batch, seq):
    return pl.pallas_call(
        _compress_kernel,
        grid=(batch,),
        in_specs=[
            pl.BlockSpec((seq, LANES), lambda b: (b, 0)),
            pl.BlockSpec((seq, LANES), lambda b: (b, 1)),
            pl.BlockSpec((2, 1, CMP_BLOCK * LANES), lambda b: (0, 0, 0)),
            pl.BlockSpec((2, CMP_BLOCK * LANES, LANES), lambda b: (0, 0, 0)),
            pl.BlockSpec((2, LANES, LANES), lambda b: (0, 0, 0)),
        ],
        out_specs=[
            pl.BlockSpec((N_CMP_PAD, LANES), lambda b: (b, 0)),
            pl.BlockSpec((N_CMP_PAD, LANES), lambda b: (b, 0)),
        ],
        out_shape=[jax.ShapeDtypeStruct((batch * N_CMP_PAD, LANES), F32)] * 2,
        compiler_params=pltpu.CompilerParams(
            dimension_semantics=("parallel",), vmem_limit_bytes=VMEM_LIMIT),
        name="nsa_compress",
    )(h_kv, h_kv, pe_rows, w1_bd, w2_bd)


N_SLC = 32
WIN_KEYS = WINDOW + Q_BLOCK
KEY_CHUNK = 512
GQ = NSA_GROUP_SIZE * Q_BLOCK
PANEL = 256
SCORE_LOOKAHEAD = 9


VT_ROWS = NSA_HEAD_DIM + 16
LOG2E = 1.4426950408889634


def _softmax_cols(s):
    m = jnp.max(s, axis=0, keepdims=True)
    m = jnp.where(m > 0.5 * NEG, m, 0.0)
    e = jnp.exp2(s - m)
    d = jnp.sum(e, axis=0, keepdims=True)
    inv = 1.0 / jnp.where(d > 0.0, d, 1.0)
    return e, inv


def _tile4(x):
    return jnp.concatenate([x] * NSA_GROUP_SIZE, axis=1)


def _values_t(v_both):
    n = v_both.shape[0]
    v_t = v_both.astype(F32).T
    extra = (lax.broadcasted_iota(jnp.int32, (VT_ROWS - NSA_HEAD_DIM, n), 0) == 0).astype(F32)
    return [jnp.concatenate([v_t[g * NSA_HEAD_DIM:(g + 1) * NSA_HEAD_DIM, :], extra], axis=0).astype(BF16)
            for g in range(NSA_KV_GROUPS)]


def _nsa_kernel(q_ref, kc_ref, vc_ref, ksvs_ref, kwvw_ref, gl_ref, o_ref,
                ks_s, vst_s, kw_s, vwt_s, kc_s, vct_s, *, seq):
    qi = pl.program_id(1)
    start = qi * Q_BLOCK
    hd = NSA_HEAD_DIM
    n_chunks = seq // KEY_CHUNK
    n_qb = seq // Q_BLOCK

    @pl.when(qi == 0)
    def _():
        key_blk = lax.broadcasted_iota(jnp.int32, (seq, LANES), 0) // SLC_BLOCK
        blk = lax.broadcasted_iota(jnp.int32, (seq, LANES), 1)
        ks_s[:, :NSA_KV_DIM] = ksvs_ref[:, :NSA_KV_DIM].astype(BF16)
        ks_s[:, NSA_KV_DIM:] = (key_blk == blk).astype(BF16)
        kw_s[...] = kwvw_ref[:, :NSA_KV_DIM].astype(BF16)
        for c in range(n_chunks):
            for g, v_t in enumerate(_values_t(ksvs_ref[c * KEY_CHUNK:(c + 1) * KEY_CHUNK, NSA_KV_DIM:])):
                vst_s[c, g] = v_t
        for j in range(n_qb):
            for g, v_t in enumerate(_values_t(kwvw_ref[j * Q_BLOCK:(j + 1) * Q_BLOCK, NSA_KV_DIM:])):
                vwt_s[j, g] = v_t
        kc_s[...] = kc_ref[...].astype(BF16)
        vct_s[...] = vc_ref[...].T.astype(BF16)

    t_row = start + lax.broadcasted_iota(jnp.int32, (1, Q_BLOCK), 1)

    q_t = (q_ref[...].astype(F32) * (hd ** -0.5 * LOG2E)).T.astype(BF16)
    zeros_half = jnp.zeros((hd, GQ), BF16)
    q_ops = []
    for g in range(NSA_KV_GROUPS):
        top = jnp.concatenate([q_t[(g * NSA_GROUP_SIZE + r) * hd:(g * NSA_GROUP_SIZE + r + 1) * hd, :]
                               for r in range(NSA_GROUP_SIZE)], axis=1)
        q_ops.append(jnp.concatenate([top, zeros_half] if g == 0 else [zeros_half, top], axis=0))

    n_sub = lax.broadcasted_iota(jnp.int32, (N_CMP_PAD, 1), 0)
    bias_c = jnp.where(n_sub * CMP_STRIDE + (CMP_BLOCK - 1) <= t_row, 0.0, NEG)
    m_sub = lax.broadcasted_iota(jnp.int32, (N_SLC, N_CMP_PAD), 0)
    n_lane = lax.broadcasted_iota(jnp.int32, (N_SLC, N_CMP_PAD), 1)
    c0 = n_lane * CMP_STRIDE
    s0 = m_sub * SLC_BLOCK
    ov_t = ((c0 <= s0 + SLC_BLOCK - 1) & (c0 + CMP_BLOCK - 1 >= s0)
            & (n_lane < seq // CMP_STRIDE - 1)).astype(BF16)
    m_idx = lax.broadcasted_iota(jnp.int32, (N_SLC, Q_BLOCK), 0)
    m_idx_f = m_idx.astype(F32)
    cur = t_row // SLC_BLOCK
    forced = (m_idx == 0) | (m_idx == cur) | (m_idx == cur - 1)
    future = m_idx * SLC_BLOCK > t_row

    def compressed_scores(g):
        return _dot(kc_s[...], q_ops[g]) + _tile4(bias_c)

    def compressed_branch(g, s_c):
        e, inv = _softmax_cols(s_c)
        p_c = e * inv
        o_cg = _dot(vct_s[g * hd:(g + 1) * hd, :], p_c.astype(BF16))
        p_sum = (p_c[:, 0:Q_BLOCK] + p_c[:, Q_BLOCK:2 * Q_BLOCK]
                 + p_c[:, 2 * Q_BLOCK:3 * Q_BLOCK] + p_c[:, 3 * Q_BLOCK:])
        hi, mid, lo = _split3(p_sum)
        imp = _dot(ov_t, hi) + _dot(ov_t, mid) + _dot(ov_t, lo)
        imp = jnp.where(forced, -NEG, jnp.where(future, NEG, imp))
        chosen = jnp.zeros((N_SLC, Q_BLOCK), jnp.bool_)
        for _ in range(SLC_TOPK):
            top = jnp.max(imp, axis=0, keepdims=True)
            first = jnp.min(jnp.where(imp == top, m_idx_f, float(N_SLC)), axis=0, keepdims=True)
            hit = m_idx_f == first
            chosen = chosen | hit
            imp = jnp.where(hit, 2.0 * NEG, imp)
        sel_bias = _tile4(jnp.where(chosen & jnp.logical_not(future), 0.0, NEG)).astype(BF16)
        return o_cg, jnp.concatenate([q_ops[g], sel_bias, jnp.zeros((LANES - N_SLC, GQ), BF16)], axis=0)

    j0 = jnp.maximum(qi - WINDOW // Q_BLOCK, 0)
    w0 = pl.multiple_of(j0 * Q_BLOCK, Q_BLOCK)
    panels = [(g, hp) for g in range(NSA_KV_GROUPS) for hp in range(GQ // PANEL)]
    init = (jnp.full((1, PANEL), NEG, F32), jnp.zeros((VT_ROWS, PANEL), F32)) * len(panels)


    def normalised(state):
        outs = []
        for g in range(NSA_KV_GROUPS):
            accs = [state[2 * panels.index((g, hp)) + 1] for hp in range(GQ // PANEL)]
            outs.append(jnp.concatenate([a[:hd] * (1.0 / a[hd:hd + 1]) for a in accs], axis=1))
        return outs

    def run_items(items, state, hooks=None):
        state = list(state)
        ahead = [it[1]() for it in items[:SCORE_LOOKAHEAD]]
        pending = None

        def flush(pending):
            i, alpha, p, values = pending
            state[2 * i + 1] = alpha * state[2 * i + 1] + _dot(values(), p)

        for n, (i, _, values) in enumerate(items):
            s = ahead.pop(0)
            if n + SCORE_LOOKAHEAD < len(items):
                ahead.append(items[n + SCORE_LOOKAHEAD][1]())
            m_p = state[2 * i]
            m_n = jnp.maximum(m_p, jnp.max(s, axis=0, keepdims=True))
            state[2 * i] = m_n
            p = jnp.exp2(s - m_n).astype(BF16)
            if pending is not None:
                flush(pending)
            pending = (i, jnp.exp2(m_p - m_n), p, values)
            if hooks and n in hooks:
                hooks[n]()
        flush(pending)
        return state

    items = []
    for o, n in ((0, 2 * Q_BLOCK), (2 * Q_BLOCK, 2 * Q_BLOCK), (4 * Q_BLOCK, Q_BLOCK)):
        kp_w = w0 + o + lax.broadcasted_iota(jnp.int32, (n, 1), 0)
        bias_w = jnp.where((kp_w <= t_row) & (kp_w > t_row - WINDOW), 0.0, NEG)
        bias_w = jnp.concatenate([bias_w, bias_w], axis=1)
        for i, (g, hp) in enumerate(panels):
            def score(o=o, n=n, g=g, hp=hp, bias_w=bias_w):
                return _dot(kw_s[pl.ds(w0 + o, n), :], q_ops[g][:, hp * PANEL:(hp + 1) * PANEL]) + bias_w

            def values(o=o, n=n, g=g):
                return jnp.concatenate([vwt_s[j0 + o // Q_BLOCK + j, g] for j in range(n // Q_BLOCK)], axis=1)
            items.append((i, score, values))
    window_items = items

    def key_panel_items(kp, diagonal, sels):
        off = kp * PANEL
        c, sub = divmod(off, KEY_CHUNK)
        causal_bias = None
        if diagonal:
            kpos = off + lax.broadcasted_iota(jnp.int32, (PANEL, 1), 0)
            causal_bias = jnp.where(kpos <= t_row, 0.0, NEG)
            causal_bias = jnp.concatenate([causal_bias, causal_bias], axis=1)
        items = []
        for i, (g, hp) in enumerate(panels):
            def score(g=g, hp=hp):
                s = _dot(ks_s[off:off + PANEL, :], sels[g][:, hp * PANEL:(hp + 1) * PANEL])
                return s if causal_bias is None else s + causal_bias

            def values(g=g):
                return vst_s[c, g, :, sub:sub + PANEL]
            items.append((len(panels) + i, score, values))
        return items

    s_cs = [compressed_scores(g) for g in range(NSA_KV_GROUPS)]
    cmp_out = {}

    def hook(g):
        return lambda: cmp_out.__setitem__(g, compressed_branch(g, s_cs[g]))

    o_w = normalised(run_items(window_items, init, hooks={1 + 4 * g: hook(g) for g in range(NSA_KV_GROUPS)}))
    o_c, sels = zip(*[cmp_out[g] for g in range(NSA_KV_GROUPS)])

    def sequence(n_past):
        def run():
            past = [it for kp in range(n_past) for it in key_panel_items(kp, False, sels)]
            return tuple(run_items(key_panel_items(n_past, True, sels) + past, init + init)[len(init):])
        return run

    o_s = normalised(lax.switch(qi // (PANEL // Q_BLOCK), [sequence(k) for k in range(seq // PANEL)]))

    gates_t = jax.nn.sigmoid(gl_ref[...].astype(F32)).T
    outs = []
    for h in range(NSA_HEADS):
        g, r = divmod(h, NSA_GROUP_SIZE)
        lanes = slice(r * Q_BLOCK, (r + 1) * Q_BLOCK)
        outs.append(gates_t[3 * h:3 * h + 1, :] * o_c[g][:, lanes]
                    + gates_t[3 * h + 1:3 * h + 2, :] * o_s[g][:, lanes]
                    + gates_t[3 * h + 2:3 * h + 3, :] * o_w[g][:, lanes])
    o_ref[...] = jnp.concatenate(outs, axis=0).T.astype(o_ref.dtype)


def _nsa(h, kc, vc, batch, seq):
    nqb = seq // Q_BLOCK
    return pl.pallas_call(
        functools.partial(_nsa_kernel, seq=seq),
        grid=(batch, nqb),
        in_specs=[
            pl.BlockSpec((Q_BLOCK, NSA_OUT), lambda b, i: (b * nqb + i, 0)),
            pl.BlockSpec((N_CMP_PAD, LANES), lambda b, i: (b, 0)),
            pl.BlockSpec((N_CMP_PAD, LANES), lambda b, i: (b, 0)),
            pl.BlockSpec((seq, 2 * NSA_KV_DIM), lambda b, i: (b, 768 // 256)),
            pl.BlockSpec((seq, 2 * NSA_KV_DIM), lambda b, i: (b, 1024 // 256)),
            pl.BlockSpec((Q_BLOCK, LANES), lambda b, i: (b * nqb + i, 2304 // LANES)),
        ],
        out_specs=pl.BlockSpec((Q_BLOCK, NSA_OUT), lambda b, i: (b * nqb + i, 0)),
        out_shape=jax.ShapeDtypeStruct((batch * seq, NSA_OUT), BF16),
        scratch_shapes=[
            pltpu.VMEM((seq, NSA_KV_DIM + LANES), BF16),
            pltpu.VMEM((seq // KEY_CHUNK, NSA_KV_GROUPS, VT_ROWS, KEY_CHUNK), BF16),
            pltpu.VMEM((seq, NSA_KV_DIM), BF16),
            pltpu.VMEM((nqb, NSA_KV_GROUPS, VT_ROWS, Q_BLOCK), BF16),
            pltpu.VMEM((N_CMP_PAD, NSA_KV_DIM), BF16),
            pltpu.VMEM((NSA_KV_DIM, N_CMP_PAD), BF16),
        ],
        compiler_params=pltpu.CompilerParams(
            dimension_semantics=("parallel", "arbitrary"), vmem_limit_bytes=VMEM_LIMIT),
        name="nsa_attn",
    )(h, kc, vc, h, h, h)


def _shift_rows(x, k, row):
    return jnp.where(row >= k, pltpu.roll(x, k, 0), 0.0)


def _pool_kernel(u_ref, w_ref, sc_ref, o_ref, *, seq):
    u = u_ref[...].astype(F32)
    row = lax.broadcasted_iota(jnp.int32, (seq, 1), 0)
    lane = lax.broadcasted_iota(jnp.int32, (1, POOL_DIM), 1)
    tp1 = (row + 1).astype(F32)
    acc = u
    mean = jnp.zeros_like(u)
    span = 1
    for gi, w in enumerate(POOL_WINDOWS):
        while span < w:
            acc = acc + _shift_rows(acc, span, row)
            span *= 2
        cnt = jnp.minimum(float(w), tp1)
        in_group = (lane >= gi * POOL_GROUP_DIM) & (lane < (gi + 1) * POOL_GROUP_DIM)
        mean = jnp.where(in_group, acc / cnt, mean)
    pooled = mean - u
    o_ref[...] = (_dot(pooled.astype(BF16), w_ref[...]) * sc_ref[...]).astype(o_ref.dtype)


def _pool(h, w_bd, scale, batch, seq):
    return pl.pallas_call(
        functools.partial(_pool_kernel, seq=seq),
        grid=(batch,),
        in_specs=[
            pl.BlockSpec((seq, POOL_DIM), lambda b: (b, 1280 // 256)),
            pl.BlockSpec((POOL_DIM, POOL_DIM), lambda b: (0, 0)),
            pl.BlockSpec((1, POOL_DIM), lambda b: (0, 0)),
        ],
        out_specs=pl.BlockSpec((seq, POOL_DIM), lambda b: (b, 0)),
        out_shape=jax.ShapeDtypeStruct((batch * seq, POOL_DIM), BF16),
        compiler_params=pltpu.CompilerParams(
            dimension_semantics=("parallel",), vmem_limit_bytes=VMEM_LIMIT),
        name="pool_mix",
    )(h, w_bd, scale)


def _gla_kernel(q_ref, k_ref, v_ref, a_ref, r_ref, wa2_ref, ba_ref, ng_ref, u_ref, pw_ref, psc_ref,
                o_ref, ob_ref, qt_ref, kt_ref, kd_ref, dec_ref, oacc_ref, *, seq):
    C, H, DK, DV = GLA_CHUNK, GLA_HEADS, GLA_KEY_DIM, GLA_VAL_DIM
    nc = seq // C
    _pool_kernel(u_ref, pw_ref, psc_ref, ob_ref, seq=seq)
    z = _dot(a_ref[...].astype(BF16), wa2_ref[...]) + ba_ref[...]
    log_a = (jnp.minimum(z, 0.0) - jnp.log(1.0 + jnp.exp(-jnp.abs(z)))) / GLA_TAU
    pos = lax.broadcasted_iota(jnp.int32, (seq, 1), 0) % C
    b = log_a
    step = 1
    while step < C:
        b = b + jnp.where(pos >= step, pltpu.roll(b, step, 0), 0.0)
        step *= 2
    b3 = b.reshape(nc, C, GLA_QK)
    b_last = b3[:, C - 1:C, :]
    k_all = k_ref[...].astype(F32)
    qt_ref[...] = q_ref[...].astype(F32) * (DK ** -0.5) * jnp.exp(b)
    kt_ref[...] = k_all * jnp.exp(-b)
    kd_ref[...] = (k_all.reshape(nc, C, GLA_QK) * jnp.exp(b_last - b3)).reshape(seq, GLA_QK)
    dec_ref[...] = jnp.exp(b_last)

    r_k = lax.broadcasted_iota(jnp.int32, (H * C, GLA_QK), 0) // C
    c_k = lax.broadcasted_iota(jnp.int32, (H * C, GLA_QK), 1) // DK
    mask_k = r_k == c_k
    r_v = lax.broadcasted_iota(jnp.int32, (H * C, GLA_OUT), 0) // C
    c_v = lax.broadcasted_iota(jnp.int32, (H * C, GLA_OUT), 1) // DV
    mask_v = r_v == c_v
    r_s = lax.broadcasted_iota(jnp.int32, (GLA_OUT, GLA_QK), 0) // DV
    c_s = lax.broadcasted_iota(jnp.int32, (GLA_OUT, GLA_QK), 1) // DK
    mask_s = r_s == c_s
    i_a = lax.broadcasted_iota(jnp.int32, (C, H * C), 0)
    j_a = lax.broadcasted_iota(jnp.int32, (C, H * C), 1) % C
    tril = j_a <= i_a

    def chunk_group(gi, state_t):
        ns = [gi * GLA_GROUP + k for k in range(GLA_GROUP)]
        rows = [pl.ds(pl.multiple_of(n * C, C), C) for n in ns]
        q_ts = [qt_ref[r, :].astype(BF16) for r in rows]
        v_cs = [v_ref[r, :].astype(F32) for r in rows]
        a_cats, d_states = [], []
        for r, q_t in zip(rows, q_ts):
            k_bd = jnp.where(mask_k, jnp.concatenate([kt_ref[r, :]] * H, axis=0), 0.0).astype(BF16)
            a_cats.append(jnp.where(tril, _dot_nt(q_t, k_bd), 0.0))
        for r, v_c in zip(rows, v_cs):
            d_states.append(jnp.where(mask_s, _dot(v_c.T.astype(BF16), kd_ref[r, :].astype(BF16)), 0.0))
        o_inters = []
        for n, q_t, d_state in zip(ns, q_ts, d_states):
            o_inters.append(_dot_nt(q_t, state_t.astype(BF16)))
            state_t = state_t * dec_ref[n] + d_state
        for r, a_cat, v_c, o_inter in zip(rows, a_cats, v_cs, o_inters):
            v_bd = jnp.where(mask_v, jnp.concatenate([v_c] * H, axis=0), 0.0).astype(BF16)
            oacc_ref[r, :] = _dot(a_cat.astype(BF16), v_bd) + o_inter
        return state_t

    lax.fori_loop(0, nc // GLA_GROUP, chunk_group, jnp.zeros((GLA_OUT, GLA_QK), F32))

    o = oacc_ref[...]
    gr = lax.broadcasted_iota(jnp.int32, (GLA_OUT, GLA_OUT), 0) // DV
    gc = lax.broadcasted_iota(jnp.int32, (GLA_OUT, GLA_OUT), 1) // DV
    group_mean = jnp.where(gr == gc, 1.0 / DV, 0.0).astype(BF16)
    hi, mid, lo = _split3(o * o)
    ms = _dot(hi, group_mean) + _dot(mid, group_mean) + _dot(lo, group_mean)
    o = o * lax.rsqrt(ms + RMS_EPS) * ng_ref[...]
    r = r_ref[...].astype(F32)
    o_ref[...] = (o * (r * jax.nn.sigmoid(r))).astype(o_ref.dtype)


def _gla_pool(h, wa2_pad, ba, norm_g, pool_w_bd, pool_scale, batch, seq):
    nc = seq // GLA_CHUNK
    return pl.pallas_call(
        functools.partial(_gla_kernel, seq=seq),
        grid=(batch,),
        in_specs=[
            pl.BlockSpec((seq, GLA_QK), lambda b: (b, 2048 // LANES)),
            pl.BlockSpec((seq, GLA_QK), lambda b: (b, 2176 // LANES)),
            pl.BlockSpec((seq, GLA_OUT), lambda b: (b, 1536 // 256)),
            pl.BlockSpec((seq, LANES), lambda b: (b, 2432 // LANES)),
            pl.BlockSpec((seq, GLA_OUT), lambda b: (b, 1792 // 256)),
            pl.BlockSpec((LANES, GLA_QK), lambda b: (0, 0)),
            pl.BlockSpec((1, GLA_QK), lambda b: (0, 0)),
            pl.BlockSpec((1, GLA_OUT), lambda b: (0, 0)),
            pl.BlockSpec((seq, POOL_DIM), lambda b: (b, 1280 // 256)),
            pl.BlockSpec((POOL_DIM, POOL_DIM), lambda b: (0, 0)),
            pl.BlockSpec((1, POOL_DIM), lambda b: (0, 0)),
        ],
        out_specs=[pl.BlockSpec((seq, GLA_OUT), lambda b: (b, 0)),
                   pl.BlockSpec((seq, POOL_DIM), lambda b: (b, 0))],
        out_shape=[jax.ShapeDtypeStruct((batch * seq, GLA_OUT), BF16),
                   jax.ShapeDtypeStruct((batch * seq, POOL_DIM), BF16)],
        scratch_shapes=[
            pltpu.VMEM((seq, GLA_QK), F32),
            pltpu.VMEM((seq, GLA_QK), F32),
            pltpu.VMEM((seq, GLA_QK), F32),
            pltpu.VMEM((nc, 1, GLA_QK), F32),
            pltpu.VMEM((seq, GLA_OUT), F32),
        ],
        compiler_params=pltpu.CompilerParams(
            dimension_semantics=("parallel",), vmem_limit_bytes=VMEM_LIMIT),
        name="gla_pool_mix",
    )(h, h, h, h, h, wa2_pad, ba, norm_g, h, pool_w_bd, pool_scale)


OUT_SUBTILES = 4


def _outproj_kernel(x_ref, oa_ref, ob_ref, oc_ref, wa_ref, wb_ref, wc_ref, g_ref, b_ref, o_ref):
    ts = x_ref.shape[0] // OUT_SUBTILES

    def mixed(s):
        rows = pl.ds(s * ts, ts)
        return (_dot(oa_ref[rows, :], wa_ref[...]) + _dot(ob_ref[rows, :], wb_ref[...])
                + _dot(oc_ref[rows, :], wc_ref[...]))

    m_next = mixed(0)
    for s in range(OUT_SUBTILES):
        m = m_next
        if s + 1 < OUT_SUBTILES:
            m_next = mixed(s + 1)
        rows = pl.ds(s * ts, ts)
        o_ref[rows, :] = _layer_norm(ALPHA * x_ref[rows, :] + m, g_ref[...], b_ref[...])


def _out_proj_ln(x, o_a, o_b, o_c, w_a, w_b, w_c, g, b, *, tm=1024):
    n = x.shape[0]
    row = lambda i: (i, 0)
    const = lambda i: (0, 0)
    return pl.pallas_call(
        _outproj_kernel,
        grid=(n // tm,),
        in_specs=[
            pl.BlockSpec((tm, D_MODEL), row),
            pl.BlockSpec((tm, NSA_OUT), row),
            pl.BlockSpec((tm, POOL_DIM), row),
            pl.BlockSpec((tm, GLA_OUT), row),
            pl.BlockSpec((NSA_OUT, D_MODEL), const),
            pl.BlockSpec((POOL_DIM, D_MODEL), const),
            pl.BlockSpec((GLA_OUT, D_MODEL), const),
            pl.BlockSpec((1, D_MODEL), const),
            pl.BlockSpec((1, D_MODEL), const),
        ],
        out_specs=pl.BlockSpec((tm, D_MODEL), row),
        out_shape=jax.ShapeDtypeStruct((n, D_MODEL), F32),
        compiler_params=pltpu.CompilerParams(
            dimension_semantics=("parallel",), vmem_limit_bytes=VMEM_LIMIT),
        name="out_proj_ln",
    )(x, o_a, o_b, o_c, w_a, w_b, w_c, g, b)


def _block_diag(blocks):
    n, r, c = blocks.shape[-3:]
    lead = [(0, 0)] * (blocks.ndim - 2)
    rows = [jnp.pad(blocks[..., g, :, :], lead + [(g * c, (n - 1 - g) * c)]) for g in range(n)]
    return jnp.concatenate(rows, axis=-2)


def _prep_compress(cmp_pe, cmp_w1, cmp_w2):
    G, HD = NSA_KV_GROUPS, NSA_HEAD_DIM
    pe = jnp.tile(cmp_pe[:, :, None, :], (1, 1, G, 1)).reshape(2, 1, CMP_BLOCK * G * HD)
    w1 = cmp_w1.reshape(2, CMP_BLOCK, 1, HD, HD).astype(BF16)
    w1_bd = _block_diag(jnp.tile(w1, (1, 1, G, 1, 1))).reshape(2, CMP_BLOCK * G * HD, G * HD)
    w2_bd = _block_diag(jnp.tile(cmp_w2[:, None].astype(BF16), (1, G, 1, 1)))
    return pe, w1_bd, w2_bd


def kernel(x, ln_g, ln_b, ffn_wg, ffn_wu, ffn_wd, w_in, w_out, cmp_pe, cmp_w1, cmp_w2,
           pool_w, pool_scale, gla_wa2, gla_ba, gla_norm_g):
    batch, seq, _ = x.shape
    xf = x.reshape(batch * seq, D_MODEL)
    w_in_t = jnp.swapaxes(w_in, 1, 2)
    for l in range(DEPTH):
        lg = lambda i: ln_g[l, i].reshape(1, D_MODEL)
        lb = lambda i: ln_b[l, i].reshape(1, D_MODEL)
        xf = _ffn_ln(xf, ffn_wg, ffn_wu, ffn_wd, lg(0), lb(0), l, 0)
        h, h_kv = _in_proj(xf, w_in_t, l)
        pe_rows, w1_bd, w2_bd = _prep_compress(cmp_pe[l], cmp_w1[l], cmp_w2[l])
        kc, vc = _compress(h_kv, pe_rows, w1_bd, w2_bd, batch, seq)
        o_a = _nsa(h, kc, vc, batch, seq)
        wa2_pad = jnp.pad(gla_wa2[l], ((0, LANES - GLA_GATE_RANK), (0, 0))).astype(BF16)
        o_c, o_b = _gla_pool(h, wa2_pad, gla_ba[l].reshape(1, GLA_QK), gla_norm_g[l].reshape(1, GLA_OUT),
                             _block_diag(pool_w[l]).astype(BF16), pool_scale[l].reshape(1, POOL_DIM), batch, seq)
        wo = w_out[l].astype(BF16)
        xf = _out_proj_ln(xf, o_a, o_b, o_c, wo[:NSA_OUT], wo[NSA_OUT:NSA_OUT + POOL_DIM],
                          wo[NSA_OUT + POOL_DIM:], lg(1), lb(1))
        xf = _ffn_ln(xf, ffn_wg, ffn_wu, ffn_wd, lg(2), lb(2), l, 1)
    return xf.reshape(batch, seq, D_MODEL)
```
